```python
import math
import jax
import jax.numpy as jnp
from jax import lax
import numpy as np

D_MODEL = 1024
BATCH = 32
SEQ = 256
DEPTH = 2
DEC_BATCH = 4
DEC_SEQ = 4096
PAST_LEN = 512

GRID_W = 64
CONV_W = 256
CONV_K = 31
NA_HEADS = 4
NA_HEAD_DIM = 64
NA_W = NA_HEADS * NA_HEAD_DIM
NA_ROWS_MAX = 8
NA_COLS = 16
NA_QB = 16
NA_KB = NA_QB + NA_COLS
NA_CTX_QB = 128
DN_HEADS = 4
DN_DK = 128
DN_DV = 128
DN_QK_W = DN_HEADS * DN_DK
DN_V_W = DN_HEADS * DN_DV
DN_CONV_K = 5
DN_CHUNK = 64
ROPE_BASE = 10000.0
MIX_W = CONV_W + NA_W + DN_V_W
OFF_CONV = 0
OFF_NA = OFF_CONV + 2 * CONV_W
OFF_DN_QKV = OFF_NA + 3 * NA_W
OFF_DN_Z = OFF_DN_QKV + 2 * DN_QK_W + DN_V_W
OFF_DN_AB = OFF_DN_Z + DN_V_W
PROJ_W = OFF_DN_AB + 4 * DN_HEADS
N_GROUPS = 4
EXPERTS_PER_GROUP = 4
N_EXPERTS = N_GROUPS * EXPERTS_PER_GROUP
TOP_K = 2
D_EXPERT = 256
EPS = 1e-6
NEG_INF = -1e30

kernel_name = "hybrid_conv_natten_deltanet_hmoe_prefix_step"


def rmsnorm(x, g):
    xf = x.astype(jnp.float32)
    y = xf * lax.rsqrt(jnp.mean(xf * xf, axis=-1, keepdims=True) + EPS)
    return (y * g.astype(jnp.float32)).astype(x.dtype)


def layernorm(x, g, b):
    xf = x.astype(jnp.float32)
    mu = jnp.mean(xf, axis=-1, keepdims=True)
    xc = xf - mu
    var = jnp.mean(xc * xc, axis=-1, keepdims=True)
    return (xc * lax.rsqrt(var + EPS) * g.astype(jnp.float32) + b.astype(jnp.float32)).astype(x.dtype)


def l2norm(x):
    return x * lax.rsqrt(jnp.sum(x * x, axis=-1, keepdims=True) + EPS)


def depthwise_conv(x, w):
    pad = (w.shape[0] - 1) // 2
    return lax.conv_general_dilated(x, w[:, None, :].astype(x.dtype), (1,), [(pad, pad)],
                                    dimension_numbers=('NWC', 'WIO', 'NWC'),
                                    feature_group_count=x.shape[-1])


def adaln(cond, w_mod, b_mod):
    m = jax.nn.silu(cond) @ w_mod + b_mod
    return jnp.split(m[:, None, :], 6, axis=-1)


def modulated_norm(x, g, shift, scale):
    return rmsnorm(x, g) * (1 + scale) + shift


def split_projection(p):
    b, t, _ = p.shape
    glu = p[..., OFF_CONV:OFF_NA]
    na = p[..., OFF_NA:OFF_DN_QKV].reshape(b, t, 3, NA_HEADS, NA_HEAD_DIM)
    return (glu, na[:, :, 0], na[:, :, 1], na[:, :, 2],
            p[..., OFF_DN_QKV:OFF_DN_Z], p[..., OFF_DN_Z:OFF_DN_AB], p[..., OFF_DN_AB:PROJ_W])


def conformer_conv(glu, conv_w, conv_b, ln_g, ln_b):
    u = glu[..., :CONV_W] * jax.nn.sigmoid(glu[..., CONV_W:])
    u = depthwise_conv(u, conv_w) + conv_b
    return jax.nn.silu(layernorm(u, ln_g, ln_b))


def context_attention(q, k, v):
    b, s, h, d = q.shape
    qb = (q * d ** -0.5).reshape(b, s // NA_CTX_QB, NA_CTX_QB, h, d).transpose(1, 0, 2, 3, 4)

    def one_block(qblk):
        sc = jnp.einsum('bqhd,bkhd->bhqk', qblk, k).astype(jnp.float32)
        pr = jax.nn.softmax(sc, axis=-1).astype(v.dtype)
        return jnp.einsum('bhqk,bkhd->bqhd', pr, v)

    o = lax.map(one_block, qb)
    return o.transpose(1, 0, 2, 3, 4).reshape(b, s, h * d)


def na_indices(rows):
    kr = min(NA_ROWS_MAX, rows)
    r = np.arange(rows)
    rs = np.clip(r - kr // 2, 0, rows - kr)
    row_idx = rs[:, None] + np.arange(kr)[None, :]
    row_off = row_idx - r[:, None] + NA_ROWS_MAX - 1
    n_cb = GRID_W // NA_QB
    j = np.arange(n_cb)
    kc0 = np.clip(j * NA_QB - NA_COLS // 2, 0, GRID_W - NA_KB)
    col_idx = kc0[:, None] + np.arange(NA_KB)[None, :]
    qcol = j[:, None] * NA_QB + np.arange(NA_QB)[None, :]
    cs = np.clip(qcol - NA_COLS // 2, 0, GRID_W - NA_COLS)[..., None]
    kcol = col_idx[:, None, :]
    col_mask = (kcol >= cs) & (kcol < cs + NA_COLS)
    col_off = np.clip(kcol - qcol[..., None] + NA_COLS - 1, 0, 2 * NA_COLS - 2)
    return kr, row_idx, row_off, col_idx, col_mask, col_off


def neighbourhood_attention(q, k, v, k_ctx, v_ctx, rel_bias):
    b, t, h, d = q.shape
    rows = t // GRID_W
    n_cb = GRID_W // NA_QB
    s_ctx_len = k_ctx.shape[1]
    kr, row_idx, row_off, col_idx, col_mask, col_off = na_indices(rows)
    q = q * d ** -0.5
    qb = q.reshape(b, rows, n_cb, NA_QB, h, d)
    kg = k.reshape(b, rows, GRID_W, h, d)
    vg = v.reshape(b, rows, GRID_W, h, d)
    ri = jnp.asarray(row_idx)[:, :, None, None]
    ci = jnp.asarray(col_idx)[None, None, :, :]
    kb = kg[:, ri, ci]
    vb = vg[:, ri, ci]
    s_loc = jnp.einsum('brjqhd,brkjchd->bhrjqkc', qb, kb).astype(jnp.float32)
    bias = rel_bias[:, jnp.asarray(row_off)[:, None, None, :, None],
                    jnp.asarray(col_off)[None, :, :, None, :]]
    s_loc = s_loc + bias[None].astype(jnp.float32)
    s_loc = jnp.where(jnp.asarray(col_mask)[None, None, None, :, :, None, :], s_loc, NEG_INF)
    s_loc = s_loc.reshape(b, h, rows, n_cb, NA_QB, kr * NA_KB)
    s_ctx = jnp.einsum('bthd,bshd->bhts', q, k_ctx).astype(jnp.float32)
    s_ctx = s_ctx.reshape(b, h, rows, n_cb, NA_QB, s_ctx_len)
    pr = jax.nn.softmax(jnp.concatenate([s_loc, s_ctx], axis=-1), axis=-1)
    p_loc = pr[..., :kr * NA_KB].reshape(b, h, rows, n_cb, NA_QB, kr, NA_KB).astype(v.dtype)
    p_ctx = pr[..., kr * NA_KB:].reshape(b, h, t, s_ctx_len).astype(v.dtype)
    o_loc = jnp.einsum('bhrjqkc,brkjchd->brjqhd', p_loc, vb).reshape(b, t, h * d)
    o_ctx = jnp.einsum('bhts,bshd->bthd', p_ctx, v_ctx).reshape(b, t, h * d)
    return o_loc + o_ctx


def axial_rope_tables(t):
    pos = np.arange(t)
    n_freq = DN_DK // 4
    inv = ROPE_BASE ** (-np.arange(n_freq) / n_freq)
    ang = np.stack([(pos // GRID_W)[:, None] * inv[None, :],
                    (pos % GRID_W)[:, None] * inv[None, :]], axis=1)
    return jnp.asarray(np.cos(ang), dtype=jnp.float32), jnp.asarray(np.sin(ang), dtype=jnp.float32)


def apply_axial_rope(x, cos, sin):
    b, t, h, d = x.shape
    xr = x.reshape(b, t, h, 2, 2, d // 4)
    x1, x2 = xr[..., 0, :], xr[..., 1, :]
    cs, sn = cos[None, :, None], sin[None, :, None]
    return jnp.stack([x1 * cs - x2 * sn, x2 * cs + x1 * sn], axis=-2).reshape(b, t, h, d)


def chunk_gated_delta(q, k, v, g, beta, s0):
    b, t, h, dk = q.shape
    dv = v.shape[-1]
    n = t // DN_CHUNK

    def blk(a):
        return jnp.moveaxis(a.reshape(b, n, DN_CHUNK, h, -1), 3, 1)

    q, k, v = blk(q), blk(k), blk(v)
    gc = jnp.cumsum(blk(g[..., None])[..., 0], axis=-1)
    be = blk(beta[..., None])[..., 0]
    tri = jnp.tril(jnp.ones((DN_CHUNK, DN_CHUNK), dtype=bool))
    strict = jnp.tril(jnp.ones((DN_CHUNK, DN_CHUNK), dtype=bool), -1)
    diff = gc[..., :, None] - gc[..., None, :]
    decay = jnp.where(tri, jnp.exp(jnp.where(tri, diff, 0.0)), 0.0)
    kb = k * be[..., None]
    a_mat = jnp.where(strict, jnp.einsum('bhnid,bhnjd->bhnij', kb, k) * decay, 0.0)
    rhs = jnp.concatenate([v * be[..., None], kb * jnp.exp(gc)[..., None]], axis=-1)
    sol = lax.linalg.triangular_solve(a_mat + jnp.eye(DN_CHUNK, dtype=a_mat.dtype), rhs,
                                      left_side=True, lower=True, unit_diagonal=True)
    u_base, w_dec = sol[..., :dv], sol[..., dv:]
    a_qk = jnp.einsum('bhnid,bhnjd->bhnij', q, k) * decay
    q_dec = q * jnp.exp(gc)[..., None]
    k_dec = k * jnp.exp(gc[..., -1:] - gc)[..., None]
    c_dec = jnp.exp(gc[..., -1])

    def step(s, inp):
        ub, wd, aqk, qd, kd, cd = inp
        u = ub - jnp.einsum('bhcd,bhde->bhce', wd, s)
        o = jnp.einsum('bhcd,bhde->bhce', qd, s) + jnp.einsum('bhij,bhje->bhie', aqk, u)
        s = s * cd[..., None, None] + jnp.einsum('bhcd,bhce->bhde', kd, u)
        return s, o

    mv = lambda a: jnp.moveaxis(a, 2, 0)
    s_fin, o = lax.scan(step, s0, (mv(u_base), mv(w_dec), mv(a_qk), mv(q_dec), mv(k_dec), mv(c_dec)))
    o = o.transpose(1, 0, 3, 2, 4).reshape(b, t, h, dv)
    return o, s_fin


def gated_deltanet(dn_qkv, dn_z, dn_ab, conv_w, a_log, dt_bias, norm_g, s_fwd, s_bwd, rope):
    b, t, _ = dn_qkv.shape
    qkv = jax.nn.silu(depthwise_conv(dn_qkv, conv_w)).astype(jnp.float32)
    q = l2norm(qkv[..., :DN_QK_W].reshape(b, t, DN_HEADS, DN_DK))
    k = l2norm(qkv[..., DN_QK_W:2 * DN_QK_W].reshape(b, t, DN_HEADS, DN_DK))
    v = qkv[..., 2 * DN_QK_W:].reshape(b, t, DN_HEADS, DN_DV)
    if rope is not None:
        q = apply_axial_rope(q, rope[0], rope[1])
        k = apply_axial_rope(k, rope[0], rope[1])
    q = q * DN_DK ** -0.5
    ab = dn_ab.astype(jnp.float32).reshape(b, t, 4, DN_HEADS)
    g = -jnp.exp(a_log.astype(jnp.float32)) * jax.nn.softplus(ab[:, :, 0:2] + dt_bias.astype(jnp.float32))
    beta = jax.nn.sigmoid(ab[:, :, 2:4])
    o_f, s_f = chunk_gated_delta(q, k, v, g[:, :, 0], beta[:, :, 0], s_fwd.astype(jnp.float32))
    fl = lambda a: jnp.flip(a, axis=1)
    o_b, s_b = chunk_gated_delta(fl(q), fl(k), fl(v), fl(g[:, :, 1]), fl(beta[:, :, 1]),
                                 s_bwd.astype(jnp.float32))
    o = rmsnorm(o_f + fl(o_b), norm_g) * jax.nn.silu(dn_z.astype(jnp.float32).reshape(b, t, DN_HEADS, DN_DV))
    return o.reshape(b, t, DN_V_W).astype(dn_qkv.dtype), jnp.stack([s_f, s_b], axis=1)


def hierarchical_moe(h, wg, bg, we, be, w1, w3, w2):
    b, t, d = h.shape
    x = h.reshape(b * t, d)
    pg = jax.nn.softmax((x @ wg).astype(jnp.float32) + bg.astype(jnp.float32), axis=-1)
    pg_sel, g_sel = lax.top_k(pg, 1)
    le = ((x @ we).astype(jnp.float32) + be.astype(jnp.float32)).reshape(-1, N_GROUPS, EXPERTS_PER_GROUP)
    le_sel = jnp.take_along_axis(le, g_sel[:, :, None], axis=1)[:, 0]
    w_top, i_top = lax.top_k(jax.nn.softmax(le_sel, axis=-1), TOP_K)
    w_top = w_top / jnp.sum(w_top, axis=-1, keepdims=True) * pg_sel
    e_idx = g_sel * EXPERTS_PER_GROUP + i_top
    gates = jnp.sum(jax.nn.one_hot(e_idx, N_EXPERTS, dtype=jnp.float32) * w_top[..., None], axis=1)
    hid = jax.nn.silu(jnp.einsum('nd,edf->nef', x, w1)) * jnp.einsum('nd,edf->nef', x, w3)
    y = jnp.einsum('nef,efd->nd', hid * gates[:, :, None].astype(hid.dtype), w2)
    return y.reshape(b, t, d)


def setup_inputs(seed: int = 0) -> dict:
    key = jax.random.key(seed)
    ks = jax.random.split(key, 32)
    nrm = lambda k, shape, s: jax.random.normal(k, shape, jnp.float32) * s
    L, D = DEPTH, D_MODEL
    dt = jnp.exp(jax.random.uniform(ks[19], (L, 2, DN_HEADS), jnp.float32, math.log(1e-3), math.log(0.1)))
    return {
        'x_prompt': nrm(ks[0], (BATCH, SEQ, D), 1.0),
        'x_sample': nrm(ks[1], (DEC_BATCH, DEC_SEQ, D), 1.0),
        'cache_na_k': nrm(ks[2], (DEC_BATCH, L, PAST_LEN, NA_HEADS, NA_HEAD_DIM), 1.0),
        'cache_na_v': nrm(ks[3], (DEC_BATCH, L, PAST_LEN, NA_HEADS, NA_HEAD_DIM), 1.0),
        'state_delta': nrm(ks[4], (DEC_BATCH, L, 2, DN_HEADS, DN_DK, DN_DV), 0.1),
        'c': nrm(ks[5], (DEC_BATCH, D), 1.0),
        'c_ctx': nrm(ks[6], (D,), 1.0),
        'w_mod': nrm(ks[7], (L, D, 6 * D), 0.5 * D ** -0.5),
        'b_mod': nrm(ks[8], (L, 6 * D), 0.01),
        'g_norm1': 1.0 + nrm(ks[9], (L, D), 0.01),
        'g_norm2': 1.0 + nrm(ks[10], (L, D), 0.01),
        'w_in': nrm(ks[11], (L, D, PROJ_W), D ** -0.5),
        'conv_a_w': nrm(ks[12], (L, CONV_K, CONV_W), CONV_K ** -0.5),
        'conv_a_b': nrm(ks[13], (L, CONV_W), 0.01),
        'ln_a_g': 1.0 + nrm(ks[14], (L, CONV_W), 0.01),
        'ln_a_b': nrm(ks[15], (L, CONV_W), 0.01),
        'na_rel_bias': nrm(ks[16], (L, NA_HEADS, 2 * NA_ROWS_MAX - 1, 2 * NA_COLS - 1), 0.2),
        'dn_conv_w': nrm(ks[17], (L, DN_CONV_K, 2 * DN_QK_W + DN_V_W), DN_CONV_K ** -0.5),
        'dn_a_log': jnp.log(jax.random.uniform(ks[18], (L, 2, DN_HEADS), jnp.float32, 1.0, 16.0)),
        'dn_dt_bias': dt + jnp.log(-jnp.expm1(-dt)),
        'dn_norm_g': 1.0 + nrm(ks[20], (L, DN_DV), 0.01),
        'w_out': nrm(ks[21], (L, MIX_W, D), MIX_W ** -0.5),
        'router_wg': nrm(ks[22], (L, D, N_GROUPS), D ** -0.5),
        'router_bg': nrm(ks[23], (L, N_GROUPS), 0.01),
        'router_we': nrm(ks[24], (L, D, N_EXPERTS), D ** -0.5),
        'router_be': nrm(ks[25], (L, N_EXPERTS), 0.01),
        'w1': nrm(ks[26], (L, N_EXPERTS, D, D_EXPERT), D ** -0.5),
        'w3': nrm(ks[27], (L, N_EXPERTS, D, D_EXPERT), D ** -0.5),
        'w2': nrm(ks[28], (L, N_EXPERTS, D_EXPERT, D), D_EXPERT ** -0.5),
        'g_final': 1.0 + nrm(ks[29], (D,), 0.01),
    }


def reference(x_prompt, x_sample, cache_na_k, cache_na_v, state_delta, c, c_ctx, w_mod, b_mod,
              g_norm1, g_norm2, w_in, conv_a_w, conv_a_b, ln_a_g, ln_a_b, na_rel_bias, dn_conv_w,
              dn_a_log, dn_dt_bias, dn_norm_g, w_out, router_wg, router_bg, router_we, router_be,
              w1, w3, w2, g_final):
    xc = x_prompt
    new_k, new_v, new_s = [], [], []
    for l in range(DEPTH):
        sh1, sc1, ga1, sh2, sc2, ga2 = adaln(c_ctx[None, :], w_mod[l], b_mod[l])
        p = modulated_norm(xc, g_norm1[l], sh1, sc1) @ w_in[l]
        glu, q_na, k_na, v_na, dn_qkv, dn_z, dn_ab = split_projection(p)
        out_a = conformer_conv(glu, conv_a_w[l], conv_a_b[l], ln_a_g[l], ln_a_b[l])
        out_b = context_attention(q_na, k_na, v_na)
        s0 = jnp.zeros((xc.shape[0], DN_HEADS, DN_DK, DN_DV), jnp.float32)
        out_c, s_ctx = gated_deltanet(dn_qkv, dn_z, dn_ab, dn_conv_w[l], dn_a_log[l], dn_dt_bias[l],
                                      dn_norm_g[l], s0, s0, None)
        xc = xc + ga1 * (jnp.concatenate([out_a, out_b, out_c], axis=-1) @ w_out[l])
        xc = xc + ga2 * hierarchical_moe(modulated_norm(xc, g_norm2[l], sh2, sc2), router_wg[l], router_bg[l],
                                         router_we[l], router_be[l], w1[l], w3[l], w2[l])
        new_k.append(k_na)
        new_v.append(v_na)
        new_s.append(s_ctx)
    y_prompt = rmsnorm(xc, g_final)
    new_cache_na_k = jnp.stack(new_k, axis=1)
    new_cache_na_v = jnp.stack(new_v, axis=1)
    new_state_delta = jnp.stack(new_s, axis=1)

    xs = x_sample
    rope = axial_rope_tables(xs.shape[1])
    for l in range(DEPTH):
        sh1, sc1, ga1, sh2, sc2, ga2 = adaln(c, w_mod[l], b_mod[l])
        p = modulated_norm(xs, g_norm1[l], sh1, sc1) @ w_in[l]
        glu, q_na, k_na, v_na, dn_qkv, dn_z, dn_ab = split_projection(p)
        out_a = conformer_conv(glu, conv_a_w[l], conv_a_b[l], ln_a_g[l], ln_a_b[l])
        out_b = neighbourhood_attention(q_na, k_na, v_na, cache_na_k[:, l], cache_na_v[:, l], na_rel_bias[l])
        out_c, _ = gated_deltanet(dn_qkv, dn_z, dn_ab, dn_conv_w[l], dn_a_log[l], dn_dt_bias[l],
                                  dn_norm_g[l], state_delta[:, l, 0], state_delta[:, l, 1], rope)
        xs = xs + ga1 * (jnp.concatenate([out_a, out_b, out_c], axis=-1) @ w_out[l])
        xs = xs + ga2 * hierarchical_moe(modulated_norm(xs, g_norm2[l], sh2, sc2), router_wg[l], router_bg[l],
                                         router_we[l], router_be[l], w1[l], w3[l], w2[l])
    y_sample = rmsnorm(xs, g_final)
    return (y_prompt, y_sample, new_cache_na_k, new_cache_na_v, new_state_delta)
```

```python
import functools
import math

import numpy as np
import jax
import jax.numpy as jnp
from jax import lax
from jax.experimental import pallas as pl
from jax.experimental.pallas import tpu as pltpu

F32 = jnp.float32
BF16 = jnp.bfloat16

D_MODEL = 1024
DEPTH = 2
GRID_W = 64
CONV_W = 256
CONV_K = 31
NA_HEADS = 4
NA_HEAD_DIM = 64
NA_W = NA_HEADS * NA_HEAD_DIM
NA_ROWS = 8
NA_COLS = 16
DN_HEADS = 4
DN_DK = 128
DN_DV = 128
DN_QK_W = DN_HEADS * DN_DK
DN_V_W = DN_HEADS * DN_DV
DN_CONV_K = 5
DN_CHUNK = 64
ROPE_BASE = 10000.0
MIX_W = CONV_W + NA_W + DN_V_W
OFF_NA = 2 * CONV_W
OFF_DN_QKV = OFF_NA + 3 * NA_W
OFF_DN_Z = OFF_DN_QKV + 2 * DN_QK_W + DN_V_W
OFF_DN_AB = OFF_DN_Z + DN_V_W
PROJ_W = OFF_DN_AB + 4 * DN_HEADS
N_GROUPS = 4
EXPERTS_PER_GROUP = 4
N_EXPERTS = N_GROUPS * EXPERTS_PER_GROUP
D_EXPERT = 256
EPS = 1e-6
NEG_INF = -1e30

LANES = 128
ROW_TILE = 256
DN_TILE = 256
NA_QROWS = 4
NA_WIN_ROWS = 12
MOE_TILE = 1024
VMEM_LIMIT = 48 * 1024 * 1024

_NT = (((1,), (1,)), ((), ()))


def _cparams(n_axes):
    return pltpu.CompilerParams(dimension_semantics=("arbitrary",) * n_axes,
                                vmem_limit_bytes=VMEM_LIMIT)


def _sigmoid(x):
    return 1.0 / (1.0 + jnp.exp(-x))


def _silu(x):
    return x * _sigmoid(x)


def _softplus(x):
    return jnp.maximum(x, 0.0) + jnp.log(1.0 + jnp.exp(-jnp.abs(x)))


def _dot(a, b):
    return jnp.dot(a.astype(BF16), b.astype(BF16), preferred_element_type=F32)


def _dot_nt(a, b):
    return lax.dot_general(a.astype(BF16), b.astype(BF16), _NT, preferred_element_type=F32)


def _adaln_kernel(c_ref, w_ref, b_ref, o_ref):
    c = c_ref[...]
    o_ref[0] = _dot(_silu(c), w_ref[0]) + b_ref[0]


def _adaln(cond8, w_mod_b, b_mod):
    n_l, d, n6 = w_mod_b.shape
    tn = 1536
    return pl.pallas_call(
        _adaln_kernel,
        grid=(n_l, n6 // tn),
        in_specs=[pl.BlockSpec((8, d), lambda l, j: (0, 0)),
                  pl.BlockSpec((1, d, tn), lambda l, j: (l, 0, j)),
                  pl.BlockSpec((1, 1, tn), lambda l, j: (l, 0, j))],
        out_specs=pl.BlockSpec((1, 8, tn), lambda l, j: (l, 0, j)),
        out_shape=jax.ShapeDtypeStruct((n_l, 8, n6), F32),
        compiler_params=_cparams(2),
        name="adaln",
    )(cond8, w_mod_b, b_mod.reshape(n_l, 1, n6))


def _mod_norm(x, g, scale, shift):
    ms = jnp.mean(x * x, axis=-1, keepdims=True)
    return (x * lax.rsqrt(ms + EPS) * g) * (1.0 + scale) + shift


def _inproj_kernel(x_ref, g_ref, sc_ref, sh_ref, w_ref, wabt_ref,
                   glu_ref, q_ref, k_ref, v_ref, dqkv_ref, z_ref, ab_ref, abt_ref):
    hb = _mod_norm(x_ref[0], g_ref[...], sc_ref[0], sh_ref[0]).astype(BF16)

    def proj(a, b):
        return jnp.dot(hb, w_ref[:, a:b], preferred_element_type=F32)

    glu_ref[0] = proj(0, OFF_NA)
    q_ref[0] = proj(OFF_NA, OFF_NA + NA_W)
    k_ref[0] = proj(OFF_NA + NA_W, OFF_NA + 2 * NA_W)
    v_ref[0] = proj(OFF_NA + 2 * NA_W, OFF_DN_QKV)
    dqkv_ref[0] = proj(OFF_DN_QKV, OFF_DN_Z)
    z_ref[0] = proj(OFF_DN_Z, OFF_DN_AB)
    ab_ref[0] = proj(OFF_DN_AB, PROJ_W)
    abt_ref[0] = lax.dot_general(wabt_ref[...], hb, _NT, preferred_element_type=F32)


def _bmap(bm):
    if bm == 1:
        return lambda b, t: (0, 0, 0)
    return lambda b, t: (b, 0, 0)


def _inproj(x, g, scale, shift, w_b, wabt_b):
    bsz, seq, d = x.shape
    tm = ROW_TILE
    widths = (OFF_NA, NA_W, NA_W, NA_W, OFF_DN_Z - OFF_DN_QKV, DN_V_W, 4 * DN_HEADS)
    out_shape = [jax.ShapeDtypeStruct((bsz, seq, w), F32) for w in widths]
    out_shape.append(jax.ShapeDtypeStruct((bsz, 4 * DN_HEADS, seq), F32))
    out_specs = [pl.BlockSpec((1, tm, w), lambda b, t: (b, t, 0)) for w in widths]
    out_specs.append(pl.BlockSpec((1, 4 * DN_HEADS, tm), lambda b, t: (b, 0, t)))
    mm = _bmap(scale.shape[0])
    return pl.pallas_call(
        _inproj_kernel,
        grid=(bsz, seq // tm),
        in_specs=[pl.BlockSpec((1, tm, d), lambda b, t: (b, t, 0)),
                  pl.BlockSpec((1, d), lambda b, t: (0, 0)),
                  pl.BlockSpec((1, 1, d), mm),
                  pl.BlockSpec((1, 1, d), mm),
                  pl.BlockSpec((d, PROJ_W), lambda b, t: (0, 0)),
                  pl.BlockSpec((4 * DN_HEADS, d), lambda b, t: (0, 0))],
        out_specs=out_specs,
        out_shape=out_shape,
        compiler_params=_cparams(2),
        name="inproj",
    )(x, g, scale, shift, w_b, wabt_b)


_CC_HALO = 16
_CC_SUB = 64


def _cconv_kernel(main_ref, prev_ref, next_ref, w_ref, b_ref, g_ref, beta_ref, o_ref, scr, *, tt, nt):
    t = pl.program_id(1)

    def glu(a):
        return a[:, :CONV_W] * _sigmoid(a[:, CONV_W:])

    scr[0:_CC_HALO, :] = jnp.where(t > 0, glu(prev_ref[0]), 0.0)
    scr[_CC_HALO:_CC_HALO + tt, :] = glu(main_ref[0])
    scr[_CC_HALO + tt:2 * _CC_HALO + tt, :] = jnp.where(t < nt - 1, glu(next_ref[0]), 0.0)
    pad = (CONV_K - 1) // 2
    for s in range(tt // _CC_SUB):
        base = s * _CC_SUB + _CC_HALO - pad
        acc = jnp.zeros((_CC_SUB, CONV_W), F32) + b_ref[...]
        for k in range(CONV_K):
            acc = acc + w_ref[k:k + 1, :] * scr[base + k:base + k + _CC_SUB, :]
        mu = jnp.mean(acc, axis=-1, keepdims=True)
        xc = acc - mu
        var = jnp.mean(xc * xc, axis=-1, keepdims=True)
        y = xc * lax.rsqrt(var + EPS) * g_ref[...] + beta_ref[...]
        o_ref[0, s * _CC_SUB:(s + 1) * _CC_SUB, :] = _silu(y)


def _halo_specs(tt, seq, halo, width):
    per = tt // halo
    last = seq // halo - 1
    prev = pl.BlockSpec((1, halo, width), lambda b, t: (b, jnp.maximum(t * per - 1, 0), 0))
    nxt = pl.BlockSpec((1, halo, width), lambda b, t: (b, jnp.minimum((t + 1) * per, last), 0))
    return prev, nxt


def _cconv(glu, w, b, g, beta):
    bsz, seq, _ = glu.shape
    tt = ROW_TILE
    nt = seq // tt
    prev, nxt = _halo_specs(tt, seq, _CC_HALO, 2 * CONV_W)
    vec = pl.BlockSpec((1, CONV_W), lambda b_, t: (0, 0))
    return pl.pallas_call(
        functools.partial(_cconv_kernel, tt=tt, nt=nt),
        grid=(bsz, nt),
        in_specs=[pl.BlockSpec((1, tt, 2 * CONV_W), lambda b_, t: (b_, t, 0)), prev, nxt,
                  pl.BlockSpec((CONV_K, CONV_W), lambda b_, t: (0, 0)), vec, vec, vec],
        out_specs=pl.BlockSpec((1, tt, CONV_W), lambda b_, t: (b_, t, 0)),
        out_shape=jax.ShapeDtypeStruct((bsz, seq, CONV_W), F32),
        scratch_shapes=[pltpu.VMEM((tt + 2 * _CC_HALO, CONV_W), F32)],
        compiler_params=_cparams(2),
        name="conformer_conv",
    )(glu, glu, glu, w, b.reshape(1, -1), g.reshape(1, -1), beta.reshape(1, -1))


def _head_lane_id(shape):
    return lax.shift_right_logical(lax.broadcasted_iota(jnp.int32, shape, 1), 6)


def _ctx_attn_kernel(q_ref, k_ref, v_ref, o_ref):
    q = q_ref[0] * (NA_HEAD_DIM ** -0.5)
    k = k_ref[0].astype(BF16)
    v = v_ref[0].astype(BF16)
    head = _head_lane_id(q.shape)
    out = jnp.zeros(q.shape, F32)
    for h in range(NA_HEADS):
        s = lax.dot_general(jnp.where(head == h, q, 0.0).astype(BF16), k, _NT, preferred_element_type=F32)
        m = jnp.max(s, axis=-1, keepdims=True)
        p = jnp.exp(s - m)
        den = jnp.sum(p, axis=-1, keepdims=True)
        oh = jnp.dot(p.astype(BF16), v, preferred_element_type=F32)
        out = jnp.where(head == h, oh / den, out)
    o_ref[0] = out


def _ctx_attn(q, k, v):
    bsz, seq, w = q.shape
    spec = pl.BlockSpec((1, seq, w), lambda b: (b, 0, 0))
    return pl.pallas_call(
        _ctx_attn_kernel,
        grid=(bsz,),
        in_specs=[spec, spec, spec],
        out_specs=spec,
        out_shape=jax.ShapeDtypeStruct((bsz, seq, w), F32),
        compiler_params=_cparams(1),
        name="context_attention",
    )(q, k, v)


def _na_window_start(blk, rows):
    return jnp.clip(blk * NA_QROWS - NA_ROWS // 2, 0, rows - NA_WIN_ROWS)


def _na_kernel(q_ref, k_ref, v_ref, kc_ref, vc_ref, eb_ref, o_ref, *, rows):
    blk = pl.program_id(1)
    ws = pl.multiple_of(_na_window_start(blk, rows) * GRID_W, GRID_W)
    nwin = NA_WIN_ROWS * GRID_W
    q = q_ref[0] * (NA_HEAD_DIM ** -0.5)
    kl = k_ref[0, pl.ds(ws, nwin), :].astype(BF16)
    vl = v_ref[0, pl.ds(ws, nwin), :].astype(BF16)
    kc = kc_ref[...].astype(BF16)
    vc = vc_ref[...].astype(BF16)
    head = _head_lane_id(q.shape)
    out = jnp.zeros(q.shape, F32)
    for h in range(NA_HEADS):
        qh = jnp.where(head == h, q, 0.0).astype(BF16)
        sl = lax.dot_general(qh, kl, _NT, preferred_element_type=F32) + eb_ref[0, h]
        sc = lax.dot_general(qh, kc, _NT, preferred_element_type=F32)
        m = jnp.maximum(jnp.max(sl, axis=-1, keepdims=True), jnp.max(sc, axis=-1, keepdims=True))
        p_l = jnp.exp(sl - m)
        p_c = jnp.exp(sc - m)
        den = jnp.sum(p_l, axis=-1, keepdims=True) + jnp.sum(p_c, axis=-1, keepdims=True)
        oh = (jnp.dot(p_l.astype(BF16), vl, preferred_element_type=F32)
              + jnp.dot(p_c.astype(BF16), vc, preferred_element_type=F32))
        out = jnp.where(head == h, oh / den, out)
    o_ref[0] = out


def _na_bias_tables(rel_bias, rows):
    nblk = rows // NA_QROWS
    tables = []
    for blk in (0, 1, nblk - 1):
        ws = int(np.clip(blk * NA_QROWS - NA_ROWS // 2, 0, rows - NA_WIN_ROWS))
        qi = np.arange(NA_QROWS * GRID_W)
        kj = np.arange(NA_WIN_ROWS * GRID_W)
        r = blk * NA_QROWS + qi // GRID_W
        c = qi % GRID_W
        kr = ws + kj // GRID_W
        kc = kj % GRID_W
        rs = np.clip(r - NA_ROWS // 2, 0, rows - NA_ROWS)
        cs = np.clip(c - NA_COLS // 2, 0, GRID_W - NA_COLS)
        valid = ((kr[None, :] >= rs[:, None]) & (kr[None, :] < rs[:, None] + NA_ROWS)
                 & (kc[None, :] >= cs[:, None]) & (kc[None, :] < cs[:, None] + NA_COLS))
        row_off = np.clip(kr[None, :] - r[:, None] + NA_ROWS - 1, 0, 2 * NA_ROWS - 2)
        col_off = np.clip(kc[None, :] - c[:, None] + NA_COLS - 1, 0, 2 * NA_COLS - 2)
        tab = rel_bias[:, jnp.asarray(row_off), jnp.asarray(col_off)]
        tables.append(jnp.where(jnp.asarray(valid)[None], tab, NEG_INF))
    return jnp.stack(tables, axis=0)


def _na_attn(q, k, v, cache_k, cache_v, layer, eb):
    bsz, seq, w = q.shape
    rows = seq // GRID_W
    nblk = rows // NA_QROWS
    tq = NA_QROWS * GRID_W
    past = cache_k.shape[2]
    full = pl.BlockSpec((1, seq, w), lambda b, r: (b, 0, 0))
    ctx = pl.BlockSpec((None, None, past, w), lambda b, r: (b, layer, 0, 0))

    def eb_map(b, r):
        return (jnp.where(r == 0, 0, jnp.where(r == nblk - 1, 2, 1)), 0, 0, 0)

    return pl.pallas_call(
        functools.partial(_na_kernel, rows=rows),
        grid=(bsz, nblk),
        in_specs=[pl.BlockSpec((1, tq, w), lambda b, r: (b, r, 0)), full, full, ctx, ctx,
                  pl.BlockSpec((1, NA_HEADS, tq, NA_WIN_ROWS * GRID_W), eb_map)],
        out_specs=pl.BlockSpec((1, tq, w), lambda b, r: (b, r, 0)),
        out_shape=jax.ShapeDtypeStruct((bsz, seq, w), F32),
        compiler_params=_cparams(2),
        name="neighbourhood_attention",
    )(q, k, v, cache_k, cache_v, eb)


_DN_HALO = 8
_DN_RSUB = 128


def _dnprep_kernel(*refs, tt, nt, use_rope):
    if use_rope:
        main_ref, prev_ref, next_ref, w_ref, cos_ref, sin_ref, q_ref, k_ref, v_ref, scr = refs
    else:
        main_ref, prev_ref, next_ref, w_ref, q_ref, k_ref, v_ref, scr = refs
    t = pl.program_id(1)
    scr[0:_DN_HALO, :] = jnp.where(t > 0, prev_ref[0], 0.0)
    scr[_DN_HALO:_DN_HALO + tt, :] = main_ref[0]
    scr[_DN_HALO + tt:2 * _DN_HALO + tt, :] = jnp.where(t < nt - 1, next_ref[0], 0.0)
    pad = (DN_CONV_K - 1) // 2
    outs = (q_ref, k_ref, v_ref)
    if use_rope:
        lane = lax.broadcasted_iota(jnp.int32, (_DN_RSUB, LANES), 1)
        first_half = (lane & (DN_DK // 4)) == 0
    for cb in range(3 * DN_HEADS):
        cols = slice(cb * LANES, (cb + 1) * LANES)
        for rs in range(tt // _DN_RSUB):
            base = rs * _DN_RSUB + _DN_HALO - pad
            acc = jnp.zeros((_DN_RSUB, LANES), F32)
            for kk in range(DN_CONV_K):
                acc = acc + w_ref[kk:kk + 1, cols] * scr[base + kk:base + kk + _DN_RSUB, cols]
            y = _silu(acc)
            if cb < 2 * DN_HEADS:
                y = y * lax.rsqrt(jnp.sum(y * y, axis=-1, keepdims=True) + EPS)
                if use_rope:
                    rsl = slice(rs * _DN_RSUB, (rs + 1) * _DN_RSUB)
                    quarter = DN_DK // 4
                    partner = jnp.where(first_half, pltpu.roll(y, LANES - quarter, 1), pltpu.roll(y, quarter, 1))
                    y = y * cos_ref[rsl, :] + partner * sin_ref[rsl, :]
                if cb < DN_HEADS:
                    y = y * (DN_DK ** -0.5)
            hh = cb % DN_HEADS
            outs[cb // DN_HEADS][0, rs * _DN_RSUB:(rs + 1) * _DN_RSUB, hh * LANES:(hh + 1) * LANES] = y


def _rope_lane_tables(seq):
    pos = np.arange(seq)
    n_freq = DN_DK // 4
    inv = ROPE_BASE ** (-np.arange(n_freq) / n_freq)
    ang = np.stack([(pos // GRID_W)[:, None] * inv[None, :], (pos % GRID_W)[:, None] * inv[None, :]], axis=1)
    cos = np.cos(ang).astype(np.float32)
    sin = np.sin(ang).astype(np.float32)
    lane = np.arange(DN_DK)
    axis, half, freq = lane // (2 * n_freq), (lane // n_freq) % 2, lane % n_freq
    cos_l = cos[:, axis, freq]
    sin_l = sin[:, axis, freq] * np.where(half == 0, -1.0, 1.0).astype(np.float32)[None, :]
    return jnp.asarray(cos_l), jnp.asarray(sin_l)


def _dnprep(dqkv, conv_w, use_rope):
    bsz, seq, width = dqkv.shape
    tt = ROW_TILE
    nt = seq // tt
    prev, nxt = _halo_specs(tt, seq, _DN_HALO, width)
    in_specs = [pl.BlockSpec((1, tt, width), lambda b, t: (b, t, 0)), prev, nxt,
                pl.BlockSpec((DN_CONV_K, width), lambda b, t: (0, 0))]
    args = [dqkv, dqkv, dqkv, conv_w]
    if use_rope:
        tab = pl.BlockSpec((tt, DN_DK), lambda b, t: (t, 0))
        in_specs += [tab, tab]
        args += list(_rope_lane_tables(seq))
    out = jax.ShapeDtypeStruct((bsz, seq, DN_QK_W), F32)
    ospec = pl.BlockSpec((1, tt, DN_QK_W), lambda b, t: (b, t, 0))
    return pl.pallas_call(
        functools.partial(_dnprep_kernel, tt=tt, nt=nt, use_rope=use_rope),
        grid=(bsz, nt),
        in_specs=in_specs,
        out_specs=[ospec, ospec, ospec],
        out_shape=[out, out, out],
        scratch_shapes=[pltpu.VMEM((tt + 2 * _DN_HALO, width), F32)],
        compiler_params=_cparams(2),
        name="deltanet_prep",
    )(*args)


_DN_LEVELS = tuple(range(int(math.log2(DN_CHUNK))))
_CHUNKS = DN_TILE // DN_CHUNK
_LOG_CHUNK = int(math.log2(DN_CHUNK))


def _dn_direction(d, hb, q_ref, k_ref, v_ref, ab_ref, abt_ref, alog_ref, dtb_ref, alogt_ref, dtbt_ref,
                  o_ref, s_scr, *, heads_per_step):
    n = DN_TILE
    ii = lax.broadcasted_iota(jnp.int32, (n, n), 0)
    jj = lax.broadcasted_iota(jnp.int32, (n, n), 1)
    xr = ii ^ jj
    before = (ii > jj) if d == 0 else (ii < jj)
    sx = jnp.where(before, xr, -1)
    same_chunk = lax.shift_right_logical(xr, _LOG_CHUNK) == 0
    strict = lax.shift_right_arithmetic(sx, _LOG_CHUNK) == 0
    incl = strict | (ii == jj)

    ab = ab_ref[0]
    abt = abt_ref[0]
    g_col = -jnp.exp(alog_ref[...]) * _softplus(ab + dtb_ref[...])
    g_row = -jnp.exp(alogt_ref[...]) * _softplus(abt + dtbt_ref[...])
    beta_all = _sigmoid(ab)
    incl_f = incl.astype(F32)
    hi = lax.Precision.HIGHEST
    gc_all = jnp.dot(incl_f, g_col, precision=hi, preferred_element_type=F32)
    tot_all = jnp.dot(same_chunk.astype(F32), g_col, precision=hi, preferred_element_type=F32)
    gct_all = lax.dot_general(g_row, incl_f, _NT, precision=hi, preferred_element_type=F32)
    lane16 = lax.broadcasted_iota(jnp.int32, ab.shape, 1)
    sub16 = lax.broadcasted_iota(jnp.int32, abt.shape, 0)

    for hh in range(heads_per_step):
        head = hb * heads_per_step + hh
        gate_idx = d * DN_HEADS + head
        pick = lane16 == gate_idx
        gc = jnp.sum(jnp.where(pick, gc_all, 0.0), axis=1, keepdims=True)
        tot = jnp.sum(jnp.where(pick, tot_all, 0.0), axis=1, keepdims=True)
        beta = jnp.sum(jnp.where(lane16 == gate_idx + 2 * DN_HEADS, beta_all, 0.0), axis=1, keepdims=True)
        gc_t = jnp.sum(jnp.where(sub16 == gate_idx, gct_all, 0.0), axis=0, keepdims=True)

        cols = slice(hh * LANES, (hh + 1) * LANES)
        q = q_ref[0, :, cols]
        k = k_ref[0, :, cols]
        v = v_ref[0, :, cols]
        kb = k.astype(BF16)
        kk = lax.dot_general(kb, kb, _NT, preferred_element_type=F32)
        qk = lax.dot_general(q.astype(BF16), kb, _NT, preferred_element_type=F32)
        decay = jnp.where(incl, jnp.exp(jnp.where(incl, gc - gc_t, 0.0)), 0.0)
        a_mat = jnp.where(strict, (beta * kk) * decay, 0.0)
        a_qk = qk * decay

        e_mat = -jnp.where(sx == 1, a_mat, 0.0)
        for lb in _DN_LEVELS[1:]:
            l_b = jnp.where(lax.shift_right_arithmetic(sx, lb) == 1, a_mat, 0.0)
            p_mat = l_b + _dot(l_b, e_mat)
            e_mat = e_mat - p_mat - _dot(e_mat, p_mat)

        eg = jnp.exp(gc)
        rhs = jnp.concatenate([v * beta, k * (beta * eg)], axis=1)
        sol = rhs + _dot(e_mat, rhs)
        u_base = sol[:, :DN_DV]
        w_dec = sol[:, DN_DV:]
        q_dec = q * eg
        kd_t = (k * jnp.exp(tot - gc)).T
        c_all = jnp.exp(tot)

        s_h = s_scr[d, hh]
        order = range(_CHUNKS) if d == 0 else range(_CHUNKS - 1, -1, -1)
        for c in order:
            rows = slice(c * DN_CHUNK, (c + 1) * DN_CHUNK)
            ws_qs = _dot(jnp.concatenate([w_dec[rows], q_dec[rows]], axis=0), s_h)
            u = u_base[rows] - ws_qs[:DN_CHUNK]
            pieces = []
            if c > 0:
                pieces.append(jnp.zeros((c * DN_CHUNK, DN_DV), F32))
            pieces.append(u)
            if c < _CHUNKS - 1:
                pieces.append(jnp.zeros(((_CHUNKS - 1 - c) * DN_CHUNK, DN_DV), F32))
            u_full = jnp.concatenate(pieces, axis=0).astype(BF16)
            o_ref[0, rows, cols] = ws_qs[DN_CHUNK:] + jnp.dot(a_qk[rows].astype(BF16), u_full,
                                                             preferred_element_type=F32)
            s_h = s_h * c_all[c * DN_CHUNK:c * DN_CHUNK + 1, :] + jnp.dot(kd_t.astype(BF16), u_full,
                                                                           preferred_element_type=F32)
        s_scr[d, hh] = s_h


def _dnscan_kernel(*refs, nt, heads_per_step, has_s0, want_state):
    it = iter(refs)
    fwd = [next(it) for _ in range(5)]
    bwd = [next(it) for _ in range(5)]
    alog_ref, dtb_ref, alogt_ref, dtbt_ref = (next(it) for _ in range(4))
    s0_ref = next(it) if has_s0 else None
    of_ref = next(it)
    ob_ref = next(it)
    st_ref = next(it) if want_state else None
    s_scr = next(it)
    hb = pl.program_id(1)
    t = pl.program_id(2)

    @pl.when(t == 0)
    def _():
        if has_s0:
            s_scr[...] = s0_ref[0]
        else:
            s_scr[...] = jnp.zeros(s_scr.shape, F32)

    par = (alog_ref, dtb_ref, alogt_ref, dtbt_ref)
    _dn_direction(0, hb, *fwd, *par, of_ref, s_scr, heads_per_step=heads_per_step)
    _dn_direction(1, hb, *bwd, *par, ob_ref, s_scr, heads_per_step=heads_per_step)

    if want_state:
        @pl.when(t == nt - 1)
        def _():
            st_ref[0] = s_scr[...]


def _dnscan(q, k, v, ab, abt, a_log, dt_bias, s0, want_state, heads_per_step=1):
    bsz, seq, _ = q.shape
    nt = seq // DN_TILE
    hs = heads_per_step
    nhb = DN_HEADS // hs
    ngate = 4 * DN_HEADS
    wblk = hs * LANES

    def tile_specs(tmap):
        qs = pl.BlockSpec((1, DN_TILE, wblk), lambda b, h, t: (b, tmap(t), h))
        return [qs, qs, qs,
                pl.BlockSpec((1, DN_TILE, ngate), lambda b, h, t: (b, tmap(t), 0)),
                pl.BlockSpec((1, ngate, DN_TILE), lambda b, h, t: (b, 0, tmap(t)))]

    fwd_map = lambda t: t
    bwd_map = lambda t: nt - 1 - t
    zeros8 = jnp.zeros((2 * DN_HEADS,), F32)
    alog16 = jnp.concatenate([a_log.reshape(-1), zeros8])
    dtb16 = jnp.concatenate([dt_bias.reshape(-1), zeros8])
    prow = pl.BlockSpec((1, ngate), lambda b, h, t: (0, 0))
    pcol = pl.BlockSpec((ngate, 1), lambda b, h, t: (0, 0))
    in_specs = tile_specs(fwd_map) + tile_specs(bwd_map) + [prow, prow, pcol, pcol]
    args = [q, k, v, ab, abt, q, k, v, ab, abt,
            alog16.reshape(1, ngate), dtb16.reshape(1, ngate), alog16.reshape(ngate, 1), dtb16.reshape(ngate, 1)]
    st_block = (1, 2, hs, DN_DK, DN_DV)
    if s0 is not None:
        in_specs.append(pl.BlockSpec(st_block, lambda b, h, t: (b, 0, h, 0, 0)))
        args.append(s0)
    o_shape = jax.ShapeDtypeStruct((bsz, seq, DN_V_W), F32)
    out_specs = [pl.BlockSpec((1, DN_TILE, wblk), lambda b, h, t: (b, t, h)),
                 pl.BlockSpec((1, DN_TILE, wblk), lambda b, h, t: (b, nt - 1 - t, h))]
    out_shape = [o_shape, o_shape]
    if want_state:
        out_specs.append(pl.BlockSpec(st_block, lambda b, h, t: (b, 0, h, 0, 0)))
        out_shape.append(jax.ShapeDtypeStruct((bsz, 2, DN_HEADS, DN_DK, DN_DV), F32))
    return pl.pallas_call(
        functools.partial(_dnscan_kernel, nt=nt, heads_per_step=hs, has_s0=s0 is not None,
                          want_state=want_state),
        grid=(bsz, nhb, nt),
        in_specs=in_specs,
        out_specs=out_specs,
        out_shape=out_shape,
        scratch_shapes=[pltpu.VMEM((2, hs, DN_DK, DN_DV), F32)],
        compiler_params=_cparams(3),
        name="deltanet_scan",
    )(*args)


def _outproj_kernel(x_ref, a_ref, b_ref, of_ref, ob_ref, z_ref, ng_ref, ga_ref, w_ref, o_ref):
    o = of_ref[0] + ob_ref[0]
    z = z_ref[0]
    acc = jnp.dot(a_ref[0].astype(BF16), w_ref[0:CONV_W, :], preferred_element_type=F32)
    acc = acc + jnp.dot(b_ref[0].astype(BF16), w_ref[CONV_W:CONV_W + NA_W, :], preferred_element_type=F32)
    for h in range(DN_HEADS):
        cols = slice(h * DN_DV, (h + 1) * DN_DV)
        oh = o[:, cols]
        y = oh * lax.rsqrt(jnp.mean(oh * oh, axis=-1, keepdims=True) + EPS) * ng_ref[...] * _silu(z[:, cols])
        r0 = CONV_W + NA_W + h * DN_DV
        acc = acc + jnp.dot(y.astype(BF16), w_ref[r0:r0 + DN_DV, :], preferred_element_type=F32)
    o_ref[0] = x_ref[0] + ga_ref[0] * acc


def _outproj(x, out_a, out_b, o_f, o_b, z, norm_g, gate, w_b):
    bsz, seq, d = x.shape
    tm = ROW_TILE

    def row(w):
        return pl.BlockSpec((1, tm, w), lambda b, t: (b, t, 0))

    return pl.pallas_call(
        _outproj_kernel,
        grid=(bsz, seq // tm),
        in_specs=[row(d), row(CONV_W), row(NA_W), row(DN_V_W), row(DN_V_W), row(DN_V_W),
                  pl.BlockSpec((1, DN_DV), lambda b, t: (0, 0)),
                  pl.BlockSpec((1, 1, d), _bmap(gate.shape[0])),
                  pl.BlockSpec((MIX_W, d), lambda b, t: (0, 0))],
        out_specs=row(d),
        out_shape=jax.ShapeDtypeStruct((bsz, seq, d), F32),
        compiler_params=_cparams(2),
        name="outproj",
    )(x, out_a, out_b, o_f, o_b, z, norm_g.reshape(1, -1), gate, w_b)


def _first_argmax(x, lane, valid):
    xm = jnp.where(valid, x, -jnp.inf)
    m = jnp.max(xm, axis=-1, keepdims=True)
    idx = jnp.min(jnp.where(valid & (xm == m), lane, float(LANES)), axis=-1, keepdims=True)
    return m, idx


def _route(h, rw_hi_ref, rw_lo_ref, rb_ref):
    h_hi = h.astype(BF16)
    h_lo = (h - h_hi.astype(F32)).astype(BF16)
    logits = (jnp.dot(h_hi, rw_hi_ref[...], preferred_element_type=F32)
              + jnp.dot(h_lo, rw_hi_ref[...], preferred_element_type=F32)
              + jnp.dot(h_hi, rw_lo_ref[...], preferred_element_type=F32)) + rb_ref[...]
    lane = lax.broadcasted_iota(jnp.int32, logits.shape, 1).astype(F32)
    is_grp = lane < N_GROUPS
    gmax, g_sel = _first_argmax(logits, lane, is_grp)
    pg_sel = 1.0 / jnp.sum(jnp.where(is_grp, jnp.exp(logits - gmax), 0.0), axis=-1, keepdims=True)
    e_lane = lane - N_GROUPS
    in_grp = (e_lane >= g_sel * EXPERTS_PER_GROUP) & (e_lane < (g_sel + 1) * EXPERTS_PER_GROUP)
    m1, i1 = _first_argmax(logits, lane, in_grp)
    m2, i2 = _first_argmax(logits, lane, in_grp & (lane != i1))
    e2 = jnp.exp(m2 - m1)
    w1 = pg_sel / (1.0 + e2)
    w2 = pg_sel * e2 / (1.0 + e2)
    return jnp.where(lane == i1, w1, 0.0) + jnp.where(lane == i2, w2, 0.0)


def _moe_kernel(x_ref, g_ref, sc_ref, sh_ref, ga_ref, rw_hi_ref, rw_lo_ref, rb_ref, w1_ref, w3_ref, w2_ref,
                gf_ref, o_ref, h_scr, gate_scr, acc_scr, *, final_norm):
    e = pl.program_id(1)

    @pl.when(e == 0)
    def _():
        h = _mod_norm(x_ref[...], g_ref[...], sc_ref[0], sh_ref[0])
        h_scr[...] = h.astype(BF16)
        gate_scr[...] = _route(h, rw_hi_ref, rw_lo_ref, rb_ref)
        acc_scr[...] = jnp.zeros(acc_scr.shape, F32)

    hb = h_scr[...]
    gates = gate_scr[...]
    lane = lax.broadcasted_iota(jnp.int32, gates.shape, 1)
    gate_e = jnp.sum(jnp.where(lane == e + N_GROUPS, gates, 0.0), axis=-1, keepdims=True)
    hid = _silu(jnp.dot(hb, w1_ref[0], preferred_element_type=F32)) * jnp.dot(hb, w3_ref[0],
                                                                            preferred_element_type=F32)
    acc_scr[...] += jnp.dot((hid * gate_e).astype(BF16), w2_ref[0], preferred_element_type=F32)

    @pl.when(e == N_EXPERTS - 1)
    def _():
        y = x_ref[...] + ga_ref[0] * acc_scr[...]
        if final_norm:
            y = y * lax.rsqrt(jnp.mean(y * y, axis=-1, keepdims=True) + EPS) * gf_ref[...]
        o_ref[...] = y


def _moe(x, g, scale, shift, gate, rw_hi, rw_lo, rb, w1_b, w3_b, w2_b, g_final, final_norm):
    bsz, seq, d = x.shape
    tm = min(MOE_TILE, seq)
    per_seq = seq // tm
    x2 = x.reshape(bsz * seq, d)
    if scale.shape[0] == 1:
        mm = lambda i, e: (0, 0, 0)
    else:
        mm = lambda i, e: (i // per_seq, 0, 0)
    vec = pl.BlockSpec((1, d), lambda i, e: (0, 0))
    mod = pl.BlockSpec((1, 1, d), mm)
    rspec = pl.BlockSpec((d, LANES), lambda i, e: (0, 0))
    out = pl.pallas_call(
        functools.partial(_moe_kernel, final_norm=final_norm),
        grid=(bsz * seq // tm, N_EXPERTS),
        in_specs=[pl.BlockSpec((tm, d), lambda i, e: (i, 0)), vec, mod, mod, mod,
                  rspec, rspec, pl.BlockSpec((1, LANES), lambda i, e: (0, 0)),
                  pl.BlockSpec((1, d, D_EXPERT), lambda i, e: (e, 0, 0)),
                  pl.BlockSpec((1, d, D_EXPERT), lambda i, e: (e, 0, 0)),
                  pl.BlockSpec((1, D_EXPERT, d), lambda i, e: (e, 0, 0)),
                  vec],
        out_specs=pl.BlockSpec((tm, d), lambda i, e: (i, 0)),
        out_shape=jax.ShapeDtypeStruct((bsz * seq, d), F32),
        scratch_shapes=[pltpu.VMEM((tm, d), BF16), pltpu.VMEM((tm, LANES), F32), pltpu.VMEM((tm, d), F32)],
        compiler_params=_cparams(2),
        name="moe",
    )(x2, g, scale, shift, gate, rw_hi, rw_lo, rb, w1_b, w3_b, w2_b, g_final.reshape(1, -1))
    return out.reshape(bsz, seq, d)


def _split_hi_lo(w):
    hi = w.astype(BF16)
    return hi, (w - hi.astype(F32)).astype(BF16)


def kernel(x_prompt, x_sample, cache_na_k, cache_na_v, state_delta, c, c_ctx, w_mod, b_mod, g_norm1, g_norm2,
           w_in, conv_a_w, conv_a_b, ln_a_g, ln_a_b, na_rel_bias, dn_conv_w, dn_a_log, dn_dt_bias, dn_norm_g,
           w_out, router_wg, router_bg, router_we, router_be, w1, w3, w2, g_final):
    n_dec = x_sample.shape[0]
    d = D_MODEL
    cond8 = jnp.concatenate([c_ctx[None, :], c, jnp.zeros((8 - 1 - n_dec, d), F32)], axis=0)
    mods = _adaln(cond8, w_mod.astype(BF16), b_mod).reshape(DEPTH, 8, 6, d)

    w_in_b = w_in.astype(BF16)
    w_abt_b = jnp.swapaxes(w_in[:, :, OFF_DN_AB:], 1, 2).astype(BF16)
    w_out_b = w_out.astype(BF16)
    w1_b, w3_b, w2_b = w1.astype(BF16), w3.astype(BF16), w2.astype(BF16)
    pad = LANES - N_GROUPS - N_EXPERTS
    rw = jnp.concatenate([router_wg, router_we, jnp.zeros((DEPTH, d, pad), F32)], axis=-1)
    rw_hi, rw_lo = _split_hi_lo(rw)
    rb = jnp.concatenate([router_bg, router_be, jnp.zeros((DEPTH, pad), F32)], axis=-1)

    past = cache_na_k.shape[2]
    cache_k = cache_na_k.reshape(n_dec, DEPTH, past, NA_W)
    cache_v = cache_na_v.reshape(n_dec, DEPTH, past, NA_W)
    rows = x_sample.shape[1] // GRID_W

    def layer(x, l, is_ctx):
        m = mods[l, 0:1] if is_ctx else mods[l, 1:1 + n_dec]
        sh1, sc1, ga1, sh2, sc2, ga2 = (m[:, i:i + 1, :] for i in range(6))
        glu, q_na, k_na, v_na, dqkv, dz, ab, abt = _inproj(x, g_norm1[l:l + 1], sc1, sh1, w_in_b[l], w_abt_b[l])
        out_a = _cconv(glu, conv_a_w[l], conv_a_b[l], ln_a_g[l], ln_a_b[l])
        if is_ctx:
            out_b = _ctx_attn(q_na, k_na, v_na)
        else:
            out_b = _na_attn(q_na, k_na, v_na, cache_k, cache_v, l, _na_bias_tables(na_rel_bias[l], rows))
        dq, dk, dv = _dnprep(dqkv, dn_conv_w[l], use_rope=not is_ctx)
        scan = _dnscan(dq, dk, dv, ab, abt, dn_a_log[l], dn_dt_bias[l],
                       None if is_ctx else state_delta[:, l], want_state=is_ctx)
        o_f, o_b = scan[0], scan[1]
        x = _outproj(x, out_a, out_b, o_f, o_b, dz, dn_norm_g[l], ga1, w_out_b[l])
        x = _moe(x, g_norm2[l:l + 1], sc2, sh2, ga2, rw_hi[l], rw_lo[l], rb[l:l + 1], w1_b[l], w3_b[l], w2_b[l],
                 g_final, final_norm=(l == DEPTH - 1))
        return x, k_na, v_na, (scan[2] if is_ctx else None)

    xc = x_prompt
    new_k, new_v, new_s = [], [], []
    for l in range(DEPTH):
        xc, k_na, v_na, s_ctx = layer(xc, l, True)
        new_k.append(k_na.reshape(k_na.shape[0], k_na.shape[1], NA_HEADS, NA_HEAD_DIM))
        new_v.append(v_na.reshape(v_na.shape[0], v_na.shape[1], NA_HEADS, NA_HEAD_DIM))
        new_s.append(s_ctx)
    xs = x_sample
    for l in range(DEPTH):
        xs, _, _, _ = layer(xs, l, False)
    return (xc, xs, jnp.stack(new_k, axis=1), jnp.stack(new_v, axis=1), jnp.stack(new_s, axis=1))
```

```python
import functools
import math

import numpy as np
import jax
import jax.numpy as jnp
from jax import lax
from jax.experimental import pallas as pl
from jax.experimental.pallas import tpu as pltpu

F32 = jnp.float32
BF16 = jnp.bfloat16

D_MODEL = 1024
DEPTH = 2
GRID_W = 64
CONV_W = 256
CONV_K = 31
NA_HEADS = 4
NA_HEAD_DIM = 64
NA_W = NA_HEADS * NA_HEAD_DIM
NA_ROWS = 8
NA_COLS = 16
DN_HEADS = 4
DN_DK = 128
DN_DV = 128
DN_QK_W = DN_HEADS * DN_DK
DN_V_W = DN_HEADS * DN_DV
DN_CONV_K = 5
DN_CHUNK = 64
ROPE_BASE = 10000.0
MIX_W = CONV_W + NA_W + DN_V_W
OFF_NA = 2 * CONV_W
OFF_DN_QKV = OFF_NA + 3 * NA_W
OFF_DN_Z = OFF_DN_QKV + 2 * DN_QK_W + DN_V_W
OFF_DN_AB = OFF_DN_Z + DN_V_W
PROJ_W = OFF_DN_AB + 4 * DN_HEADS
N_GROUPS = 4
EXPERTS_PER_GROUP = 4
N_EXPERTS = N_GROUPS * EXPERTS_PER_GROUP
D_EXPERT = 256
EPS = 1e-6
NEG_INF = -1e30

LANES = 128
ROW_TILE = 256
DN_TILE = 256
NA_QROWS = 4
NA_WIN_ROWS = 12
MOE_TILE = 1024
VMEM_LIMIT = 48 * 1024 * 1024

_NT = (((1,), (1,)), ((), ()))


def _cparams(n_axes):
    return pltpu.CompilerParams(dimension_semantics=("arbitrary",) * n_axes,
                                vmem_limit_bytes=VMEM_LIMIT)


def _sigmoid(x):
    return 1.0 / (1.0 + jnp.exp(-x))


def _silu(x):
    return x * _sigmoid(x)


def _softplus(x):
    return jnp.maximum(x, 0.0) + jnp.log(1.0 + jnp.exp(-jnp.abs(x)))


def _dot(a, b):
    return jnp.dot(a.astype(BF16), b.astype(BF16), preferred_element_type=F32)


def _adaln_kernel(c_ref, w_ref, b_ref, o_ref):
    c = c_ref[...]
    o_ref[0] = _dot(_silu(c), w_ref[0]) + b_ref[0]


def _adaln(cond8, w_mod_b, b_mod):
    n_l, d, n6 = w_mod_b.shape
    tn = 1536
    return pl.pallas_call(
        _adaln_kernel,
        grid=(n_l, n6 // tn),
        in_specs=[pl.BlockSpec((8, d), lambda l, j: (0, 0)),
                  pl.BlockSpec((1, d, tn), lambda l, j: (l, 0, j)),
                  pl.BlockSpec((1, 1, tn), lambda l, j: (l, 0, j))],
        out_specs=pl.BlockSpec((1, 8, tn), lambda l, j: (l, 0, j)),
        out_shape=jax.ShapeDtypeStruct((n_l, 8, n6), F32),
        compiler_params=_cparams(2),
        name="adaln",
    )(cond8, w_mod_b, b_mod.reshape(n_l, 1, n6))


def _mod_norm(x, g, scale, shift):
    ms = jnp.mean(x * x, axis=-1, keepdims=True)
    return (x * lax.rsqrt(ms + EPS) * g) * (1.0 + scale) + shift


def _inproj_kernel(x_ref, g_ref, sc_ref, sh_ref, w_ref, wabt_ref,
                   glu_ref, q_ref, k_ref, v_ref, dqkv_ref, z_ref, ab_ref, abt_ref):
    hb = _mod_norm(x_ref[0], g_ref[...], sc_ref[0], sh_ref[0]).astype(BF16)

    def proj(a, b):
        return jnp.dot(hb, w_ref[:, a:b], preferred_element_type=F32)

    glu_ref[0] = proj(0, OFF_NA)
    q_ref[0] = proj(OFF_NA, OFF_NA + NA_W)
    k_ref[0] = proj(OFF_NA + NA_W, OFF_NA + 2 * NA_W)
    v_ref[0] = proj(OFF_NA + 2 * NA_W, OFF_DN_QKV)
    dqkv_ref[0] = proj(OFF_DN_QKV, OFF_DN_Z)
    z_ref[0] = proj(OFF_DN_Z, OFF_DN_AB)
    ab_ref[0] = proj(OFF_DN_AB, PROJ_W)
    abt_ref[0] = lax.dot_general(wabt_ref[...], hb, _NT, preferred_element_type=F32)


def _bmap(bm):
    if bm == 1:
        return lambda b, t: (0, 0, 0)
    return lambda b, t: (b, 0, 0)


def _inproj(x, g, scale, shift, w_b, wabt_b):
    bsz, seq, d = x.shape
    tm = ROW_TILE
    widths = (OFF_NA, NA_W, NA_W, NA_W, OFF_DN_Z - OFF_DN_QKV, DN_V_W, 4 * DN_HEADS)
    out_shape = [jax.ShapeDtypeStruct((bsz, seq, w), F32) for w in widths]
    out_shape.append(jax.ShapeDtypeStruct((bsz, 4 * DN_HEADS, seq), F32))
    out_specs = [pl.BlockSpec((1, tm, w), lambda b, t: (b, t, 0)) for w in widths]
    out_specs.append(pl.BlockSpec((1, 4 * DN_HEADS, tm), lambda b, t: (b, 0, t)))
    mm = _bmap(scale.shape[0])
    return pl.pallas_call(
        _inproj_kernel,
        grid=(bsz, seq // tm),
        in_specs=[pl.BlockSpec((1, tm, d), lambda b, t: (b, t, 0)),
                  pl.BlockSpec((1, d), lambda b, t: (0, 0)),
                  pl.BlockSpec((1, 1, d), mm),
                  pl.BlockSpec((1, 1, d), mm),
                  pl.BlockSpec((d, PROJ_W), lambda b, t: (0, 0)),
                  pl.BlockSpec((4 * DN_HEADS, d), lambda b, t: (0, 0))],
        out_specs=out_specs,
        out_shape=out_shape,
        compiler_params=_cparams(2),
        name="inproj",
    )(x, g, scale, shift, w_b, wabt_b)


_CC_HALO = 16
_CC_SUB = 64


def _cconv_kernel(main_ref, prev_ref, next_ref, w_ref, b_ref, g_ref, beta_ref, o_ref, scr, *, tt, nt):
    t = pl.program_id(1)

    def glu(a):
        return a[:, :CONV_W] * _sigmoid(a[:, CONV_W:])

    scr[0:_CC_HALO, :] = jnp.where(t > 0, glu(prev_ref[0]), 0.0)
    scr[_CC_HALO:_CC_HALO + tt, :] = glu(main_ref[0])
    scr[_CC_HALO + tt:2 * _CC_HALO + tt, :] = jnp.where(t < nt - 1, glu(next_ref[0]), 0.0)
    pad = (CONV_K - 1) // 2
    for s in range(tt // _CC_SUB):
        base = s * _CC_SUB + _CC_HALO - pad
        acc = jnp.zeros((_CC_SUB, CONV_W), F32) + b_ref[...]
        for k in range(CONV_K):
            acc = acc + w_ref[k:k + 1, :] * scr[base + k:base + k + _CC_SUB, :]
        mu = jnp.mean(acc, axis=-1, keepdims=True)
        xc = acc - mu
        var = jnp.mean(xc * xc, axis=-1, keepdims=True)
        y = xc * lax.rsqrt(var + EPS) * g_ref[...] + beta_ref[...]
        o_ref[0, s * _CC_SUB:(s + 1) * _CC_SUB, :] = _silu(y)


def _halo_specs(tt, seq, halo, width):
    per = tt // halo
    last = seq // halo - 1
    prev = pl.BlockSpec((1, halo, width), lambda b, t: (b, jnp.maximum(t * per - 1, 0), 0))
    nxt = pl.BlockSpec((1, halo, width), lambda b, t: (b, jnp.minimum((t + 1) * per, last), 0))
    return prev, nxt


def _cconv(glu, w, b, g, beta):
    bsz, seq, _ = glu.shape
    tt = ROW_TILE
    nt = seq // tt
    prev, nxt = _halo_specs(tt, seq, _CC_HALO, 2 * CONV_W)
    vec = pl.BlockSpec((1, CONV_W), lambda b_, t: (0, 0))
    return pl.pallas_call(
        functools.partial(_cconv_kernel, tt=tt, nt=nt),
        grid=(bsz, nt),
        in_specs=[pl.BlockSpec((1, tt, 2 * CONV_W), lambda b_, t: (b_, t, 0)), prev, nxt,
                  pl.BlockSpec((CONV_K, CONV_W), lambda b_, t: (0, 0)), vec, vec, vec],
        out_specs=pl.BlockSpec((1, tt, CONV_W), lambda b_, t: (b_, t, 0)),
        out_shape=jax.ShapeDtypeStruct((bsz, seq, CONV_W), F32),
        scratch_shapes=[pltpu.VMEM((tt + 2 * _CC_HALO, CONV_W), F32)],
        compiler_params=_cparams(2),
        name="conformer_conv",
    )(glu, glu, glu, w, b.reshape(1, -1), g.reshape(1, -1), beta.reshape(1, -1))


def _head_lane_id(shape):
    return lax.shift_right_logical(lax.broadcasted_iota(jnp.int32, shape, 1), 6)


def _ctx_attn_kernel(q_ref, k_ref, v_ref, o_ref):
    q = q_ref[0] * (NA_HEAD_DIM ** -0.5)
    k = k_ref[0].astype(BF16)
    v = v_ref[0].astype(BF16)
    head = _head_lane_id(q.shape)
    out = jnp.zeros(q.shape, F32)
    for h in range(NA_HEADS):
        s = lax.dot_general(jnp.where(head == h, q, 0.0).astype(BF16), k, _NT, preferred_element_type=F32)
        m = jnp.max(s, axis=-1, keepdims=True)
        p = jnp.exp(s - m)
        den = jnp.sum(p, axis=-1, keepdims=True)
        oh = jnp.dot(p.astype(BF16), v, preferred_element_type=F32)
        out = jnp.where(head == h, oh / den, out)
    o_ref[0] = out


def _ctx_attn(q, k, v):
    bsz, seq, w = q.shape
    spec = pl.BlockSpec((1, seq, w), lambda b: (b, 0, 0))
    return pl.pallas_call(
        _ctx_attn_kernel,
        grid=(bsz,),
        in_specs=[spec, spec, spec],
        out_specs=spec,
        out_shape=jax.ShapeDtypeStruct((bsz, seq, w), F32),
        compiler_params=_cparams(1),
        name="context_attention",
    )(q, k, v)


def _na_window_start(blk, rows):
    return jnp.clip(blk * NA_QROWS - NA_ROWS // 2, 0, rows - NA_WIN_ROWS)


def _na_kernel(q_ref, k_ref, v_ref, kc_ref, vc_ref, eb_ref, o_ref, *, rows):
    blk = pl.program_id(1)
    ws = pl.multiple_of(_na_window_start(blk, rows) * GRID_W, GRID_W)
    nwin = NA_WIN_ROWS * GRID_W
    q = q_ref[0] * (NA_HEAD_DIM ** -0.5)
    kl = k_ref[0, pl.ds(ws, nwin), :].astype(BF16)
    vl = v_ref[0, pl.ds(ws, nwin), :].astype(BF16)
    kc = kc_ref[...].astype(BF16)
    vc = vc_ref[...].astype(BF16)
    head = _head_lane_id(q.shape)
    out = jnp.zeros(q.shape, F32)
    for h in range(NA_HEADS):
        qh = jnp.where(head == h, q, 0.0).astype(BF16)
        sl = lax.dot_general(qh, kl, _NT, preferred_element_type=F32) + eb_ref[0, h]
        sc = lax.dot_general(qh, kc, _NT, preferred_element_type=F32)
        m = jnp.maximum(jnp.max(sl, axis=-1, keepdims=True), jnp.max(sc, axis=-1, keepdims=True))
        p_l = jnp.exp(sl - m)
        p_c = jnp.exp(sc - m)
        den = jnp.sum(p_l, axis=-1, keepdims=True) + jnp.sum(p_c, axis=-1, keepdims=True)
        oh = (jnp.dot(p_l.astype(BF16), vl, preferred_element_type=F32)
              + jnp.dot(p_c.astype(BF16), vc, preferred_element_type=F32))
        out = jnp.where(head == h, oh / den, out)
    o_ref[0] = out


def _na_bias_tables(rel_bias, rows):
    n_heads = rel_bias.shape[0]
    c = np.arange(GRID_W)
    cs = np.clip(c - NA_COLS // 2, 0, GRID_W - NA_COLS)
    col_valid = (c[None, :] >= cs[:, None]) & (c[None, :] < cs[:, None] + NA_COLS)
    padw = GRID_W - NA_COLS
    padded = jnp.pad(rel_bias, ((0, 0), (0, 0), (padw, padw)))
    toeplitz = jnp.stack([padded[:, :, GRID_W - 1 - ci:2 * GRID_W - 1 - ci] for ci in range(GRID_W)], axis=2)
    toeplitz = jnp.where(jnp.asarray(col_valid)[None, None], toeplitz, NEG_INF)
    masked = jnp.full((n_heads, GRID_W, GRID_W), NEG_INF, F32)
    nblk = rows // NA_QROWS
    tables = []
    for blk in (0, 1, nblk - 1):
        ws = int(np.clip(blk * NA_QROWS - NA_ROWS // 2, 0, rows - NA_WIN_ROWS))
        q_rows = []
        for a in range(NA_QROWS):
            r = blk * NA_QROWS + a
            rs = int(np.clip(r - NA_ROWS // 2, 0, rows - NA_ROWS))
            blocks = []
            for j in range(NA_WIN_ROWS):
                kr = ws + j
                blocks.append(toeplitz[:, kr - r + NA_ROWS - 1] if rs <= kr < rs + NA_ROWS else masked)
            q_rows.append(jnp.concatenate(blocks, axis=-1))
        tables.append(jnp.concatenate(q_rows, axis=1))
    return jnp.stack(tables, axis=0)


def _na_attn(q, k, v, cache_k, cache_v, layer, eb):
    bsz, seq, w = q.shape
    rows = seq // GRID_W
    nblk = rows // NA_QROWS
    tq = NA_QROWS * GRID_W
    past = cache_k.shape[2]
    full = pl.BlockSpec((1, seq, w), lambda b, r: (b, 0, 0))
    ctx = pl.BlockSpec((None, None, past, w), lambda b, r: (b, layer, 0, 0))

    def eb_map(b, r):
        return (jnp.where(r == 0, 0, jnp.where(r == nblk - 1, 2, 1)), 0, 0, 0)

    return pl.pallas_call(
        functools.partial(_na_kernel, rows=rows),
        grid=(bsz, nblk),
        in_specs=[pl.BlockSpec((1, tq, w), lambda b, r: (b, r, 0)), full, full, ctx, ctx,
                  pl.BlockSpec((1, NA_HEADS, tq, NA_WIN_ROWS * GRID_W), eb_map)],
        out_specs=pl.BlockSpec((1, tq, w), lambda b, r: (b, r, 0)),
        out_shape=jax.ShapeDtypeStruct((bsz, seq, w), F32),
        compiler_params=_cparams(2),
        name="neighbourhood_attention",
    )(q, k, v, cache_k, cache_v, eb)


_DN_HALO = 8
_DN_RSUB = 128


def _dnprep_kernel(*refs, tt, nt, use_rope):
    if use_rope:
        main_ref, prev_ref, next_ref, w_ref, cos_ref, sin_ref, q_ref, k_ref, v_ref, scr = refs
    else:
        main_ref, prev_ref, next_ref, w_ref, q_ref, k_ref, v_ref, scr = refs
    t = pl.program_id(1)
    scr[0:_DN_HALO, :] = jnp.where(t > 0, prev_ref[0], 0.0)
    scr[_DN_HALO:_DN_HALO + tt, :] = main_ref[0]
    scr[_DN_HALO + tt:2 * _DN_HALO + tt, :] = jnp.where(t < nt - 1, next_ref[0], 0.0)
    pad = (DN_CONV_K - 1) // 2
    outs = (q_ref, k_ref, v_ref)
    if use_rope:
        lane = lax.broadcasted_iota(jnp.int32, (_DN_RSUB, LANES), 1)
        first_half = (lane & (DN_DK // 4)) == 0
    for cb in range(3 * DN_HEADS):
        cols = slice(cb * LANES, (cb + 1) * LANES)
        for rs in range(tt // _DN_RSUB):
            base = rs * _DN_RSUB + _DN_HALO - pad
            acc = jnp.zeros((_DN_RSUB, LANES), F32)
            for kk in range(DN_CONV_K):
                acc = acc + w_ref[kk:kk + 1, cols] * scr[base + kk:base + kk + _DN_RSUB, cols]
            y = _silu(acc)
            if cb < 2 * DN_HEADS:
                y = y * lax.rsqrt(jnp.sum(y * y, axis=-1, keepdims=True) + EPS)
                if use_rope:
                    rsl = slice(rs * _DN_RSUB, (rs + 1) * _DN_RSUB)
                    quarter = DN_DK // 4
                    partner = jnp.where(first_half, pltpu.roll(y, LANES - quarter, 1), pltpu.roll(y, quarter, 1))
                    y = y * cos_ref[rsl, :] + partner * sin_ref[rsl, :]
                if cb < DN_HEADS:
                    y = y * (DN_DK ** -0.5)
            hh = cb % DN_HEADS
            outs[cb // DN_HEADS][0, rs * _DN_RSUB:(rs + 1) * _DN_RSUB, hh * LANES:(hh + 1) * LANES] = y


def _rope_lane_tables(seq):
    pos = np.arange(seq)
    n_freq = DN_DK // 4
    inv = ROPE_BASE ** (-np.arange(n_freq) / n_freq)
    ang = np.stack([(pos // GRID_W)[:, None] * inv[None, :], (pos % GRID_W)[:, None] * inv[None, :]], axis=1)
    cos = np.cos(ang).astype(np.float32)
    sin = np.sin(ang).astype(np.float32)
    lane = np.arange(DN_DK)
    axis, half, freq = lane // (2 * n_freq), (lane // n_freq) % 2, lane % n_freq
    cos_l = cos[:, axis, freq]
    sin_l = sin[:, axis, freq] * np.where(half == 0, -1.0, 1.0).astype(np.float32)[None, :]
    return jnp.asarray(cos_l), jnp.asarray(sin_l)


def _dnprep(dqkv, conv_w, use_rope):
    bsz, seq, width = dqkv.shape
    tt = ROW_TILE
    nt = seq // tt
    prev, nxt = _halo_specs(tt, seq, _DN_HALO, width)
    in_specs = [pl.BlockSpec((1, tt, width), lambda b, t: (b, t, 0)), prev, nxt,
                pl.BlockSpec((DN_CONV_K, width), lambda b, t: (0, 0))]
    args = [dqkv, dqkv, dqkv, conv_w]
    if use_rope:
        tab = pl.BlockSpec((tt, DN_DK), lambda b, t: (t, 0))
        in_specs += [tab, tab]
        args += list(_rope_lane_tables(seq))
    out = jax.ShapeDtypeStruct((bsz, seq, DN_QK_W), F32)
    ospec = pl.BlockSpec((1, tt, DN_QK_W), lambda b, t: (b, t, 0))
    return pl.pallas_call(
        functools.partial(_dnprep_kernel, tt=tt, nt=nt, use_rope=use_rope),
        grid=(bsz, nt),
        in_specs=in_specs,
        out_specs=[ospec, ospec, ospec],
        out_shape=[out, out, out],
        scratch_shapes=[pltpu.VMEM((tt + 2 * _DN_HALO, width), F32)],
        compiler_params=_cparams(2),
        name="deltanet_prep",
    )(*args)


_CHUNKS = DN_TILE // DN_CHUNK
_LOG_CHUNK = int(math.log2(DN_CHUNK))
_N_GATES = 4 * DN_HEADS


def _bmm(a, b):
    return lax.dot_general(a.astype(BF16), b.astype(BF16), (((2,), (1,)), ((0,), (0,))),
                           preferred_element_type=F32)


def _bmm_nt(a, b):
    return lax.dot_general(a.astype(BF16), b.astype(BF16), (((2,), (2,)), ((0,), (0,))),
                           preferred_element_type=F32)


def _bmm_tn(a, b):
    return lax.dot_general(a.astype(BF16), b.astype(BF16), (((1,), (1,)), ((0,), (0,))),
                           preferred_element_type=F32)


def _problems(x):
    return jnp.stack([x[c * DN_CHUNK:(c + 1) * DN_CHUNK, h * LANES:(h + 1) * LANES]
                      for c in range(_CHUNKS) for h in range(DN_HEADS)], axis=0)


def _dn_direction(d, q_ref, k_ref, v_ref, ab_ref, abt_ref, alog_ref, dtb_ref, alogt_ref, dtbt_ref, o_ref, s_scr):
    cs = DN_CHUNK
    ii = lax.broadcasted_iota(jnp.int32, (cs, cs), 0)
    jj = lax.broadcasted_iota(jnp.int32, (cs, cs), 1)
    before = (ii > jj) if d == 0 else (ii < jj)
    sx = jnp.where(before, ii ^ jj, -1)
    incl = before | (ii == jj)
    incl_f = incl.astype(F32)
    hi = lax.Precision.HIGHEST

    ab = ab_ref[0]
    abt = abt_ref[0]
    g_col = -jnp.exp(alog_ref[...]) * _softplus(ab + dtb_ref[...])
    g_row = -jnp.exp(alogt_ref[...]) * _softplus(abt + dtbt_ref[...])
    beta_all = _sigmoid(ab)
    gc_l, gct_l, tot_l, beta_l = [], [], [], []
    for c in range(_CHUNKS):
        rows = slice(c * cs, (c + 1) * cs)
        gc_c = jnp.dot(incl_f, g_col[rows], precision=hi, preferred_element_type=F32)
        gct_c = lax.dot_general(g_row[:, rows], incl_f, _NT, precision=hi, preferred_element_type=F32)
        tot_c = jnp.sum(g_col[rows], axis=0, keepdims=True)
        for h in range(DN_HEADS):
            gi = d * DN_HEADS + h
            gc_l.append(gc_c[:, gi:gi + 1])
            gct_l.append(gct_c[gi:gi + 1, :])
            tot_l.append(tot_c[:, gi:gi + 1])
            beta_l.append(beta_all[rows, gi + 2 * DN_HEADS:gi + 2 * DN_HEADS + 1])
    gc = jnp.stack(gc_l, axis=0)
    gc_t = jnp.stack(gct_l, axis=0)
    tot = jnp.stack(tot_l, axis=0)
    beta = jnp.stack(beta_l, axis=0)

    q = _problems(q_ref[0])
    k = _problems(k_ref[0])
    v = _problems(v_ref[0])
    kk = _bmm_nt(k, k)
    qk = _bmm_nt(q, k)
    decay = jnp.where(incl, jnp.exp(jnp.where(incl, gc - gc_t, 0.0)), 0.0)
    a_mat = jnp.where(before, (beta * kk) * decay, 0.0)
    a_qk = qk * decay

    e_mat = -jnp.where(sx == 1, a_mat, 0.0)
    for lb in range(1, _LOG_CHUNK):
        l_b = jnp.where(lax.shift_right_arithmetic(sx, lb) == 1, a_mat, 0.0)
        p_mat = l_b + _bmm(l_b, e_mat)
        e_mat = e_mat - p_mat - _bmm(e_mat, p_mat)

    eg = jnp.exp(gc)
    rhs = jnp.concatenate([v * beta, k * (beta * eg)], axis=-1)
    sol = rhs + _bmm(e_mat, rhs)
    u_base = sol[..., :DN_DV]
    w_dec = sol[..., DN_DV:]
    q_dec = q * eg
    k_dec = k * jnp.exp(tot - gc)
    c_dec = jnp.exp(tot)

    s = s_scr[d]
    order = range(_CHUNKS) if d == 0 else range(_CHUNKS - 1, -1, -1)
    for c in order:
        ps = slice(c * DN_HEADS, (c + 1) * DN_HEADS)
        ws_qs = _bmm(jnp.concatenate([w_dec[ps], q_dec[ps]], axis=1), s)
        u = u_base[ps] - ws_qs[:, :cs]
        o = ws_qs[:, cs:] + _bmm(a_qk[ps], u)
        s = s * c_dec[ps] + _bmm_tn(k_dec[ps], u)
        for h in range(DN_HEADS):
            o_ref[0, c * cs:(c + 1) * cs, h * LANES:(h + 1) * LANES] = o[h]
    s_scr[d] = s


def _dnscan_kernel(*refs, nt, has_s0, want_state):
    it = iter(refs)
    fwd = [next(it) for _ in range(5)]
    bwd = [next(it) for _ in range(5)]
    par = [next(it) for _ in range(4)]
    s0_ref = next(it) if has_s0 else None
    of_ref = next(it)
    ob_ref = next(it)
    st_ref = next(it) if want_state else None
    s_scr = next(it)
    t = pl.program_id(1)

    @pl.when(t == 0)
    def _():
        if has_s0:
            s_scr[...] = s0_ref[0]
        else:
            s_scr[...] = jnp.zeros(s_scr.shape, F32)

    _dn_direction(0, *fwd, *par, of_ref, s_scr)
    _dn_direction(1, *bwd, *par, ob_ref, s_scr)

    if want_state:
        @pl.when(t == nt - 1)
        def _():
            st_ref[0] = s_scr[...]


def _dnscan(q, k, v, ab, abt, a_log, dt_bias, s0, want_state):
    bsz, seq, _ = q.shape
    nt = seq // DN_TILE

    def tile_specs(tmap):
        qs = pl.BlockSpec((1, DN_TILE, DN_V_W), lambda b, t: (b, tmap(t), 0))
        return [qs, qs, qs,
                pl.BlockSpec((1, DN_TILE, _N_GATES), lambda b, t: (b, tmap(t), 0)),
                pl.BlockSpec((1, _N_GATES, DN_TILE), lambda b, t: (b, 0, tmap(t)))]

    zeros8 = jnp.zeros((2 * DN_HEADS,), F32)
    alog16 = jnp.concatenate([a_log.reshape(-1), zeros8])
    dtb16 = jnp.concatenate([dt_bias.reshape(-1), zeros8])
    prow = pl.BlockSpec((1, _N_GATES), lambda b, t: (0, 0))
    pcol = pl.BlockSpec((_N_GATES, 1), lambda b, t: (0, 0))
    in_specs = tile_specs(lambda t: t) + tile_specs(lambda t: nt - 1 - t) + [prow, prow, pcol, pcol]
    args = [q, k, v, ab, abt, q, k, v, ab, abt,
            alog16.reshape(1, _N_GATES), dtb16.reshape(1, _N_GATES),
            alog16.reshape(_N_GATES, 1), dtb16.reshape(_N_GATES, 1)]
    st_block = (1, 2, DN_HEADS, DN_DK, DN_DV)
    if s0 is not None:
        in_specs.append(pl.BlockSpec(st_block, lambda b, t: (b, 0, 0, 0, 0)))
        args.append(s0)
    o_shape = jax.ShapeDtypeStruct((bsz, seq, DN_V_W), F32)
    out_specs = [pl.BlockSpec((1, DN_TILE, DN_V_W), lambda b, t: (b, t, 0)),
                 pl.BlockSpec((1, DN_TILE, DN_V_W), lambda b, t: (b, nt - 1 - t, 0))]
    out_shape = [o_shape, o_shape]
    if want_state:
        out_specs.append(pl.BlockSpec(st_block, lambda b, t: (b, 0, 0, 0, 0)))
        out_shape.append(jax.ShapeDtypeStruct((bsz, 2, DN_HEADS, DN_DK, DN_DV), F32))
    return pl.pallas_call(
        functools.partial(_dnscan_kernel, nt=nt, has_s0=s0 is not None, want_state=want_state),
        grid=(bsz, nt),
        in_specs=in_specs,
        out_specs=out_specs,
        out_shape=out_shape,
        scratch_shapes=[pltpu.VMEM((2, DN_HEADS, DN_DK, DN_DV), F32)],
        compiler_params=_cparams(2),
        name="deltanet_scan",
    )(*args)


def _outproj_kernel(x_ref, a_ref, b_ref, of_ref, ob_ref, z_ref, ng_ref, ga_ref, w_ref, o_ref):
    o = of_ref[0] + ob_ref[0]
    z = z_ref[0]
    acc = jnp.dot(a_ref[0].astype(BF16), w_ref[0:CONV_W, :], preferred_element_type=F32)
    acc = acc + jnp.dot(b_ref[0].astype(BF16), w_ref[CONV_W:CONV_W + NA_W, :], preferred_element_type=F32)
    for h in range(DN_HEADS):
        cols = slice(h * DN_DV, (h + 1) * DN_DV)
        oh = o[:, cols]
        y = oh * lax.rsqrt(jnp.mean(oh * oh, axis=-1, keepdims=True) + EPS) * ng_ref[...] * _silu(z[:, cols])
        r0 = CONV_W + NA_W + h * DN_DV
        acc = acc + jnp.dot(y.astype(BF16), w_ref[r0:r0 + DN_DV, :], preferred_element_type=F32)
    o_ref[0] = x_ref[0] + ga_ref[0] * acc


def _outproj(x, out_a, out_b, o_f, o_b, z, norm_g, gate, w_b):
    bsz, seq, d = x.shape
    tm = ROW_TILE

    def row(w):
        return pl.BlockSpec((1, tm, w), lambda b, t: (b, t, 0))

    return pl.pallas_call(
        _outproj_kernel,
        grid=(bsz, seq // tm),
        in_specs=[row(d), row(CONV_W), row(NA_W), row(DN_V_W), row(DN_V_W), row(DN_V_W),
                  pl.BlockSpec((1, DN_DV), lambda b, t: (0, 0)),
                  pl.BlockSpec((1, 1, d), _bmap(gate.shape[0])),
                  pl.BlockSpec((MIX_W, d), lambda b, t: (0, 0))],
        out_specs=row(d),
        out_shape=jax.ShapeDtypeStruct((bsz, seq, d), F32),
        compiler_params=_cparams(2),
        name="outproj",
    )(x, out_a, out_b, o_f, o_b, z, norm_g.reshape(1, -1), gate, w_b)


def _first_argmax(x, lane, valid):
    xm = jnp.where(valid, x, -jnp.inf)
    m = jnp.max(xm, axis=-1, keepdims=True)
    idx = jnp.min(jnp.where(valid & (xm == m), lane, float(LANES)), axis=-1, keepdims=True)
    return m, idx


def _route(h, rw_hi_ref, rw_lo_ref, rb_ref):
    h_hi = h.astype(BF16)
    h_lo = (h - h_hi.astype(F32)).astype(BF16)
    logits = (jnp.dot(h_hi, rw_hi_ref[...], preferred_element_type=F32)
              + jnp.dot(h_lo, rw_hi_ref[...], preferred_element_type=F32)
              + jnp.dot(h_hi, rw_lo_ref[...], preferred_element_type=F32)) + rb_ref[...]
    lane = lax.broadcasted_iota(jnp.int32, logits.shape, 1).astype(F32)
    is_grp = lane < N_GROUPS
    gmax, g_sel = _first_argmax(logits, lane, is_grp)
    pg_sel = 1.0 / jnp.sum(jnp.where(is_grp, jnp.exp(logits - gmax), 0.0), axis=-1, keepdims=True)
    e_lane = lane - N_GROUPS
    in_grp = (e_lane >= g_sel * EXPERTS_PER_GROUP) & (e_lane < (g_sel + 1) * EXPERTS_PER_GROUP)
    m1, i1 = _first_argmax(logits, lane, in_grp)
    m2, i2 = _first_argmax(logits, lane, in_grp & (lane != i1))
    e2 = jnp.exp(m2 - m1)
    w1 = pg_sel / (1.0 + e2)
    w2 = pg_sel * e2 / (1.0 + e2)
    return jnp.where(lane == i1, w1, 0.0) + jnp.where(lane == i2, w2, 0.0)


def _moe_kernel(x_ref, g_ref, sc_ref, sh_ref, ga_ref, rw_hi_ref, rw_lo_ref, rb_ref, w1_ref, w3_ref, w2_ref,
                gf_ref, o_ref, h_scr, gate_scr, acc_scr, *, final_norm):
    e = pl.program_id(1)

    @pl.when(e == 0)
    def _():
        h = _mod_norm(x_ref[...], g_ref[...], sc_ref[0], sh_ref[0])
        h_scr[...] = h.astype(BF16)
        gate_scr[...] = _route(h, rw_hi_ref, rw_lo_ref, rb_ref)
        acc_scr[...] = jnp.zeros(acc_scr.shape, F32)

    hb = h_scr[...]
    gates = gate_scr[...]
    lane = lax.broadcasted_iota(jnp.int32, gates.shape, 1)
    gate_e = jnp.sum(jnp.where(lane == e + N_GROUPS, gates, 0.0), axis=-1, keepdims=True)
    hid = _silu(jnp.dot(hb, w1_ref[0], preferred_element_type=F32)) * jnp.dot(hb, w3_ref[0],
                                                                            preferred_element_type=F32)
    acc_scr[...] += jnp.dot((hid * gate_e).astype(BF16), w2_ref[0], preferred_element_type=F32)

    @pl.when(e == N_EXPERTS - 1)
    def _():
        y = x_ref[...] + ga_ref[0] * acc_scr[...]
        if final_norm:
            y = y * lax.rsqrt(jnp.mean(y * y, axis=-1, keepdims=True) + EPS) * gf_ref[...]
        o_ref[...] = y


def _moe(x, g, scale, shift, gate, rw_hi, rw_lo, rb, w1_b, w3_b, w2_b, g_final, final_norm):
    bsz, seq, d = x.shape
    x2 = x.reshape(bsz * seq, d)
    if scale.shape[0] == 1:
        tm = MOE_TILE
        mm = lambda i, e: (0, 0, 0)
    else:
        tm = min(MOE_TILE, seq)
        per_seq = seq // tm
        mm = lambda i, e: (i // per_seq, 0, 0)
    vec = pl.BlockSpec((1, d), lambda i, e: (0, 0))
    mod = pl.BlockSpec((1, 1, d), mm)
    rspec = pl.BlockSpec((d, LANES), lambda i, e: (0, 0))
    out = pl.pallas_call(
        functools.partial(_moe_kernel, final_norm=final_norm),
        grid=(bsz * seq // tm, N_EXPERTS),
        in_specs=[pl.BlockSpec((tm, d), lambda i, e: (i, 0)), vec, mod, mod, mod,
                  rspec, rspec, pl.BlockSpec((1, LANES), lambda i, e: (0, 0)),
                  pl.BlockSpec((1, d, D_EXPERT), lambda i, e: (e, 0, 0)),
                  pl.BlockSpec((1, d, D_EXPERT), lambda i, e: (e, 0, 0)),
                  pl.BlockSpec((1, D_EXPERT, d), lambda i, e: (e, 0, 0)),
                  vec],
        out_specs=pl.BlockSpec((tm, d), lambda i, e: (i, 0)),
        out_shape=jax.ShapeDtypeStruct((bsz * seq, d), F32),
        scratch_shapes=[pltpu.VMEM((tm, d), BF16), pltpu.VMEM((tm, LANES), F32), pltpu.VMEM((tm, d), F32)],
        compiler_params=_cparams(2),
        name="moe",
    )(x2, g, scale, shift, gate, rw_hi, rw_lo, rb, w1_b, w3_b, w2_b, g_final.reshape(1, -1))
    return out.reshape(bsz, seq, d)


def _split_hi_lo(w):
    hi = w.astype(BF16)
    return hi, (w - hi.astype(F32)).astype(BF16)


def kernel(x_prompt, x_sample, cache_na_k, cache_na_v, state_delta, c, c_ctx, w_mod, b_mod, g_norm1, g_norm2,
           w_in, conv_a_w, conv_a_b, ln_a_g, ln_a_b, na_rel_bias, dn_conv_w, dn_a_log, dn_dt_bias, dn_norm_g,
           w_out, router_wg, router_bg, router_we, router_be, w1, w3, w2, g_final):
    n_dec = x_sample.shape[0]
    d = D_MODEL
    cond8 = jnp.concatenate([c_ctx[None, :], c, jnp.zeros((8 - 1 - n_dec, d), F32)], axis=0)
    mods = _adaln(cond8, w_mod.astype(BF16), b_mod).reshape(DEPTH, 8, 6, d)

    w_in_b = w_in.astype(BF16)
    w_abt_b = jnp.swapaxes(w_in[:, :, OFF_DN_AB:], 1, 2).astype(BF16)
    w_out_b = w_out.astype(BF16)
    w1_b, w3_b, w2_b = w1.astype(BF16), w3.astype(BF16), w2.astype(BF16)
    pad = LANES - N_GROUPS - N_EXPERTS
    rw = jnp.concatenate([router_wg, router_we, jnp.zeros((DEPTH, d, pad), F32)], axis=-1)
    rw_hi, rw_lo = _split_hi_lo(rw)
    rb = jnp.concatenate([router_bg, router_be, jnp.zeros((DEPTH, pad), F32)], axis=-1)

    past = cache_na_k.shape[2]
    cache_k = cache_na_k.reshape(n_dec, DEPTH, past, NA_W)
    cache_v = cache_na_v.reshape(n_dec, DEPTH, past, NA_W)
    rows = x_sample.shape[1] // GRID_W

    def layer(x, l, is_ctx):
        m = mods[l, 0:1] if is_ctx else mods[l, 1:1 + n_dec]
        sh1, sc1, ga1, sh2, sc2, ga2 = (m[:, i:i + 1, :] for i in range(6))
        glu, q_na, k_na, v_na, dqkv, dz, ab, abt = _inproj(x, g_norm1[l:l + 1], sc1, sh1, w_in_b[l], w_abt_b[l])
        out_a = _cconv(glu, conv_a_w[l], conv_a_b[l], ln_a_g[l], ln_a_b[l])
        if is_ctx:
            out_b = _ctx_attn(q_na, k_na, v_na)
        else:
            out_b = _na_attn(q_na, k_na, v_na, cache_k, cache_v, l, _na_bias_tables(na_rel_bias[l], rows))
        dq, dk, dv = _dnprep(dqkv, dn_conv_w[l], use_rope=not is_ctx)
        scan = _dnscan(dq, dk, dv, ab, abt, dn_a_log[l], dn_dt_bias[l],
                       None if is_ctx else state_delta[:, l], want_state=is_ctx)
        o_f, o_b = scan[0], scan[1]
        x = _outproj(x, out_a, out_b, o_f, o_b, dz, dn_norm_g[l], ga1, w_out_b[l])
        x = _moe(x, g_norm2[l:l + 1], sc2, sh2, ga2, rw_hi[l], rw_lo[l], rb[l:l + 1], w1_b[l], w3_b[l], w2_b[l],
                 g_final, final_norm=(l == DEPTH - 1))
        return x, k_na, v_na, (scan[2] if is_ctx else None)

    xc = x_prompt
    new_k, new_v, new_s = [], [], []
    for l in range(DEPTH):
        xc, k_na, v_na, s_ctx = layer(xc, l, True)
        new_k.append(k_na.reshape(k_na.shape[0], k_na.shape[1], NA_HEADS, NA_HEAD_DIM))
        new_v.append(v_na.reshape(v_na.shape[0], v_na.shape[1], NA_HEADS, NA_HEAD_DIM))
        new_s.append(s_ctx)
    xs = x_sample
    for l in range(DEPTH):
        xs, _, _, _ = layer(xs, l, False)
    return (xc, xs, jnp.stack(new_k, axis=1), jnp.stack(new_v, axis=1), jnp.stack(new_s, axis=1))
```

```python
import functools
import math

import numpy as np
import jax
import jax.numpy as jnp
from jax import lax
from jax.experimental import pallas as pl
from jax.experimental.pallas import tpu as pltpu

F32 = jnp.float32
BF16 = jnp.bfloat16

D_MODEL = 1024
DEPTH = 2
GRID_W = 64
CONV_W = 256
CONV_K = 31
NA_HEADS = 4
NA_HEAD_DIM = 64
NA_W = NA_HEADS * NA_HEAD_DIM
NA_ROWS = 8
NA_COLS = 16
DN_HEADS = 4
DN_DK = 128
DN_DV = 128
DN_QK_W = DN_HEADS * DN_DK
DN_V_W = DN_HEADS * DN_DV
DN_CONV_K = 5
DN_CHUNK = 64
ROPE_BASE = 10000.0
MIX_W = CONV_W + NA_W + DN_V_W
OFF_NA = 2 * CONV_W
OFF_DN_QKV = OFF_NA + 3 * NA_W
OFF_DN_Z = OFF_DN_QKV + 2 * DN_QK_W + DN_V_W
OFF_DN_AB = OFF_DN_Z + DN_V_W
PROJ_W = OFF_DN_AB + 4 * DN_HEADS
N_GROUPS = 4
EXPERTS_PER_GROUP = 4
N_EXPERTS = N_GROUPS * EXPERTS_PER_GROUP
D_EXPERT = 256
EPS = 1e-6
NEG_INF = -1e30

LANES = 128
SUBLANES = 8
ROW_TILE = 256
PROJ_TILE = 512
DN_TILE = 256
NA_QROWS = 4
NA_WIN_ROWS = 12
MOE_TILE = 1024
VMEM_LIMIT = 48 * 1024 * 1024

_NT = (((1,), (1,)), ((), ()))


def _cparams(n_axes):
    return pltpu.CompilerParams(dimension_semantics=("arbitrary",) * n_axes,
                                vmem_limit_bytes=VMEM_LIMIT)


def _sigmoid(x):
    return 1.0 / (1.0 + jnp.exp(-x))


def _silu(x):
    return x * _sigmoid(x)


def _softplus(x):
    return jnp.maximum(x, 0.0) + jnp.log(1.0 + jnp.exp(-jnp.abs(x)))


def _dot(a, b):
    return jnp.dot(a.astype(BF16), b.astype(BF16), preferred_element_type=F32)


def _adaln_kernel(c_ref, w_ref, b_ref, o_ref):
    c = c_ref[...]
    o_ref[0] = _dot(_silu(c), w_ref[0]) + b_ref[0]


def _adaln(cond8, w_mod_b, b_mod):
    n_l, d, n6 = w_mod_b.shape
    tn = 1536
    return pl.pallas_call(
        _adaln_kernel,
        grid=(n_l, n6 // tn),
        in_specs=[pl.BlockSpec((8, d), lambda l, j: (0, 0)),
                  pl.BlockSpec((1, d, tn), lambda l, j: (l, 0, j)),
                  pl.BlockSpec((1, 1, tn), lambda l, j: (l, 0, j))],
        out_specs=pl.BlockSpec((1, 8, tn), lambda l, j: (l, 0, j)),
        out_shape=jax.ShapeDtypeStruct((n_l, 8, n6), F32),
        compiler_params=_cparams(2),
        name="adaln",
    )(cond8, w_mod_b, b_mod.reshape(n_l, 1, n6))


def _mod_norm(x, g, scale, shift):
    ms = jnp.mean(x * x, axis=-1, keepdims=True)
    return (x * lax.rsqrt(ms + EPS) * g) * (1.0 + scale) + shift


def _inproj_kernel(x_ref, g_ref, sc_ref, sh_ref, w_ref, wabt_ref,
                   glu_ref, q_ref, k_ref, v_ref, dqkv_ref, z_ref, ab_ref, abt_ref):
    hb = _mod_norm(x_ref[0], g_ref[...], sc_ref[0], sh_ref[0]).astype(BF16)

    def proj(a, b):
        return jnp.dot(hb, w_ref[:, a:b], preferred_element_type=F32)

    glu_ref[0] = proj(0, OFF_NA)
    q_ref[0] = proj(OFF_NA, OFF_NA + NA_W)
    k_ref[0] = proj(OFF_NA + NA_W, OFF_NA + 2 * NA_W)
    v_ref[0] = proj(OFF_NA + 2 * NA_W, OFF_DN_QKV)
    dqkv_ref[0] = proj(OFF_DN_QKV, OFF_DN_Z)
    z_ref[0] = proj(OFF_DN_Z, OFF_DN_AB)
    ab_ref[0] = proj(OFF_DN_AB, PROJ_W)
    abt_ref[0] = lax.dot_general(wabt_ref[...], hb, _NT, preferred_element_type=F32)


def _bmap(bm):
    if bm == 1:
        return lambda b, t: (0, 0, 0)
    return lambda b, t: (b, 0, 0)


def _merge_short_sequences(x, tm, shared_mod):
    bsz, seq, d = x.shape
    if shared_mod and seq < tm:
        return x.reshape(bsz * seq // tm, tm, d)
    return x


def _inproj(x, g, scale, shift, w_b, wabt_b):
    bsz0, seq0, _ = x.shape
    tm = PROJ_TILE
    x = _merge_short_sequences(x, tm, scale.shape[0] == 1)
    outs = _inproj_call(x, g, scale, shift, w_b, wabt_b, tm)
    if x.shape[0] == bsz0:
        return outs
    per = tm // seq0
    row_outs = [o.reshape(bsz0, seq0, o.shape[-1]) for o in outs[:-1]]
    abt = outs[-1].reshape(x.shape[0], 4 * DN_HEADS, per, seq0)
    abt = jnp.swapaxes(abt, 1, 2).reshape(bsz0, 4 * DN_HEADS, seq0)
    return row_outs + [abt]


def _inproj_call(x, g, scale, shift, w_b, wabt_b, tm):
    bsz, seq, d = x.shape
    widths = (OFF_NA, NA_W, NA_W, NA_W, OFF_DN_Z - OFF_DN_QKV, DN_V_W, 4 * DN_HEADS)
    out_shape = [jax.ShapeDtypeStruct((bsz, seq, w), F32) for w in widths]
    out_shape.append(jax.ShapeDtypeStruct((bsz, 4 * DN_HEADS, seq), F32))
    out_specs = [pl.BlockSpec((1, tm, w), lambda b, t: (b, t, 0)) for w in widths]
    out_specs.append(pl.BlockSpec((1, 4 * DN_HEADS, tm), lambda b, t: (b, 0, t)))
    mm = _bmap(scale.shape[0])
    return pl.pallas_call(
        _inproj_kernel,
        grid=(bsz, seq // tm),
        in_specs=[pl.BlockSpec((1, tm, d), lambda b, t: (b, t, 0)),
                  pl.BlockSpec((1, d), lambda b, t: (0, 0)),
                  pl.BlockSpec((1, 1, d), mm),
                  pl.BlockSpec((1, 1, d), mm),
                  pl.BlockSpec((d, PROJ_W), lambda b, t: (0, 0)),
                  pl.BlockSpec((4 * DN_HEADS, d), lambda b, t: (0, 0))],
        out_specs=out_specs,
        out_shape=out_shape,
        compiler_params=_cparams(2),
        name="inproj",
    )(x, g, scale, shift, w_b, wabt_b)


_CC_HALO = 16
_CC_SUB = 64


def _cconv_kernel(main_ref, prev_ref, next_ref, w_ref, b_ref, g_ref, beta_ref, o_ref, scr, *, tt, nt):
    t = pl.program_id(1)

    def glu(a):
        return a[:, :CONV_W] * _sigmoid(a[:, CONV_W:])

    scr[0, 0:_CC_HALO, :] = jnp.where(t > 0, glu(prev_ref[0]), 0.0)
    scr[0, _CC_HALO:_CC_HALO + tt, :] = glu(main_ref[0])
    scr[0, _CC_HALO + tt:2 * _CC_HALO + tt, :] = jnp.where(t < nt - 1, glu(next_ref[0]), 0.0)
    n_shift = tt + 2 * _CC_HALO - SUBLANES
    for r in range(1, SUBLANES):
        scr[r, 0:n_shift, :] = scr[0, r:r + n_shift, :]
    pad = (CONV_K - 1) // 2
    for s in range(tt // _CC_SUB):
        base = s * _CC_SUB + _CC_HALO - pad
        acc = jnp.zeros((_CC_SUB, CONV_W), F32) + b_ref[...]
        for k in range(CONV_K):
            r = (base + k) % SUBLANES
            a0 = base + k - r
            acc = acc + w_ref[k:k + 1, :] * scr[r, a0:a0 + _CC_SUB, :]
        mu = jnp.mean(acc, axis=-1, keepdims=True)
        xc = acc - mu
        var = jnp.mean(xc * xc, axis=-1, keepdims=True)
        y = xc * lax.rsqrt(var + EPS) * g_ref[...] + beta_ref[...]
        o_ref[0, s * _CC_SUB:(s + 1) * _CC_SUB, :] = _silu(y)


def _halo_specs(tt, seq, halo, width):
    per = tt // halo
    last = seq // halo - 1
    prev = pl.BlockSpec((1, halo, width), lambda b, t: (b, jnp.maximum(t * per - 1, 0), 0))
    nxt = pl.BlockSpec((1, halo, width), lambda b, t: (b, jnp.minimum((t + 1) * per, last), 0))
    return prev, nxt


def _cconv(glu, w, b, g, beta):
    bsz, seq, _ = glu.shape
    tt = ROW_TILE
    nt = seq // tt
    prev, nxt = _halo_specs(tt, seq, _CC_HALO, 2 * CONV_W)
    vec = pl.BlockSpec((1, CONV_W), lambda b_, t: (0, 0))
    return pl.pallas_call(
        functools.partial(_cconv_kernel, tt=tt, nt=nt),
        grid=(bsz, nt),
        in_specs=[pl.BlockSpec((1, tt, 2 * CONV_W), lambda b_, t: (b_, t, 0)), prev, nxt,
                  pl.BlockSpec((CONV_K, CONV_W), lambda b_, t: (0, 0)), vec, vec, vec],
        out_specs=pl.BlockSpec((1, tt, CONV_W), lambda b_, t: (b_, t, 0)),
        out_shape=jax.ShapeDtypeStruct((bsz, seq, CONV_W), F32),
        scratch_shapes=[pltpu.VMEM((SUBLANES, tt + 2 * _CC_HALO, CONV_W), F32)],
        compiler_params=_cparams(2),
        name="conformer_conv",
    )(glu, glu, glu, w, b.reshape(1, -1), g.reshape(1, -1), beta.reshape(1, -1))


def _head_lane_id(shape):
    return lax.shift_right_logical(lax.broadcasted_iota(jnp.int32, shape, 1), 6)


def _ctx_attn_kernel(q_ref, k_ref, v_ref, o_ref):
    q = q_ref[0] * (NA_HEAD_DIM ** -0.5)
    k = k_ref[0].astype(BF16)
    v = v_ref[0].astype(BF16)
    head = _head_lane_id(q.shape)
    out = jnp.zeros(q.shape, F32)
    for h in range(NA_HEADS):
        s = lax.dot_general(jnp.where(head == h, q, 0.0).astype(BF16), k, _NT, preferred_element_type=F32)
        m = jnp.max(s, axis=-1, keepdims=True)
        p = jnp.exp(s - m)
        den = jnp.sum(p, axis=-1, keepdims=True)
        oh = jnp.dot(p.astype(BF16), v, preferred_element_type=F32)
        out = jnp.where(head == h, oh / den, out)
    o_ref[0] = out


def _ctx_attn(q, k, v):
    bsz, seq, w = q.shape
    spec = pl.BlockSpec((1, seq, w), lambda b: (b, 0, 0))
    return pl.pallas_call(
        _ctx_attn_kernel,
        grid=(bsz,),
        in_specs=[spec, spec, spec],
        out_specs=spec,
        out_shape=jax.ShapeDtypeStruct((bsz, seq, w), F32),
        compiler_params=_cparams(1),
        name="context_attention",
    )(q, k, v)


def _na_window_start(blk, rows):
    return jnp.clip(blk * NA_QROWS - NA_ROWS // 2, 0, rows - NA_WIN_ROWS)


def _na_kernel(q_ref, k_ref, v_ref, kc_ref, vc_ref, eb_ref, o_ref, *, rows):
    blk = pl.program_id(1)
    ws = pl.multiple_of(_na_window_start(blk, rows) * GRID_W, GRID_W)
    nwin = NA_WIN_ROWS * GRID_W
    q = q_ref[0] * (NA_HEAD_DIM ** -0.5)
    kl = k_ref[0, pl.ds(ws, nwin), :].astype(BF16)
    vl = v_ref[0, pl.ds(ws, nwin), :].astype(BF16)
    kc = kc_ref[...].astype(BF16)
    vc = vc_ref[...].astype(BF16)
    head = _head_lane_id(q.shape)
    out = jnp.zeros(q.shape, F32)
    for h in range(NA_HEADS):
        qh = jnp.where(head == h, q, 0.0).astype(BF16)
        sl = lax.dot_general(qh, kl, _NT, preferred_element_type=F32) + eb_ref[0, h]
        sc = lax.dot_general(qh, kc, _NT, preferred_element_type=F32)
        m = jnp.maximum(jnp.max(sl, axis=-1, keepdims=True), jnp.max(sc, axis=-1, keepdims=True))
        p_l = jnp.exp(sl - m)
        p_c = jnp.exp(sc - m)
        den = jnp.sum(p_l, axis=-1, keepdims=True) + jnp.sum(p_c, axis=-1, keepdims=True)
        oh = (jnp.dot(p_l.astype(BF16), vl, preferred_element_type=F32)
              + jnp.dot(p_c.astype(BF16), vc, preferred_element_type=F32))
        out = jnp.where(head == h, oh / den, out)
    o_ref[0] = out


def _na_bias_tables(rel_bias, rows):
    n_heads = rel_bias.shape[0]
    c = np.arange(GRID_W)
    cs = np.clip(c - NA_COLS // 2, 0, GRID_W - NA_COLS)
    col_valid = (c[None, :] >= cs[:, None]) & (c[None, :] < cs[:, None] + NA_COLS)
    padw = GRID_W - NA_COLS
    padded = jnp.pad(rel_bias, ((0, 0), (0, 0), (padw, padw)))
    toeplitz = jnp.stack([padded[:, :, GRID_W - 1 - ci:2 * GRID_W - 1 - ci] for ci in range(GRID_W)], axis=2)
    toeplitz = jnp.where(jnp.asarray(col_valid)[None, None], toeplitz, NEG_INF)
    masked = jnp.full((n_heads, GRID_W, GRID_W), NEG_INF, F32)
    nblk = rows // NA_QROWS
    tables = []
    for blk in (0, 1, nblk - 1):
        ws = int(np.clip(blk * NA_QROWS - NA_ROWS // 2, 0, rows - NA_WIN_ROWS))
        q_rows = []
        for a in range(NA_QROWS):
            r = blk * NA_QROWS + a
            rs = int(np.clip(r - NA_ROWS // 2, 0, rows - NA_ROWS))
            blocks = []
            for j in range(NA_WIN_ROWS):
                kr = ws + j
                blocks.append(toeplitz[:, kr - r + NA_ROWS - 1] if rs <= kr < rs + NA_ROWS else masked)
            q_rows.append(jnp.concatenate(blocks, axis=-1))
        tables.append(jnp.concatenate(q_rows, axis=1))
    return jnp.stack(tables, axis=0)


def _na_attn(q, k, v, cache_k, cache_v, layer, eb):
    bsz, seq, w = q.shape
    rows = seq // GRID_W
    nblk = rows // NA_QROWS
    tq = NA_QROWS * GRID_W
    past = cache_k.shape[2]
    full = pl.BlockSpec((1, seq, w), lambda b, r: (b, 0, 0))
    ctx = pl.BlockSpec((None, None, past, w), lambda b, r: (b, layer, 0, 0))

    def eb_map(b, r):
        return (jnp.where(r == 0, 0, jnp.where(r == nblk - 1, 2, 1)), 0, 0, 0)

    return pl.pallas_call(
        functools.partial(_na_kernel, rows=rows),
        grid=(bsz, nblk),
        in_specs=[pl.BlockSpec((1, tq, w), lambda b, r: (b, r, 0)), full, full, ctx, ctx,
                  pl.BlockSpec((1, NA_HEADS, tq, NA_WIN_ROWS * GRID_W), eb_map)],
        out_specs=pl.BlockSpec((1, tq, w), lambda b, r: (b, r, 0)),
        out_shape=jax.ShapeDtypeStruct((bsz, seq, w), F32),
        compiler_params=_cparams(2),
        name="neighbourhood_attention",
    )(q, k, v, cache_k, cache_v, eb)


_DN_HALO = 8
_DN_RSUB = 128


def _dnprep_kernel(*refs, tt, nt, use_rope):
    if use_rope:
        main_ref, prev_ref, next_ref, w_ref, cos_ref, sin_ref, q_ref, k_ref, v_ref, scr = refs
    else:
        main_ref, prev_ref, next_ref, w_ref, q_ref, k_ref, v_ref, scr = refs
    t = pl.program_id(1)
    scr[0:_DN_HALO, :] = jnp.where(t > 0, prev_ref[0], 0.0)
    scr[_DN_HALO:_DN_HALO + tt, :] = main_ref[0]
    scr[_DN_HALO + tt:2 * _DN_HALO + tt, :] = jnp.where(t < nt - 1, next_ref[0], 0.0)
    pad = (DN_CONV_K - 1) // 2
    outs = (q_ref, k_ref, v_ref)
    if use_rope:
        lane = lax.broadcasted_iota(jnp.int32, (_DN_RSUB, LANES), 1)
        first_half = (lane & (DN_DK // 4)) == 0
    for cb in range(3 * DN_HEADS):
        cols = slice(cb * LANES, (cb + 1) * LANES)
        for rs in range(tt // _DN_RSUB):
            base = rs * _DN_RSUB + _DN_HALO - pad
            acc = jnp.zeros((_DN_RSUB, LANES), F32)
            for kk in range(DN_CONV_K):
                acc = acc + w_ref[kk:kk + 1, cols] * scr[base + kk:base + kk + _DN_RSUB, cols]
            y = _silu(acc)
            if cb < 2 * DN_HEADS:
                y = y * lax.rsqrt(jnp.sum(y * y, axis=-1, keepdims=True) + EPS)
                if use_rope:
                    rsl = slice(rs * _DN_RSUB, (rs + 1) * _DN_RSUB)
                    quarter = DN_DK // 4
                    partner = jnp.where(first_half, pltpu.roll(y, LANES - quarter, 1), pltpu.roll(y, quarter, 1))
                    y = y * cos_ref[rsl, :] + partner * sin_ref[rsl, :]
                if cb < DN_HEADS:
                    y = y * (DN_DK ** -0.5)
            hh = cb % DN_HEADS
            outs[cb // DN_HEADS][0, rs * _DN_RSUB:(rs + 1) * _DN_RSUB, hh * LANES:(hh + 1) * LANES] = y


def _rope_lane_tables(seq):
    pos = np.arange(seq)
    n_freq = DN_DK // 4
    inv = ROPE_BASE ** (-np.arange(n_freq) / n_freq)
    ang = np.stack([(pos // GRID_W)[:, None] * inv[None, :], (pos % GRID_W)[:, None] * inv[None, :]], axis=1)
    cos = np.cos(ang).astype(np.float32)
    sin = np.sin(ang).astype(np.float32)
    lane = np.arange(DN_DK)
    axis, half, freq = lane // (2 * n_freq), (lane // n_freq) % 2, lane % n_freq
    cos_l = cos[:, axis, freq]
    sin_l = sin[:, axis, freq] * np.where(half == 0, -1.0, 1.0).astype(np.float32)[None, :]
    return jnp.asarray(cos_l), jnp.asarray(sin_l)


def _dnprep(dqkv, conv_w, use_rope):
    bsz, seq, width = dqkv.shape
    tt = ROW_TILE
    nt = seq // tt
    prev, nxt = _halo_specs(tt, seq, _DN_HALO, width)
    in_specs = [pl.BlockSpec((1, tt, width), lambda b, t: (b, t, 0)), prev, nxt,
                pl.BlockSpec((DN_CONV_K, width), lambda b, t: (0, 0))]
    args = [dqkv, dqkv, dqkv, conv_w]
    if use_rope:
        tab = pl.BlockSpec((tt, DN_DK), lambda b, t: (t, 0))
        in_specs += [tab, tab]
        args += list(_rope_lane_tables(seq))
    out = jax.ShapeDtypeStruct((bsz, seq, DN_QK_W), F32)
    ospec = pl.BlockSpec((1, tt, DN_QK_W), lambda b, t: (b, t, 0))
    return pl.pallas_call(
        functools.partial(_dnprep_kernel, tt=tt, nt=nt, use_rope=use_rope),
        grid=(bsz, nt),
        in_specs=in_specs,
        out_specs=[ospec, ospec, ospec],
        out_shape=[out, out, out],
        scratch_shapes=[pltpu.VMEM((tt + 2 * _DN_HALO, width), F32)],
        compiler_params=_cparams(2),
        name="deltanet_prep",
    )(*args)


_CHUNKS = DN_TILE // DN_CHUNK
_LOG_CHUNK = int(math.log2(DN_CHUNK))
_N_GATES = 4 * DN_HEADS


def _bmm(a, b):
    return lax.dot_general(a.astype(BF16), b.astype(BF16), (((2,), (1,)), ((0,), (0,))),
                           preferred_element_type=F32)


def _bmm_nt(a, b):
    return lax.dot_general(a.astype(BF16), b.astype(BF16), (((2,), (2,)), ((0,), (0,))),
                           preferred_element_type=F32)


def _bmm_tn(a, b):
    return lax.dot_general(a.astype(BF16), b.astype(BF16), (((1,), (1,)), ((0,), (0,))),
                           preferred_element_type=F32)


def _pair_cols(col_a, col_b, width):
    lane = lax.broadcasted_iota(jnp.int32, (col_a.shape[0], width), 1)
    return jnp.where(lane < width // 2, col_a, col_b)


def _block_diag(y):
    lane = lax.broadcasted_iota(jnp.int32, y.shape, 2)
    first = lane < y.shape[2] // 2
    return jnp.concatenate([jnp.where(first, y, 0.0), jnp.where(first, 0.0, y)], axis=1)


def _dn_tile_pair(fwd, bwd, par, of_ref, ob_ref, s_scr):
    alog_ref, dtb_ref, alogt_ref, dtbt_ref = par
    cs = DN_CHUNK
    pw = 2 * cs
    hp_n = DN_HEADS // 2
    npd = _CHUNKS * hp_n
    ii = lax.broadcasted_iota(jnp.int32, (cs, pw), 0)
    jj = lax.broadcasted_iota(jnp.int32, (cs, pw), 1) & (cs - 1)
    sx = jnp.stack([jnp.where(ii >= jj, ii ^ jj, -1), jnp.where(ii <= jj, ii ^ jj, -1)], axis=0)[:, None]
    incl = sx >= 0
    before = sx > 0
    i1 = lax.broadcasted_iota(jnp.int32, (cs, cs), 0)
    j1 = lax.broadcasted_iota(jnp.int32, (cs, cs), 1)
    hi = lax.Precision.HIGHEST

    gc_l, gct_l, beta_l, gc2_l, beta2_l, tot2_l, q_l, k_l, v_l = ([] for _ in range(9))
    for d, (q_ref, k_ref, v_ref, ab_ref, abt_ref) in enumerate((fwd, bwd)):
        incl_f = ((i1 >= j1) if d == 0 else (i1 <= j1)).astype(F32)
        ab = ab_ref[0]
        abt = abt_ref[0]
        g_col = -jnp.exp(alog_ref[...]) * _softplus(ab + dtb_ref[...])
        g_row = -jnp.exp(alogt_ref[...]) * _softplus(abt + dtbt_ref[...])
        beta_all = _sigmoid(ab)
        qx = q_ref[0]
        kx = k_ref[0]
        vx = v_ref[0]
        for c in range(_CHUNKS):
            rows = slice(c * cs, (c + 1) * cs)
            gc_c = jnp.dot(incl_f, g_col[rows], precision=hi, preferred_element_type=F32)
            gct_c = lax.dot_general(g_row[:, rows], incl_f, _NT, precision=hi, preferred_element_type=F32)
            tot_c = jnp.sum(g_col[rows], axis=0, keepdims=True)
            for hp in range(hp_n):
                ga = d * DN_HEADS + 2 * hp
                gb = ga + 1
                ba = ga + 2 * DN_HEADS
                bb = ba + 1
                gc_l.append(_pair_cols(gc_c[:, ga:ga + 1], gc_c[:, gb:gb + 1], pw))
                gct_l.append(jnp.concatenate([gct_c[ga:ga + 1, :], gct_c[gb:gb + 1, :]], axis=1))
                beta_l.append(_pair_cols(beta_all[rows, ba:ba + 1], beta_all[rows, bb:bb + 1], pw))
                gc2_l.append(_pair_cols(gc_c[:, ga:ga + 1], gc_c[:, gb:gb + 1], 2 * LANES))
                beta2_l.append(_pair_cols(beta_all[rows, ba:ba + 1], beta_all[rows, bb:bb + 1], 2 * LANES))
                tot2_l.append(_pair_cols(tot_c[:, ga:ga + 1], tot_c[:, gb:gb + 1], 2 * LANES))
                cols = slice(hp * 2 * LANES, (hp + 1) * 2 * LANES)
                q_l.append(qx[rows, cols])
                k_l.append(kx[rows, cols])
                v_l.append(vx[rows, cols])

    def stk(xs):
        return jnp.stack(xs, axis=0)

    gc, gc_t, beta = stk(gc_l), stk(gct_l), stk(beta_l)
    gc2, beta2, tot2 = stk(gc2_l), stk(beta2_l), stk(tot2_l)
    q, k, v = stk(q_l), stk(k_l), stk(v_l)

    def by_dir(x):
        return x.reshape((2, npd) + x.shape[1:])

    def flat(x):
        return x.reshape((2 * npd,) + x.shape[2:])

    k_bd = _block_diag(k)
    kk = by_dir(_bmm_nt(k, k_bd))
    qk = by_dir(_bmm_nt(q, k_bd))
    decay = jnp.where(incl, jnp.exp(jnp.where(incl, by_dir(gc - gc_t), 0.0)), 0.0)
    a_mat = jnp.where(before, (by_dir(beta) * kk) * decay, 0.0)
    a_qk = flat(qk * decay)

    e_mat = flat(-jnp.where(sx == 1, a_mat, 0.0))
    for lb in range(1, _LOG_CHUNK):
        l_b = flat(jnp.where(lax.shift_right_arithmetic(sx, lb) == 1, a_mat, 0.0))
        p_mat = l_b + _bmm(l_b, _block_diag(e_mat))
        e_mat = e_mat - p_mat - _bmm(e_mat, _block_diag(p_mat))

    eg2 = jnp.exp(gc2)
    vb = v * beta2
    kb = k * (beta2 * eg2)
    zero2 = jnp.zeros(vb.shape, F32)
    rhs = jnp.concatenate([vb[..., :LANES], kb[..., :LANES], vb[..., LANES:], kb[..., LANES:]], axis=-1)
    rhs_bd = jnp.concatenate([jnp.concatenate([vb[..., :LANES], kb[..., :LANES], zero2], axis=-1),
                              jnp.concatenate([zero2, vb[..., LANES:], kb[..., LANES:]], axis=-1)], axis=1)
    sol = rhs + _bmm(e_mat, rhs_bd)
    q_dec = q * eg2
    k_dec = k * jnp.exp(tot2 - gc2)
    c_dec2 = jnp.exp(tot2)

    def heads(x, chunks, off, stride):
        out = []
        for d in range(2):
            for hp in range(hp_n):
                p = d * npd + chunks[d] * hp_n + hp
                out += [x[p][:, off:off + LANES], x[p][:, off + stride:off + stride + LANES]]
        return jnp.stack(out, axis=0)

    s = s_scr[...].reshape(2 * DN_HEADS, DN_DK, DN_DV)
    zero_u = jnp.zeros((cs, LANES), F32)
    for step in range(_CHUNKS):
        chunks = (step, _CHUNKS - 1 - step)
        u_base = heads(sol, chunks, 0, 2 * LANES)
        w_dec = heads(sol, chunks, LANES, 2 * LANES)
        qd = heads(q_dec, chunks, 0, LANES)
        kd = heads(k_dec, chunks, 0, LANES)
        cd = heads(c_dec2, chunks, 0, LANES)[:, :, :1]
        ws_qs = _bmm(jnp.concatenate([w_dec, qd], axis=1), s)
        u = u_base - ws_qs[:, :cs]
        u_bd = jnp.stack([jnp.concatenate([jnp.concatenate([u[2 * i], zero_u], axis=-1),
                                           jnp.concatenate([zero_u, u[2 * i + 1]], axis=-1)], axis=0)
                          for i in range(DN_HEADS)], axis=0)
        aqk_sel = jnp.stack([a_qk[d * npd + chunks[d] * hp_n + hp] for d in range(2) for hp in range(hp_n)], axis=0)
        o_loc = _bmm(aqk_sel, u_bd)
        s = s * cd + _bmm_tn(kd, u)
        for d, o_ref in enumerate((of_ref, ob_ref)):
            c = chunks[d]
            for h in range(DN_HEADS):
                i = d * DN_HEADS + h
                o_ref[0, c * cs:(c + 1) * cs, h * LANES:(h + 1) * LANES] = (
                    ws_qs[i, cs:] + o_loc[i // 2][:, (h % 2) * LANES:(h % 2 + 1) * LANES])
    s_scr[...] = s.reshape(2, DN_HEADS, DN_DK, DN_DV)


def _dnscan_kernel(*refs, nt, has_s0, want_state):
    it = iter(refs)
    fwd = [next(it) for _ in range(5)]
    bwd = [next(it) for _ in range(5)]
    par = [next(it) for _ in range(4)]
    s0_ref = next(it) if has_s0 else None
    of_ref = next(it)
    ob_ref = next(it)
    st_ref = next(it) if want_state else None
    s_scr = next(it)
    t = pl.program_id(1)

    @pl.when(t == 0)
    def _():
        if has_s0:
            s_scr[...] = s0_ref[0]
        else:
            s_scr[...] = jnp.zeros(s_scr.shape, F32)

    _dn_tile_pair(fwd, bwd, par, of_ref, ob_ref, s_scr)

    if want_state:
        @pl.when(t == nt - 1)
        def _():
            st_ref[0] = s_scr[...]


def _dnscan(q, k, v, ab, abt, a_log, dt_bias, s0, want_state):
    bsz, seq, _ = q.shape
    nt = seq // DN_TILE

    def tile_specs(tmap):
        qs = pl.BlockSpec((1, DN_TILE, DN_V_W), lambda b, t: (b, tmap(t), 0))
        return [qs, qs, qs,
                pl.BlockSpec((1, DN_TILE, _N_GATES), lambda b, t: (b, tmap(t), 0)),
                pl.BlockSpec((1, _N_GATES, DN_TILE), lambda b, t: (b, 0, tmap(t)))]

    zeros8 = jnp.zeros((2 * DN_HEADS,), F32)
    alog16 = jnp.concatenate([a_log.reshape(-1), zeros8])
    dtb16 = jnp.concatenate([dt_bias.reshape(-1), zeros8])
    prow = pl.BlockSpec((1, _N_GATES), lambda b, t: (0, 0))
    pcol = pl.BlockSpec((_N_GATES, 1), lambda b, t: (0, 0))
    in_specs = tile_specs(lambda t: t) + tile_specs(lambda t: nt - 1 - t) + [prow, prow, pcol, pcol]
    args = [q, k, v, ab, abt, q, k, v, ab, abt,
            alog16.reshape(1, _N_GATES), dtb16.reshape(1, _N_GATES),
            alog16.reshape(_N_GATES, 1), dtb16.reshape(_N_GATES, 1)]
    st_block = (1, 2, DN_HEADS, DN_DK, DN_DV)
    if s0 is not None:
        in_specs.append(pl.BlockSpec(st_block, lambda b, t: (b, 0, 0, 0, 0)))
        args.append(s0)
    o_shape = jax.ShapeDtypeStruct((bsz, seq, DN_V_W), F32)
    out_specs = [pl.BlockSpec((1, DN_TILE, DN_V_W), lambda b, t: (b, t, 0)),
                 pl.BlockSpec((1, DN_TILE, DN_V_W), lambda b, t: (b, nt - 1 - t, 0))]
    out_shape = [o_shape, o_shape]
    if want_state:
        out_specs.append(pl.BlockSpec(st_block, lambda b, t: (b, 0, 0, 0, 0)))
        out_shape.append(jax.ShapeDtypeStruct((bsz, 2, DN_HEADS, DN_DK, DN_DV), F32))
    return pl.pallas_call(
        functools.partial(_dnscan_kernel, nt=nt, has_s0=s0 is not None, want_state=want_state),
        grid=(bsz, nt),
        in_specs=in_specs,
        out_specs=out_specs,
        out_shape=out_shape,
        scratch_shapes=[pltpu.VMEM((2, DN_HEADS, DN_DK, DN_DV), F32)],
        compiler_params=_cparams(2),
        name="deltanet_scan",
    )(*args)


def _outproj_kernel(x_ref, a_ref, b_ref, of_ref, ob_ref, z_ref, ng_ref, ga_ref, w_ref, o_ref):
    o = of_ref[0] + ob_ref[0]
    z = z_ref[0]
    acc = jnp.dot(a_ref[0].astype(BF16), w_ref[0:CONV_W, :], preferred_element_type=F32)
    acc = acc + jnp.dot(b_ref[0].astype(BF16), w_ref[CONV_W:CONV_W + NA_W, :], preferred_element_type=F32)
    for h in range(DN_HEADS):
        cols = slice(h * DN_DV, (h + 1) * DN_DV)
        oh = o[:, cols]
        y = oh * lax.rsqrt(jnp.mean(oh * oh, axis=-1, keepdims=True) + EPS) * ng_ref[...] * _silu(z[:, cols])
        r0 = CONV_W + NA_W + h * DN_DV
        acc = acc + jnp.dot(y.astype(BF16), w_ref[r0:r0 + DN_DV, :], preferred_element_type=F32)
    o_ref[0] = x_ref[0] + ga_ref[0] * acc


def _outproj(x, out_a, out_b, o_f, o_b, z, norm_g, gate, w_b):
    shape0 = x.shape
    tm = PROJ_TILE
    x, out_a, out_b, o_f, o_b, z = (_merge_short_sequences(a, tm, gate.shape[0] == 1)
                                    for a in (x, out_a, out_b, o_f, o_b, z))
    bsz, seq, d = x.shape

    def row(w):
        return pl.BlockSpec((1, tm, w), lambda b, t: (b, t, 0))

    return pl.pallas_call(
        _outproj_kernel,
        grid=(bsz, seq // tm),
        in_specs=[row(d), row(CONV_W), row(NA_W), row(DN_V_W), row(DN_V_W), row(DN_V_W),
                  pl.BlockSpec((1, DN_DV), lambda b, t: (0, 0)),
                  pl.BlockSpec((1, 1, d), _bmap(gate.shape[0])),
                  pl.BlockSpec((MIX_W, d), lambda b, t: (0, 0))],
        out_specs=row(d),
        out_shape=jax.ShapeDtypeStruct((bsz, seq, d), F32),
        compiler_params=_cparams(2),
        name="outproj",
    )(x, out_a, out_b, o_f, o_b, z, norm_g.reshape(1, -1), gate, w_b).reshape(shape0)


def _first_argmax(x, lane, valid):
    xm = jnp.where(valid, x, -jnp.inf)
    m = jnp.max(xm, axis=-1, keepdims=True)
    idx = jnp.min(jnp.where(valid & (xm == m), lane, float(LANES)), axis=-1, keepdims=True)
    return m, idx


def _route(h, rw_hi_ref, rw_lo_ref, rb_ref):
    h_hi = h.astype(BF16)
    h_lo = (h - h_hi.astype(F32)).astype(BF16)
    logits = (jnp.dot(h_hi, rw_hi_ref[...], preferred_element_type=F32)
              + jnp.dot(h_lo, rw_hi_ref[...], preferred_element_type=F32)
              + jnp.dot(h_hi, rw_lo_ref[...], preferred_element_type=F32)) + rb_ref[...]
    lane = lax.broadcasted_iota(jnp.int32, logits.shape, 1).astype(F32)
    is_grp = lane < N_GROUPS
    gmax, g_sel = _first_argmax(logits, lane, is_grp)
    pg_sel = 1.0 / jnp.sum(jnp.where(is_grp, jnp.exp(logits - gmax), 0.0), axis=-1, keepdims=True)
    e_lane = lane - N_GROUPS
    in_grp = (e_lane >= g_sel * EXPERTS_PER_GROUP) & (e_lane < (g_sel + 1) * EXPERTS_PER_GROUP)
    m1, i1 = _first_argmax(logits, lane, in_grp)
    m2, i2 = _first_argmax(logits, lane, in_grp & (lane != i1))
    e2 = jnp.exp(m2 - m1)
    w1 = pg_sel / (1.0 + e2)
    w2 = pg_sel * e2 / (1.0 + e2)
    return jnp.where(lane == i1, w1, 0.0) + jnp.where(lane == i2, w2, 0.0)


def _moe_kernel(x_ref, g_ref, sc_ref, sh_ref, ga_ref, rw_hi_ref, rw_lo_ref, rb_ref, w1_ref, w3_ref, w2_ref,
                gf_ref, o_ref, h_scr, gate_scr, acc_scr, *, final_norm):
    grp = pl.program_id(1)

    @pl.when(grp == 0)
    def _():
        h = _mod_norm(x_ref[...], g_ref[...], sc_ref[0], sh_ref[0])
        h_scr[...] = h.astype(BF16)
        gate_scr[...] = _route(h, rw_hi_ref, rw_lo_ref, rb_ref)
        acc_scr[...] = jnp.zeros(acc_scr.shape, F32)

    hb = h_scr[...]
    gates = gate_scr[...]
    lane = lax.broadcasted_iota(jnp.int32, gates.shape, 1)
    pair_w = 2 * D_EXPERT
    for half in range(EXPERTS_PER_GROUP // 2):
        cols = slice(half * pair_w, (half + 1) * pair_w)
        hid = _silu(jnp.dot(hb, w1_ref[0, :, cols], preferred_element_type=F32)) * jnp.dot(
            hb, w3_ref[0, :, cols], preferred_element_type=F32)
        gate_cols = []
        for j in range(2):
            e_lane = N_GROUPS + grp * EXPERTS_PER_GROUP + 2 * half + j
            gate_e = jnp.sum(jnp.where(lane == e_lane, gates, 0.0), axis=-1, keepdims=True)
            gate_cols.append(jnp.broadcast_to(gate_e, (gate_e.shape[0], D_EXPERT)))
        hid = hid * jnp.concatenate(gate_cols, axis=-1)
        acc_scr[...] += jnp.dot(hid.astype(BF16), w2_ref[0, cols, :], preferred_element_type=F32)

    @pl.when(grp == N_GROUPS - 1)
    def _():
        y = x_ref[...] + ga_ref[0] * acc_scr[...]
        if final_norm:
            y = y * lax.rsqrt(jnp.mean(y * y, axis=-1, keepdims=True) + EPS) * gf_ref[...]
        o_ref[...] = y


def _moe(x, g, scale, shift, gate, rw_hi, rw_lo, rb, w1_b, w3_b, w2_b, g_final, final_norm):
    bsz, seq, d = x.shape
    x2 = x.reshape(bsz * seq, d)
    if scale.shape[0] == 1:
        tm = MOE_TILE
        mm = lambda i, e: (0, 0, 0)
    else:
        tm = min(MOE_TILE, seq)
        per_seq = seq // tm
        mm = lambda i, e: (i // per_seq, 0, 0)
    vec = pl.BlockSpec((1, d), lambda i, e: (0, 0))
    mod = pl.BlockSpec((1, 1, d), mm)
    rspec = pl.BlockSpec((d, LANES), lambda i, e: (0, 0))
    gw = EXPERTS_PER_GROUP * D_EXPERT
    out = pl.pallas_call(
        functools.partial(_moe_kernel, final_norm=final_norm),
        grid=(bsz * seq // tm, N_GROUPS),
        in_specs=[pl.BlockSpec((tm, d), lambda i, e: (i, 0)), vec, mod, mod, mod,
                  rspec, rspec, pl.BlockSpec((1, LANES), lambda i, e: (0, 0)),
                  pl.BlockSpec((1, d, gw), lambda i, e: (e, 0, 0)),
                  pl.BlockSpec((1, d, gw), lambda i, e: (e, 0, 0)),
                  pl.BlockSpec((1, gw, d), lambda i, e: (e, 0, 0)),
                  vec],
        out_specs=pl.BlockSpec((tm, d), lambda i, e: (i, 0)),
        out_shape=jax.ShapeDtypeStruct((bsz * seq, d), F32),
        scratch_shapes=[pltpu.VMEM((tm, d), BF16), pltpu.VMEM((tm, LANES), F32), pltpu.VMEM((tm, d), F32)],
        compiler_params=_cparams(2),
        name="moe",
    )(x2, g, scale, shift, gate, rw_hi, rw_lo, rb, w1_b, w3_b, w2_b, g_final.reshape(1, -1))
    return out.reshape(bsz, seq, d)


def _split_hi_lo(w):
    hi = w.astype(BF16)
    return hi, (w - hi.astype(F32)).astype(BF16)


def kernel(x_prompt, x_sample, cache_na_k, cache_na_v, state_delta, c, c_ctx, w_mod, b_mod, g_norm1, g_norm2,
           w_in, conv_a_w, conv_a_b, ln_a_g, ln_a_b, na_rel_bias, dn_conv_w, dn_a_log, dn_dt_bias, dn_norm_g,
           w_out, router_wg, router_bg, router_we, router_be, w1, w3, w2, g_final):
    n_dec = x_sample.shape[0]
    d = D_MODEL
    cond8 = jnp.concatenate([c_ctx[None, :], c, jnp.zeros((8 - 1 - n_dec, d), F32)], axis=0)
    mods = _adaln(cond8, w_mod.astype(BF16), b_mod).reshape(DEPTH, 8, 6, d)

    w_in_b = w_in.astype(BF16)
    w_abt_b = jnp.swapaxes(w_in[:, :, OFF_DN_AB:], 1, 2).astype(BF16)
    w_out_b = w_out.astype(BF16)
    def group_cols(w):
        w = w.astype(BF16).reshape(DEPTH, N_GROUPS, EXPERTS_PER_GROUP, d, D_EXPERT)
        return jnp.swapaxes(w, 2, 3).reshape(DEPTH, N_GROUPS, d, EXPERTS_PER_GROUP * D_EXPERT)

    w1_b, w3_b = group_cols(w1), group_cols(w3)
    w2_b = w2.astype(BF16).reshape(DEPTH, N_GROUPS, EXPERTS_PER_GROUP * D_EXPERT, d)
    pad = LANES - N_GROUPS - N_EXPERTS
    rw = jnp.concatenate([router_wg, router_we, jnp.zeros((DEPTH, d, pad), F32)], axis=-1)
    rw_hi, rw_lo = _split_hi_lo(rw)
    rb = jnp.concatenate([router_bg, router_be, jnp.zeros((DEPTH, pad), F32)], axis=-1)

    past = cache_na_k.shape[2]
    cache_k = cache_na_k.reshape(n_dec, DEPTH, past, NA_W)
    cache_v = cache_na_v.reshape(n_dec, DEPTH, past, NA_W)
    rows = x_sample.shape[1] // GRID_W

    def layer(x, l, is_ctx):
        m = mods[l, 0:1] if is_ctx else mods[l, 1:1 + n_dec]
        sh1, sc1, ga1, sh2, sc2, ga2 = (m[:, i:i + 1, :] for i in range(6))
        glu, q_na, k_na, v_na, dqkv, dz, ab, abt = _inproj(x, g_norm1[l:l + 1], sc1, sh1, w_in_b[l], w_abt_b[l])
        out_a = _cconv(glu, conv_a_w[l], conv_a_b[l], ln_a_g[l], ln_a_b[l])
        if is_ctx:
            out_b = _ctx_attn(q_na, k_na, v_na)
        else:
            out_b = _na_attn(q_na, k_na, v_na, cache_k, cache_v, l, _na_bias_tables(na_rel_bias[l], rows))
        dq, dk, dv = _dnprep(dqkv, dn_conv_w[l], use_rope=not is_ctx)
        scan = _dnscan(dq, dk, dv, ab, abt, dn_a_log[l], dn_dt_bias[l],
                       None if is_ctx else state_delta[:, l], want_state=is_ctx)
        o_f, o_b = scan[0], scan[1]
        x = _outproj(x, out_a, out_b, o_f, o_b, dz, dn_norm_g[l], ga1, w_out_b[l])
        x = _moe(x, g_norm2[l:l + 1], sc2, sh2, ga2, rw_hi[l], rw_lo[l], rb[l:l + 1], w1_b[l], w3_b[l], w2_b[l],
                 g_final, final_norm=(l == DEPTH - 1))
        return x, k_na, v_na, (scan[2] if is_ctx else None)

    xc = x_prompt
    new_k, new_v, new_s = [], [], []
    for l in range(DEPTH):
        xc, k_na, v_na, s_ctx = layer(xc, l, True)
        new_k.append(k_na.reshape(k_na.shape[0], k_na.shape[1], NA_HEADS, NA_HEAD_DIM))
        new_v.append(v_na.reshape(v_na.shape[0], v_na.shape[1], NA_HEADS, NA_HEAD_DIM))
        new_s.append(s_ctx)
    xs = x_sample
    for l in range(DEPTH):
        xs, _, _, _ = layer(xs, l, False)
    return (xc, xs, jnp.stack(new_k, axis=1), jnp.stack(new_v, axis=1), jnp.stack(new_s, axis=1))
```

```python
import functools
import math

import numpy as np
import jax
import jax.numpy as jnp
from jax import lax
from jax.experimental import pallas as pl
from jax.experimental.pallas import tpu as pltpu

F32 = jnp.float32
BF16 = jnp.bfloat16

D_MODEL = 1024
DEPTH = 2
GRID_W = 64
CONV_W = 256
CONV_K = 31
NA_HEADS = 4
NA_HEAD_DIM = 64
NA_W = NA_HEADS * NA_HEAD_DIM
NA_ROWS = 8
NA_COLS = 16
DN_HEADS = 4
DN_DK = 128
DN_DV = 128
DN_QK_W = DN_HEADS * DN_DK
DN_V_W = DN_HEADS * DN_DV
DN_CONV_K = 5
DN_CHUNK = 64
ROPE_BASE = 10000.0
MIX_W = CONV_W + NA_W + DN_V_W
OFF_NA = 2 * CONV_W
OFF_DN_QKV = OFF_NA + 3 * NA_W
OFF_DN_Z = OFF_DN_QKV + 2 * DN_QK_W + DN_V_W
OFF_DN_AB = OFF_DN_Z + DN_V_W
PROJ_W = OFF_DN_AB + 4 * DN_HEADS
N_GROUPS = 4
EXPERTS_PER_GROUP = 4
N_EXPERTS = N_GROUPS * EXPERTS_PER_GROUP
D_EXPERT = 256
EPS = 1e-6
NEG_INF = -1e30

LANES = 128
SUBLANES = 8
ROW_TILE = 256
PROJ_TILE = 512
DN_TILE = 256
NA_QROWS = 4
NA_WIN_ROWS = 12
MOE_TILE = 1024
VMEM_LIMIT = 48 * 1024 * 1024

_NT = (((1,), (1,)), ((), ()))


def _cparams(n_axes):
    return pltpu.CompilerParams(dimension_semantics=("arbitrary",) * n_axes,
                                vmem_limit_bytes=VMEM_LIMIT)


def _sigmoid(x):
    return 1.0 / (1.0 + jnp.exp(-x))


def _silu(x):
    return x * _sigmoid(x)


def _softplus(x):
    return jnp.maximum(x, 0.0) + jnp.log(1.0 + jnp.exp(-jnp.abs(x)))


def _dot(a, b):
    return jnp.dot(a.astype(BF16), b.astype(BF16), preferred_element_type=F32)


def _adaln_kernel(c_ref, w_ref, b_ref, o_ref):
    c = c_ref[...]
    o_ref[0] = _dot(_silu(c), w_ref[0]) + b_ref[0]


def _adaln(cond8, w_mod, b_mod):
    n_l, d, n6 = w_mod.shape
    tn = 1536
    return pl.pallas_call(
        _adaln_kernel,
        grid=(n_l, n6 // tn),
        in_specs=[pl.BlockSpec((8, d), lambda l, j: (0, 0)),
                  pl.BlockSpec((1, d, tn), lambda l, j: (l, 0, j)),
                  pl.BlockSpec((1, 1, tn), lambda l, j: (l, 0, j))],
        out_specs=pl.BlockSpec((1, 8, tn), lambda l, j: (l, 0, j)),
        out_shape=jax.ShapeDtypeStruct((n_l, 8, n6), F32),
        compiler_params=_cparams(2),
        name="adaln",
    )(cond8, w_mod, b_mod.reshape(n_l, 1, n6))


def _mod_norm(x, g, scale, shift):
    ms = jnp.mean(x * x, axis=-1, keepdims=True)
    return (x * lax.rsqrt(ms + EPS) * g) * (1.0 + scale) + shift


def _inproj_kernel(x_ref, g_ref, sc_ref, sh_ref, w_ref, wabt_ref,
                   glu_ref, q_ref, k_ref, v_ref, dqkv_ref, z_ref, ab_ref, abt_ref):
    hb = _mod_norm(x_ref[0], g_ref[...], sc_ref[0], sh_ref[0]).astype(BF16)

    def proj(a, b):
        return jnp.dot(hb, w_ref[:, a:b], preferred_element_type=F32)

    glu_ref[0] = proj(0, OFF_NA)
    q_ref[0] = proj(OFF_NA, OFF_NA + NA_W).astype(BF16)
    k_ref[0] = proj(OFF_NA + NA_W, OFF_NA + 2 * NA_W)
    v_ref[0] = proj(OFF_NA + 2 * NA_W, OFF_DN_QKV)
    dqkv_ref[0] = proj(OFF_DN_QKV, OFF_DN_Z).astype(BF16)
    z_ref[0] = proj(OFF_DN_Z, OFF_DN_AB).astype(BF16)
    ab_ref[0] = proj(OFF_DN_AB, PROJ_W)
    abt_ref[0] = lax.dot_general(wabt_ref[...], hb, _NT, preferred_element_type=F32)


def _bmap(bm):
    if bm == 1:
        return lambda b, t: (0, 0, 0)
    return lambda b, t: (b, 0, 0)


def _merge_short_sequences(x, tm, shared_mod):
    bsz, seq, d = x.shape
    if shared_mod and seq < tm:
        return x.reshape(bsz * seq // tm, tm, d)
    return x


def _inproj(x, g, scale, shift, w_b, wabt_b, layer):
    bsz0, seq0, _ = x.shape
    tm = PROJ_TILE
    x = _merge_short_sequences(x, tm, scale.shape[0] == 1)
    outs = _inproj_call(x, g, scale, shift, w_b, wabt_b, layer, tm)
    if x.shape[0] == bsz0:
        return outs
    per = tm // seq0
    row_outs = [o.reshape(bsz0, seq0, o.shape[-1]) for o in outs[:-1]]
    abt = outs[-1].reshape(x.shape[0], 4 * DN_HEADS, per, seq0)
    abt = jnp.swapaxes(abt, 1, 2).reshape(bsz0, 4 * DN_HEADS, seq0)
    return row_outs + [abt]


def _inproj_call(x, g, scale, shift, w_b, wabt_b, layer, tm):
    bsz, seq, d = x.shape
    widths = (OFF_NA, NA_W, NA_W, NA_W, OFF_DN_Z - OFF_DN_QKV, DN_V_W, 4 * DN_HEADS)
    dtypes = (F32, BF16, F32, F32, BF16, BF16, F32)
    out_shape = [jax.ShapeDtypeStruct((bsz, seq, w), dt) for w, dt in zip(widths, dtypes)]
    out_shape.append(jax.ShapeDtypeStruct((bsz, 4 * DN_HEADS, seq), F32))
    out_specs = [pl.BlockSpec((1, tm, w), lambda b, t: (b, t, 0)) for w in widths]
    out_specs.append(pl.BlockSpec((1, 4 * DN_HEADS, tm), lambda b, t: (b, 0, t)))
    mm = _bmap(scale.shape[0])
    return pl.pallas_call(
        _inproj_kernel,
        grid=(bsz, seq // tm),
        in_specs=[pl.BlockSpec((1, tm, d), lambda b, t: (b, t, 0)),
                  pl.BlockSpec((1, d), lambda b, t: (0, 0)),
                  pl.BlockSpec((1, 1, d), mm),
                  pl.BlockSpec((1, 1, d), mm),
                  pl.BlockSpec((None, d, PROJ_W), lambda b, t: (layer, 0, 0)),
                  pl.BlockSpec((None, 4 * DN_HEADS, d), lambda b, t: (layer, 0, 0))],
        out_specs=out_specs,
        out_shape=out_shape,
        compiler_params=_cparams(2),
        name="inproj",
    )(x, g, scale, shift, w_b, wabt_b)


_CC_HALO = 16
_CC_SUB = 64


def _cconv_kernel(main_ref, prev_ref, next_ref, w_ref, b_ref, g_ref, beta_ref, o_ref, scr, *, tt, nt):
    t = pl.program_id(1)

    def glu(a):
        return a[:, :CONV_W] * _sigmoid(a[:, CONV_W:])

    scr[0, 0:_CC_HALO, :] = jnp.where(t > 0, glu(prev_ref[0]), 0.0)
    scr[0, _CC_HALO:_CC_HALO + tt, :] = glu(main_ref[0])
    scr[0, _CC_HALO + tt:2 * _CC_HALO + tt, :] = jnp.where(t < nt - 1, glu(next_ref[0]), 0.0)
    n_shift = tt + 2 * _CC_HALO - SUBLANES
    for r in range(1, SUBLANES):
        scr[r, 0:n_shift, :] = scr[0, r:r + n_shift, :]
    pad = (CONV_K - 1) // 2
    for s in range(tt // _CC_SUB):
        base = s * _CC_SUB + _CC_HALO - pad
        acc = jnp.zeros((_CC_SUB, CONV_W), F32) + b_ref[...]
        for k in range(CONV_K):
            r = (base + k) % SUBLANES
            a0 = base + k - r
            acc = acc + w_ref[k:k + 1, :] * scr[r, a0:a0 + _CC_SUB, :]
        mu = jnp.mean(acc, axis=-1, keepdims=True)
        xc = acc - mu
        var = jnp.mean(xc * xc, axis=-1, keepdims=True)
        y = xc * lax.rsqrt(var + EPS) * g_ref[...] + beta_ref[...]
        o_ref[0, s * _CC_SUB:(s + 1) * _CC_SUB, :] = _silu(y).astype(BF16)


def _halo_specs(tt, seq, halo, width):
    per = tt // halo
    last = seq // halo - 1
    prev = pl.BlockSpec((1, halo, width), lambda b, t: (b, jnp.maximum(t * per - 1, 0), 0))
    nxt = pl.BlockSpec((1, halo, width), lambda b, t: (b, jnp.minimum((t + 1) * per, last), 0))
    return prev, nxt


def _cconv(glu, w, b, g, beta):
    bsz, seq, _ = glu.shape
    tt = ROW_TILE
    nt = seq // tt
    prev, nxt = _halo_specs(tt, seq, _CC_HALO, 2 * CONV_W)
    vec = pl.BlockSpec((1, CONV_W), lambda b_, t: (0, 0))
    return pl.pallas_call(
        functools.partial(_cconv_kernel, tt=tt, nt=nt),
        grid=(bsz, nt),
        in_specs=[pl.BlockSpec((1, tt, 2 * CONV_W), lambda b_, t: (b_, t, 0)), prev, nxt,
                  pl.BlockSpec((CONV_K, CONV_W), lambda b_, t: (0, 0)), vec, vec, vec],
        out_specs=pl.BlockSpec((1, tt, CONV_W), lambda b_, t: (b_, t, 0)),
        out_shape=jax.ShapeDtypeStruct((bsz, seq, CONV_W), BF16),
        scratch_shapes=[pltpu.VMEM((SUBLANES, tt + 2 * _CC_HALO, CONV_W), F32)],
        compiler_params=_cparams(2),
        name="conformer_conv",
    )(glu, glu, glu, w, b.reshape(1, -1), g.reshape(1, -1), beta.reshape(1, -1))


def _head_lane_id(shape):
    return lax.shift_right_logical(lax.broadcasted_iota(jnp.int32, shape, 1), 6)


def _ctx_attn_kernel(q_ref, k_ref, v_ref, o_ref):
    q = q_ref[0].astype(F32) * (NA_HEAD_DIM ** -0.5)
    k = k_ref[0].astype(BF16)
    v = v_ref[0].astype(BF16)
    head = _head_lane_id(q.shape)
    out = jnp.zeros(q.shape, F32)
    for h in range(NA_HEADS):
        s = lax.dot_general(jnp.where(head == h, q, 0.0).astype(BF16), k, _NT, preferred_element_type=F32)
        m = jnp.max(s, axis=-1, keepdims=True)
        p = jnp.exp(s - m)
        den = jnp.sum(p, axis=-1, keepdims=True)
        oh = jnp.dot(p.astype(BF16), v, preferred_element_type=F32)
        out = jnp.where(head == h, oh / den, out)
    o_ref[0] = out.astype(BF16)


def _ctx_attn(q, k, v):
    bsz, seq, w = q.shape
    spec = pl.BlockSpec((1, seq, w), lambda b: (b, 0, 0))
    return pl.pallas_call(
        _ctx_attn_kernel,
        grid=(bsz,),
        in_specs=[spec, spec, spec],
        out_specs=spec,
        out_shape=jax.ShapeDtypeStruct((bsz, seq, w), BF16),
        compiler_params=_cparams(1),
        name="context_attention",
    )(q, k, v)


def _na_window_start(blk, rows):
    return jnp.clip(blk * NA_QROWS - NA_ROWS // 2, 0, rows - NA_WIN_ROWS)


def _na_kernel(q_ref, k_ref, v_ref, kc_ref, vc_ref, eb_ref, o_ref, *, rows):
    blk = pl.program_id(1)
    ws = pl.multiple_of(_na_window_start(blk, rows) * GRID_W, GRID_W)
    nwin = NA_WIN_ROWS * GRID_W
    q = q_ref[0].astype(F32) * (NA_HEAD_DIM ** -0.5)
    kl = k_ref[0, pl.ds(ws, nwin), :].astype(BF16)
    vl = v_ref[0, pl.ds(ws, nwin), :].astype(BF16)
    kc = kc_ref[...].astype(BF16)
    vc = vc_ref[...].astype(BF16)
    head = _head_lane_id(q.shape)
    out = jnp.zeros(q.shape, F32)
    for h in range(NA_HEADS):
        qh = jnp.where(head == h, q, 0.0).astype(BF16)
        sl = lax.dot_general(qh, kl, _NT, preferred_element_type=F32) + eb_ref[0, h]
        sc = lax.dot_general(qh, kc, _NT, preferred_element_type=F32)
        m = jnp.maximum(jnp.max(sl, axis=-1, keepdims=True), jnp.max(sc, axis=-1, keepdims=True))
        p_l = jnp.exp(sl - m)
        p_c = jnp.exp(sc - m)
        den = jnp.sum(p_l, axis=-1, keepdims=True) + jnp.sum(p_c, axis=-1, keepdims=True)
        oh = (jnp.dot(p_l.astype(BF16), vl, preferred_element_type=F32)
              + jnp.dot(p_c.astype(BF16), vc, preferred_element_type=F32))
        out = jnp.where(head == h, oh / den, out)
    o_ref[0] = out.astype(BF16)


def _na_bias_tables(rel_bias, rows):
    n_heads = rel_bias.shape[0]
    c = np.arange(GRID_W)
    cs = np.clip(c - NA_COLS // 2, 0, GRID_W - NA_COLS)
    col_valid = (c[None, :] >= cs[:, None]) & (c[None, :] < cs[:, None] + NA_COLS)
    padw = GRID_W - NA_COLS
    padded = jnp.pad(rel_bias, ((0, 0), (0, 0), (padw, padw)))
    toeplitz = jnp.stack([padded[:, :, GRID_W - 1 - ci:2 * GRID_W - 1 - ci] for ci in range(GRID_W)], axis=2)
    toeplitz = jnp.where(jnp.asarray(col_valid)[None, None], toeplitz, NEG_INF)
    masked = jnp.full((n_heads, GRID_W, GRID_W), NEG_INF, F32)
    nblk = rows // NA_QROWS
    tables = []
    for blk in (0, 1, nblk - 1):
        ws = int(np.clip(blk * NA_QROWS - NA_ROWS // 2, 0, rows - NA_WIN_ROWS))
        q_rows = []
        for a in range(NA_QROWS):
            r = blk * NA_QROWS + a
            rs = int(np.clip(r - NA_ROWS // 2, 0, rows - NA_ROWS))
            blocks = []
            for j in range(NA_WIN_ROWS):
                kr = ws + j
                blocks.append(toeplitz[:, kr - r + NA_ROWS - 1] if rs <= kr < rs + NA_ROWS else masked)
            q_rows.append(jnp.concatenate(blocks, axis=-1))
        tables.append(jnp.concatenate(q_rows, axis=1))
    return jnp.stack(tables, axis=0)


def _na_attn(q, k, v, cache_k, cache_v, layer, eb):
    bsz, seq, w = q.shape
    rows = seq // GRID_W
    nblk = rows // NA_QROWS
    tq = NA_QROWS * GRID_W
    past = cache_k.shape[2]
    full = pl.BlockSpec((1, seq, w), lambda b, r: (b, 0, 0))
    ctx = pl.BlockSpec((None, None, past, w), lambda b, r: (b, layer, 0, 0))

    def eb_map(b, r):
        return (jnp.where(r == 0, 0, jnp.where(r == nblk - 1, 2, 1)), 0, 0, 0)

    return pl.pallas_call(
        functools.partial(_na_kernel, rows=rows),
        grid=(bsz, nblk),
        in_specs=[pl.BlockSpec((1, tq, w), lambda b, r: (b, r, 0)), full, full, ctx, ctx,
                  pl.BlockSpec((1, NA_HEADS, tq, NA_WIN_ROWS * GRID_W), eb_map)],
        out_specs=pl.BlockSpec((1, tq, w), lambda b, r: (b, r, 0)),
        out_shape=jax.ShapeDtypeStruct((bsz, seq, w), BF16),
        compiler_params=_cparams(2),
        name="neighbourhood_attention",
    )(q, k, v, cache_k, cache_v, eb)


_DN_HALO = 16
_DN_RSUB = 128
_DN_CGRP = 2 * LANES


def _dnprep_kernel(*refs, tt, nt, use_rope):
    if use_rope:
        main_ref, prev_ref, next_ref, w_ref, cos_ref, sin_ref, q_ref, k_ref, v_ref = refs
    else:
        main_ref, prev_ref, next_ref, w_ref, q_ref, k_ref, v_ref = refs
    t = pl.program_id(1)
    main = main_ref[0]
    halo = jnp.concatenate([jnp.where(t > 0, prev_ref[0], jnp.zeros_like(prev_ref[0])),
                            jnp.where(t < nt - 1, next_ref[0], jnp.zeros_like(next_ref[0]))], axis=0)
    pad = (DN_CONV_K - 1) // 2
    row = lax.broadcasted_iota(jnp.int32, (tt, tt), 0)
    col = lax.broadcasted_iota(jnp.int32, (tt, tt), 1)
    hrow = lax.broadcasted_iota(jnp.int32, (tt, 2 * _DN_HALO), 0)
    hcol = lax.broadcasted_iota(jnp.int32, (tt, 2 * _DN_HALO), 1)
    htok = jnp.where(hcol < _DN_HALO, hcol - _DN_HALO, tt + hcol - _DN_HALO)
    sel = {}
    for kk in range(DN_CONV_K):
        if kk != pad:
            sel[kk] = (jnp.where(col == row + (kk - pad), 1.0, 0.0).astype(BF16),
                       jnp.where(htok == hrow + (kk - pad), 1.0, 0.0).astype(BF16))
    outs = (q_ref, k_ref, v_ref)
    if use_rope:
        lane = lax.broadcasted_iota(jnp.int32, (_DN_RSUB, LANES), 1)
        first_half = (lane & (DN_DK // 4)) == 0
    for cg in range(3 * DN_QK_W // _DN_CGRP):
        gcols = slice(cg * _DN_CGRP, (cg + 1) * _DN_CGRP)
        taps = {}
        for kk in range(DN_CONV_K):
            if kk == pad:
                taps[kk] = main[:, gcols].astype(F32)
            else:
                taps[kk] = (jnp.dot(sel[kk][0], main[:, gcols], preferred_element_type=F32)
                            + jnp.dot(sel[kk][1], halo[:, gcols], preferred_element_type=F32))
        for sub in range(_DN_CGRP // LANES):
            cb = cg * (_DN_CGRP // LANES) + sub
            cols = slice(cb * LANES, (cb + 1) * LANES)
            lcols = slice(sub * LANES, (sub + 1) * LANES)
            for rs in range(tt // _DN_RSUB):
                rsl = slice(rs * _DN_RSUB, (rs + 1) * _DN_RSUB)
                acc = jnp.zeros((_DN_RSUB, LANES), F32)
                for kk in range(DN_CONV_K):
                    acc = acc + w_ref[kk:kk + 1, cols] * taps[kk][rsl, lcols]
                y = _silu(acc)
                if cb < 2 * DN_HEADS:
                    y = y * lax.rsqrt(jnp.sum(y * y, axis=-1, keepdims=True) + EPS)
                    if use_rope:
                        quarter = DN_DK // 4
                        partner = jnp.where(first_half, pltpu.roll(y, LANES - quarter, 1), pltpu.roll(y, quarter, 1))
                        y = y * cos_ref[rsl, :] + partner * sin_ref[rsl, :]
                    if cb < DN_HEADS:
                        y = y * (DN_DK ** -0.5)
                hh = cb % DN_HEADS
                outs[cb // DN_HEADS][0, rsl, hh * LANES:(hh + 1) * LANES] = y


def _rope_lane_tables(seq):
    pos = np.arange(seq)
    n_freq = DN_DK // 4
    inv = ROPE_BASE ** (-np.arange(n_freq) / n_freq)
    ang = np.stack([(pos // GRID_W)[:, None] * inv[None, :], (pos % GRID_W)[:, None] * inv[None, :]], axis=1)
    cos = np.cos(ang).astype(np.float32)
    sin = np.sin(ang).astype(np.float32)
    lane = np.arange(DN_DK)
    axis, half, freq = lane // (2 * n_freq), (lane // n_freq) % 2, lane % n_freq
    cos_l = cos[:, axis, freq]
    sin_l = sin[:, axis, freq] * np.where(half == 0, -1.0, 1.0).astype(np.float32)[None, :]
    return jnp.asarray(cos_l), jnp.asarray(sin_l)


def _dnprep(dqkv, conv_w, use_rope):
    bsz, seq, width = dqkv.shape
    tt = ROW_TILE
    nt = seq // tt
    prev, nxt = _halo_specs(tt, seq, _DN_HALO, width)
    in_specs = [pl.BlockSpec((1, tt, width), lambda b, t: (b, t, 0)), prev, nxt,
                pl.BlockSpec((DN_CONV_K, width), lambda b, t: (0, 0))]
    args = [dqkv, dqkv, dqkv, conv_w]
    if use_rope:
        tab = pl.BlockSpec((tt, DN_DK), lambda b, t: (t, 0))
        in_specs += [tab, tab]
        args += list(_rope_lane_tables(seq))
    out = jax.ShapeDtypeStruct((bsz, seq, DN_QK_W), F32)
    ospec = pl.BlockSpec((1, tt, DN_QK_W), lambda b, t: (b, t, 0))
    return pl.pallas_call(
        functools.partial(_dnprep_kernel, tt=tt, nt=nt, use_rope=use_rope),
        grid=(bsz, nt),
        in_specs=in_specs,
        out_specs=[ospec, ospec, ospec],
        out_shape=[out, out, out],
        compiler_params=_cparams(2),
        name="deltanet_prep",
    )(*args)


_CHUNKS = DN_TILE // DN_CHUNK
_LOG_CHUNK = int(math.log2(DN_CHUNK))
_N_GATES = 4 * DN_HEADS


def _bmm(a, b):
    return lax.dot_general(a.astype(BF16), b.astype(BF16), (((2,), (1,)), ((0,), (0,))),
                           preferred_element_type=F32)


def _bmm_nt(a, b):
    return lax.dot_general(a.astype(BF16), b.astype(BF16), (((2,), (2,)), ((0,), (0,))),
                           preferred_element_type=F32)


def _bmm_tn(a, b):
    return lax.dot_general(a.astype(BF16), b.astype(BF16), (((1,), (1,)), ((0,), (0,))),
                           preferred_element_type=F32)


def _pair_cols(col_a, col_b, width):
    lane = lax.broadcasted_iota(jnp.int32, (col_a.shape[0], width), 1)
    return jnp.where(lane < width // 2, col_a, col_b)


def _block_diag(y):
    lane = lax.broadcasted_iota(jnp.int32, y.shape, 2)
    first = lane < y.shape[2] // 2
    return jnp.concatenate([jnp.where(first, y, 0.0), jnp.where(first, 0.0, y)], axis=1)


def _chunk_cumsum(x, axis, reverse):
    n = x.shape[axis]
    pos = lax.broadcasted_iota(jnp.int32, x.shape, axis) & (DN_CHUNK - 1)
    step = 1
    while step < DN_CHUNK:
        if reverse:
            x = x + jnp.where(pos < DN_CHUNK - step, pltpu.roll(x, n - step, axis), 0.0)
        else:
            x = x + jnp.where(pos >= step, pltpu.roll(x, step, axis), 0.0)
        step *= 2
    return x


def _dn_tile_pair(fwd, bwd, par, of_ref, ob_ref, s_scr):
    alog_ref, dtb_ref, alogt_ref, dtbt_ref = par
    cs = DN_CHUNK
    pw = 2 * cs
    hp_n = DN_HEADS // 2
    npd = _CHUNKS * hp_n
    ii = lax.broadcasted_iota(jnp.int32, (cs, pw), 0)
    jj = lax.broadcasted_iota(jnp.int32, (cs, pw), 1) & (cs - 1)
    sx = jnp.stack([jnp.where(ii >= jj, ii ^ jj, -1), jnp.where(ii <= jj, ii ^ jj, -1)], axis=0)[:, None]
    incl = sx >= 0
    before = sx > 0

    gc_l, gct_l, beta_l, gc2_l, beta2_l, tot2_l, q_l, k_l, v_l = ([] for _ in range(9))
    for d, (q_ref, k_ref, v_ref, ab_ref, abt_ref) in enumerate((fwd, bwd)):
        ab = ab_ref[0]
        abt = abt_ref[0]
        g_col = -jnp.exp(alog_ref[...]) * _softplus(ab + dtb_ref[...])
        g_row = -jnp.exp(alogt_ref[...]) * _softplus(abt + dtbt_ref[...])
        beta_all = _sigmoid(ab)
        gc_rows = jnp.concatenate([_chunk_cumsum(g_row[:, i * LANES:(i + 1) * LANES], 1, d == 1)
                                   for i in range(DN_TILE // LANES)], axis=1)
        qx = q_ref[0]
        kx = k_ref[0]
        vx = v_ref[0]
        for c in range(_CHUNKS):
            rows = slice(c * cs, (c + 1) * cs)
            gc_c = _chunk_cumsum(g_col[rows], 0, d == 1)
            gct_c = gc_rows[:, rows]
            last = 0 if d == 1 else cs - 1
            tot_c = gc_c[last:last + 1, :]
            for hp in range(hp_n):
                ga = d * DN_HEADS + 2 * hp
                gb = ga + 1
                ba = ga + 2 * DN_HEADS
                bb = ba + 1
                gc_l.append(_pair_cols(gc_c[:, ga:ga + 1], gc_c[:, gb:gb + 1], pw))
                gct_l.append(jnp.concatenate([gct_c[ga:ga + 1, :], gct_c[gb:gb + 1, :]], axis=1))
                beta_l.append(_pair_cols(beta_all[rows, ba:ba + 1], beta_all[rows, bb:bb + 1], pw))
                gc2_l.append(_pair_cols(gc_c[:, ga:ga + 1], gc_c[:, gb:gb + 1], 2 * LANES))
                beta2_l.append(_pair_cols(beta_all[rows, ba:ba + 1], beta_all[rows, bb:bb + 1], 2 * LANES))
                tot2_l.append(_pair_cols(tot_c[:, ga:ga + 1], tot_c[:, gb:gb + 1], 2 * LANES))
                cols = slice(hp * 2 * LANES, (hp + 1) * 2 * LANES)
                q_l.append(qx[rows, cols])
                k_l.append(kx[rows, cols])
                v_l.append(vx[rows, cols])

    def stk(xs):
        return jnp.stack(xs, axis=0)

    gc, gc_t, beta = stk(gc_l), stk(gct_l), stk(beta_l)
    gc2, beta2, tot2 = stk(gc2_l), stk(beta2_l), stk(tot2_l)
    q, k, v = stk(q_l), stk(k_l), stk(v_l)

    def by_dir(x):
        return x.reshape((2, npd) + x.shape[1:])

    def flat(x):
        return x.reshape((2 * npd,) + x.shape[2:])

    k_bd = _block_diag(k)
    kk = by_dir(_bmm_nt(k, k_bd))
    qk = by_dir(_bmm_nt(q, k_bd))
    decay = jnp.where(incl, jnp.exp(jnp.where(incl, by_dir(gc - gc_t), 0.0)), 0.0)
    a_mat = jnp.where(before, (by_dir(beta) * kk) * decay, 0.0)
    a_qk = flat(qk * decay)

    e_mat = flat(-jnp.where(sx == 1, a_mat, 0.0))
    for lb in range(1, _LOG_CHUNK):
        l_b = flat(jnp.where(lax.shift_right_arithmetic(sx, lb) == 1, a_mat, 0.0))
        p_mat = l_b + _bmm(l_b, _block_diag(e_mat))
        e_mat = e_mat - p_mat - _bmm(e_mat, _block_diag(p_mat))

    eg2 = jnp.exp(gc2)
    vb = v * beta2
    kb = k * (beta2 * eg2)
    zero2 = jnp.zeros(vb.shape, F32)
    rhs = jnp.concatenate([vb[..., :LANES], kb[..., :LANES], vb[..., LANES:], kb[..., LANES:]], axis=-1)
    rhs_bd = jnp.concatenate([jnp.concatenate([vb[..., :LANES], kb[..., :LANES], zero2], axis=-1),
                              jnp.concatenate([zero2, vb[..., LANES:], kb[..., LANES:]], axis=-1)], axis=1)
    sol = rhs + _bmm(e_mat, rhs_bd)
    q_dec = q * eg2
    k_dec = k * jnp.exp(tot2 - gc2)
    c_dec2 = jnp.exp(tot2)

    def heads(x, chunks, off, stride):
        out = []
        for d in range(2):
            for hp in range(hp_n):
                p = d * npd + chunks[d] * hp_n + hp
                out += [x[p][:, off:off + LANES], x[p][:, off + stride:off + stride + LANES]]
        return jnp.stack(out, axis=0)

    s = s_scr[...].reshape(2 * DN_HEADS, DN_DK, DN_DV)
    zero_u = jnp.zeros((cs, LANES), F32)
    for step in range(_CHUNKS):
        chunks = (step, _CHUNKS - 1 - step)
        u_base = heads(sol, chunks, 0, 2 * LANES)
        w_dec = heads(sol, chunks, LANES, 2 * LANES)
        qd = heads(q_dec, chunks, 0, LANES)
        kd = heads(k_dec, chunks, 0, LANES)
        cd = heads(c_dec2, chunks, 0, LANES)[:, :, :1]
        ws_qs = _bmm(jnp.concatenate([w_dec, qd], axis=1), s)
        u = u_base - ws_qs[:, :cs]
        u_bd = jnp.stack([jnp.concatenate([jnp.concatenate([u[2 * i], zero_u], axis=-1),
                                           jnp.concatenate([zero_u, u[2 * i + 1]], axis=-1)], axis=0)
                          for i in range(DN_HEADS)], axis=0)
        aqk_sel = jnp.stack([a_qk[d * npd + chunks[d] * hp_n + hp] for d in range(2) for hp in range(hp_n)], axis=0)
        o_loc = _bmm(aqk_sel, u_bd)
        s = s * cd + _bmm_tn(kd, u)
        for d, o_ref in enumerate((of_ref, ob_ref)):
            c = chunks[d]
            for h in range(DN_HEADS):
                i = d * DN_HEADS + h
                o_ref[0, c * cs:(c + 1) * cs, h * LANES:(h + 1) * LANES] = (
                    ws_qs[i, cs:] + o_loc[i // 2][:, (h % 2) * LANES:(h % 2 + 1) * LANES]).astype(o_ref.dtype)
    s_scr[...] = s.reshape(2, DN_HEADS, DN_DK, DN_DV)


def _dnscan_kernel(*refs, nt, has_s0, want_state):
    it = iter(refs)
    fwd = [next(it) for _ in range(5)]
    bwd = [next(it) for _ in range(5)]
    par = [next(it) for _ in range(4)]
    s0_ref = next(it) if has_s0 else None
    of_ref = next(it)
    ob_ref = next(it)
    st_ref = next(it) if want_state else None
    s_scr = next(it)
    t = pl.program_id(1)

    @pl.when(t == 0)
    def _():
        if has_s0:
            s_scr[...] = s0_ref[0]
        else:
            s_scr[...] = jnp.zeros(s_scr.shape, F32)

    _dn_tile_pair(fwd, bwd, par, of_ref, ob_ref, s_scr)

    if want_state:
        @pl.when(t == nt - 1)
        def _():
            st_ref[0] = s_scr[...]


def _dnscan(q, k, v, ab, abt, a_log, dt_bias, s0, want_state):
    bsz, seq, _ = q.shape
    nt = seq // DN_TILE

    def tile_specs(tmap):
        qs = pl.BlockSpec((1, DN_TILE, DN_V_W), lambda b, t: (b, tmap(t), 0))
        return [qs, qs, qs,
                pl.BlockSpec((1, DN_TILE, _N_GATES), lambda b, t: (b, tmap(t), 0)),
                pl.BlockSpec((1, _N_GATES, DN_TILE), lambda b, t: (b, 0, tmap(t)))]

    zeros8 = jnp.zeros((2 * DN_HEADS,), F32)
    alog16 = jnp.concatenate([a_log.reshape(-1), zeros8])
    dtb16 = jnp.concatenate([dt_bias.reshape(-1), zeros8])
    prow = pl.BlockSpec((1, _N_GATES), lambda b, t: (0, 0))
    pcol = pl.BlockSpec((_N_GATES, 1), lambda b, t: (0, 0))
    in_specs = tile_specs(lambda t: t) + tile_specs(lambda t: nt - 1 - t) + [prow, prow, pcol, pcol]
    args = [q, k, v, ab, abt, q, k, v, ab, abt,
            alog16.reshape(1, _N_GATES), dtb16.reshape(1, _N_GATES),
            alog16.reshape(_N_GATES, 1), dtb16.reshape(_N_GATES, 1)]
    st_block = (1, 2, DN_HEADS, DN_DK, DN_DV)
    if s0 is not None:
        in_specs.append(pl.BlockSpec(st_block, lambda b, t: (b, 0, 0, 0, 0)))
        args.append(s0)
    o_shape = jax.ShapeDtypeStruct((bsz, seq, DN_V_W), BF16)
    out_specs = [pl.BlockSpec((1, DN_TILE, DN_V_W), lambda b, t: (b, t, 0)),
                 pl.BlockSpec((1, DN_TILE, DN_V_W), lambda b, t: (b, nt - 1 - t, 0))]
    out_shape = [o_shape, o_shape]
    if want_state:
        out_specs.append(pl.BlockSpec(st_block, lambda b, t: (b, 0, 0, 0, 0)))
        out_shape.append(jax.ShapeDtypeStruct((bsz, 2, DN_HEADS, DN_DK, DN_DV), F32))
    return pl.pallas_call(
        functools.partial(_dnscan_kernel, nt=nt, has_s0=s0 is not None, want_state=want_state),
        grid=(bsz, nt),
        in_specs=in_specs,
        out_specs=out_specs,
        out_shape=out_shape,
        scratch_shapes=[pltpu.VMEM((2, DN_HEADS, DN_DK, DN_DV), F32)],
        compiler_params=_cparams(2),
        name="deltanet_scan",
    )(*args)


def _outproj_kernel(x_ref, a_ref, b_ref, of_ref, ob_ref, z_ref, ng_ref, ga_ref, w_ref, o_ref):
    o = of_ref[0].astype(F32) + ob_ref[0].astype(F32)
    z = z_ref[0].astype(F32)
    acc = jnp.dot(a_ref[0].astype(BF16), w_ref[0:CONV_W, :], preferred_element_type=F32)
    acc = acc + jnp.dot(b_ref[0].astype(BF16), w_ref[CONV_W:CONV_W + NA_W, :], preferred_element_type=F32)
    for h in range(DN_HEADS):
        cols = slice(h * DN_DV, (h + 1) * DN_DV)
        oh = o[:, cols]
        y = oh * lax.rsqrt(jnp.mean(oh * oh, axis=-1, keepdims=True) + EPS) * ng_ref[...] * _silu(z[:, cols])
        r0 = CONV_W + NA_W + h * DN_DV
        acc = acc + jnp.dot(y.astype(BF16), w_ref[r0:r0 + DN_DV, :], preferred_element_type=F32)
    o_ref[0] = x_ref[0] + ga_ref[0] * acc


def _outproj(x, out_a, out_b, o_f, o_b, z, norm_g, gate, w_b, layer):
    shape0 = x.shape
    tm = PROJ_TILE
    x, out_a, out_b, o_f, o_b, z = (_merge_short_sequences(a, tm, gate.shape[0] == 1)
                                    for a in (x, out_a, out_b, o_f, o_b, z))
    bsz, seq, d = x.shape

    def row(w):
        return pl.BlockSpec((1, tm, w), lambda b, t: (b, t, 0))

    return pl.pallas_call(
        _outproj_kernel,
        grid=(bsz, seq // tm),
        in_specs=[row(d), row(CONV_W), row(NA_W), row(DN_V_W), row(DN_V_W), row(DN_V_W),
                  pl.BlockSpec((1, DN_DV), lambda b, t: (0, 0)),
                  pl.BlockSpec((1, 1, d), _bmap(gate.shape[0])),
                  pl.BlockSpec((None, MIX_W, d), lambda b, t: (layer, 0, 0))],
        out_specs=row(d),
        out_shape=jax.ShapeDtypeStruct((bsz, seq, d), F32),
        compiler_params=_cparams(2),
        name="outproj",
    )(x, out_a, out_b, o_f, o_b, z, norm_g.reshape(1, -1), gate, w_b).reshape(shape0)


def _first_argmax(x, lane, valid):
    xm = jnp.where(valid, x, -jnp.inf)
    m = jnp.max(xm, axis=-1, keepdims=True)
    idx = jnp.min(jnp.where(valid & (xm == m), lane, float(LANES)), axis=-1, keepdims=True)
    return m, idx


def _route(h, rw_hi_ref, rw_lo_ref, rb_ref):
    h_hi = h.astype(BF16)
    h_lo = (h - h_hi.astype(F32)).astype(BF16)
    logits = (jnp.dot(h_hi, rw_hi_ref[...], preferred_element_type=F32)
              + jnp.dot(h_lo, rw_hi_ref[...], preferred_element_type=F32)
              + jnp.dot(h_hi, rw_lo_ref[...], preferred_element_type=F32)) + rb_ref[...]
    lane = lax.broadcasted_iota(jnp.int32, logits.shape, 1).astype(F32)
    is_grp = lane < N_GROUPS
    gmax, g_sel = _first_argmax(logits, lane, is_grp)
    pg_sel = 1.0 / jnp.sum(jnp.where(is_grp, jnp.exp(logits - gmax), 0.0), axis=-1, keepdims=True)
    e_lane = lane - N_GROUPS
    in_grp = (e_lane >= g_sel * EXPERTS_PER_GROUP) & (e_lane < (g_sel + 1) * EXPERTS_PER_GROUP)
    m1, i1 = _first_argmax(logits, lane, in_grp)
    m2, i2 = _first_argmax(logits, lane, in_grp & (lane != i1))
    e2 = jnp.exp(m2 - m1)
    w1 = pg_sel / (1.0 + e2)
    w2 = pg_sel * e2 / (1.0 + e2)
    return jnp.where(lane == i1, w1, 0.0) + jnp.where(lane == i2, w2, 0.0)


def _moe_kernel(x_ref, g_ref, sc_ref, sh_ref, ga_ref, rw_hi_ref, rw_lo_ref, rb_ref, w1_ref, w3_ref, w2_ref,
                gf_ref, o_ref, h_scr, gate_scr, acc_scr, *, final_norm):
    grp = pl.program_id(1)

    @pl.when(grp == 0)
    def _():
        h = _mod_norm(x_ref[...], g_ref[...], sc_ref[0], sh_ref[0])
        h_scr[...] = h.astype(BF16)
        gate_scr[...] = _route(h, rw_hi_ref, rw_lo_ref, rb_ref)
        acc_scr[...] = jnp.zeros(acc_scr.shape, F32)

    hb = h_scr[...]
    gates = gate_scr[...]
    lane = lax.broadcasted_iota(jnp.int32, gates.shape, 1)
    for half in range(EXPERTS_PER_GROUP // 2):
        hid = []
        for j in (2 * half, 2 * half + 1):
            gate_e = jnp.sum(jnp.where(lane == N_GROUPS + grp * EXPERTS_PER_GROUP + j, gates, 0.0),
                             axis=-1, keepdims=True)
            up = _silu(jnp.dot(hb, w1_ref[0, j], preferred_element_type=F32)) * jnp.dot(
                hb, w3_ref[0, j], preferred_element_type=F32)
            hid.append((up * gate_e).astype(BF16))
        rows = slice(2 * half * D_EXPERT, 2 * (half + 1) * D_EXPERT)
        acc_scr[...] += jnp.dot(jnp.concatenate(hid, axis=-1), w2_ref[0, 0, rows, :], preferred_element_type=F32)

    @pl.when(grp == N_GROUPS - 1)
    def _():
        y = x_ref[...] + ga_ref[0] * acc_scr[...]
        if final_norm:
            y = y * lax.rsqrt(jnp.mean(y * y, axis=-1, keepdims=True) + EPS) * gf_ref[...]
        o_ref[...] = y


def _moe(x, g, scale, shift, gate, rw_hi, rw_lo, rb, w1_b, w3_b, w2_b, g_final, layer, final_norm):
    bsz, seq, d = x.shape
    x2 = x.reshape(bsz * seq, d)
    if scale.shape[0] == 1:
        tm = MOE_TILE
        mm = lambda i, e: (0, 0, 0)
    else:
        tm = min(MOE_TILE, seq)
        per_seq = seq // tm
        mm = lambda i, e: (i // per_seq, 0, 0)
    vec = pl.BlockSpec((1, d), lambda i, e: (0, 0))
    mod = pl.BlockSpec((1, 1, d), mm)
    rspec = pl.BlockSpec((d, LANES), lambda i, e: (0, 0))
    gw = EXPERTS_PER_GROUP * D_EXPERT
    out = pl.pallas_call(
        functools.partial(_moe_kernel, final_norm=final_norm),
        grid=(bsz * seq // tm, N_GROUPS),
        in_specs=[pl.BlockSpec((tm, d), lambda i, e: (i, 0)), vec, mod, mod, mod,
                  rspec, rspec, pl.BlockSpec((1, LANES), lambda i, e: (0, 0)),
                  pl.BlockSpec((None, 1, EXPERTS_PER_GROUP, d, D_EXPERT), lambda i, e: (layer, e, 0, 0, 0)),
                  pl.BlockSpec((None, 1, EXPERTS_PER_GROUP, d, D_EXPERT), lambda i, e: (layer, e, 0, 0, 0)),
                  pl.BlockSpec((1, 1, gw, d), lambda i, e: (layer, e, 0, 0)),
                  vec],
        out_specs=pl.BlockSpec((tm, d), lambda i, e: (i, 0)),
        out_shape=jax.ShapeDtypeStruct((bsz * seq, d), F32),
        scratch_shapes=[pltpu.VMEM((tm, d), BF16), pltpu.VMEM((tm, LANES), F32), pltpu.VMEM((tm, d), F32)],
        compiler_params=_cparams(2),
        name="moe",
    )(x2, g, scale, shift, gate, rw_hi, rw_lo, rb, w1_b, w3_b, w2_b, g_final.reshape(1, -1))
    return out.reshape(bsz, seq, d)


def _split_hi_lo(w):
    hi = w.astype(BF16)
    return hi, (w - hi.astype(F32)).astype(BF16)


def kernel(x_prompt, x_sample, cache_na_k, cache_na_v, state_delta, c, c_ctx, w_mod, b_mod, g_norm1, g_norm2,
           w_in, conv_a_w, conv_a_b, ln_a_g, ln_a_b, na_rel_bias, dn_conv_w, dn_a_log, dn_dt_bias, dn_norm_g,
           w_out, router_wg, router_bg, router_we, router_be, w1, w3, w2, g_final):
    n_dec = x_sample.shape[0]
    d = D_MODEL
    cond8 = jnp.concatenate([c_ctx[None, :], c, jnp.zeros((8 - 1 - n_dec, d), F32)], axis=0)
    mods = _adaln(cond8, w_mod, b_mod).reshape(DEPTH, 8, 6, d)

    w_in_b = w_in.astype(BF16)
    w_abt_b = jnp.swapaxes(w_in[:, :, OFF_DN_AB:], 1, 2).astype(BF16)
    w_out_b = w_out.astype(BF16)
    w1_b = w1.astype(BF16).reshape(DEPTH, N_GROUPS, EXPERTS_PER_GROUP, d, D_EXPERT)
    w3_b = w3.astype(BF16).reshape(DEPTH, N_GROUPS, EXPERTS_PER_GROUP, d, D_EXPERT)
    w2_b = w2.astype(BF16).reshape(DEPTH, N_GROUPS, EXPERTS_PER_GROUP * D_EXPERT, d)
    pad = LANES - N_GROUPS - N_EXPERTS
    rw = jnp.concatenate([router_wg, router_we, jnp.zeros((DEPTH, d, pad), F32)], axis=-1)
    rw_hi, rw_lo = _split_hi_lo(rw)
    rb = jnp.concatenate([router_bg, router_be, jnp.zeros((DEPTH, pad), F32)], axis=-1)

    past = cache_na_k.shape[2]
    cache_k = cache_na_k.reshape(n_dec, DEPTH, past, NA_W)
    cache_v = cache_na_v.reshape(n_dec, DEPTH, past, NA_W)
    rows = x_sample.shape[1] // GRID_W

    def layer(x, l, is_ctx):
        m = mods[l, 0:1] if is_ctx else mods[l, 1:1 + n_dec]
        sh1, sc1, ga1, sh2, sc2, ga2 = (m[:, i:i + 1, :] for i in range(6))
        glu, q_na, k_na, v_na, dqkv, dz, ab, abt = _inproj(x, g_norm1[l:l + 1], sc1, sh1, w_in_b, w_abt_b, l)
        out_a = _cconv(glu, conv_a_w[l], conv_a_b[l], ln_a_g[l], ln_a_b[l])
        if is_ctx:
            out_b = _ctx_attn(q_na, k_na, v_na)
        else:
            out_b = _na_attn(q_na, k_na, v_na, cache_k, cache_v, l, _na_bias_tables(na_rel_bias[l], rows))
        dq, dk, dv = _dnprep(dqkv, dn_conv_w[l], use_rope=not is_ctx)
        scan = _dnscan(dq, dk, dv, ab, abt, dn_a_log[l], dn_dt_bias[l],
                       None if is_ctx else state_delta[:, l], want_state=is_ctx)
        o_f, o_b = scan[0], scan[1]
        x = _outproj(x, out_a, out_b, o_f, o_b, dz, dn_norm_g[l], ga1, w_out_b, l)
        x = _moe(x, g_norm2[l:l + 1], sc2, sh2, ga2, rw_hi[l], rw_lo[l], rb[l:l + 1], w1_b, w3_b, w2_b,
                 g_final, layer=l, final_norm=(l == DEPTH - 1))
        return x, k_na, v_na, (scan[2] if is_ctx else None)

    xc = x_prompt
    new_k, new_v, new_s = [], [], []
    for l in range(DEPTH):
        xc, k_na, v_na, s_ctx = layer(xc, l, True)
        new_k.append(k_na.reshape(k_na.shape[0], k_na.shape[1], NA_HEADS, NA_HEAD_DIM))
        new_v.append(v_na.reshape(v_na.shape[0], v_na.shape[1], NA_HEADS, NA_HEAD_DIM))
        new_s.append(s_ctx)
    xs = x_sample
    for l in range(DEPTH):
        xs, _, _, _ = layer(xs, l, False)
    return (xc, xs, jnp.stack(new_k, axis=1), jnp.stack(new_v, axis=1), jnp.stack(new_s, axis=1))
```

```python
import functools
import math

import numpy as np
import jax
import jax.numpy as jnp
from jax import lax
from jax.experimental import pallas as pl
from jax.experimental.pallas import tpu as pltpu

F32 = jnp.float32
BF16 = jnp.bfloat16

D_MODEL = 1024
DEPTH = 2
GRID_W = 64
CONV_W = 256
CONV_K = 31
NA_HEADS = 4
NA_HEAD_DIM = 64
NA_W = NA_HEADS * NA_HEAD_DIM
NA_ROWS = 8
NA_COLS = 16
DN_HEADS = 4
DN_DK = 128
DN_DV = 128
DN_QK_W = DN_HEADS * DN_DK
DN_V_W = DN_HEADS * DN_DV
DN_CONV_K = 5
DN_CHUNK = 64
ROPE_BASE = 10000.0
MIX_W = CONV_W + NA_W + DN_V_W
OFF_NA = 2 * CONV_W
OFF_DN_QKV = OFF_NA + 3 * NA_W
OFF_DN_Z = OFF_DN_QKV + 2 * DN_QK_W + DN_V_W
OFF_DN_AB = OFF_DN_Z + DN_V_W
PROJ_W = OFF_DN_AB + 4 * DN_HEADS
N_GROUPS = 4
EXPERTS_PER_GROUP = 4
N_EXPERTS = N_GROUPS * EXPERTS_PER_GROUP
D_EXPERT = 256
EPS = 1e-6
NEG_INF = -1e30

LANES = 128
SUBLANES = 8
ROW_TILE = 256
PROJ_TILE = 512
DN_TILE = 256
NA_QROWS = 4
NA_WIN_ROWS = 12
MOE_TILE = 1024
MOE_ROWS = 320
VMEM_LIMIT = 48 * 1024 * 1024

_NT = (((1,), (1,)), ((), ()))


def _cparams(n_axes):
    return pltpu.CompilerParams(dimension_semantics=("arbitrary",) * n_axes,
                                vmem_limit_bytes=VMEM_LIMIT)


def _sigmoid(x):
    return 1.0 / (1.0 + jnp.exp(-x))


def _silu(x):
    return x * _sigmoid(x)


def _softplus(x):
    return jnp.maximum(x, 0.0) + jnp.log(1.0 + jnp.exp(-jnp.abs(x)))


def _dot(a, b):
    return jnp.dot(a.astype(BF16), b.astype(BF16), preferred_element_type=F32)


def _adaln_kernel(c_ref, w_ref, b_ref, o_ref):
    c = c_ref[...]
    o_ref[0] = _dot(_silu(c), w_ref[0]) + b_ref[0]


def _adaln(cond8, w_mod, b_mod):
    n_l, d, n6 = w_mod.shape
    tn = 1536
    return pl.pallas_call(
        _adaln_kernel,
        grid=(n_l, n6 // tn),
        in_specs=[pl.BlockSpec((8, d), lambda l, j: (0, 0)),
                  pl.BlockSpec((1, d, tn), lambda l, j: (l, 0, j)),
                  pl.BlockSpec((1, 1, tn), lambda l, j: (l, 0, j))],
        out_specs=pl.BlockSpec((1, 8, tn), lambda l, j: (l, 0, j)),
        out_shape=jax.ShapeDtypeStruct((n_l, 8, n6), F32),
        compiler_params=_cparams(2),
        name="adaln",
    )(cond8, w_mod, b_mod.reshape(n_l, 1, n6))


def _mod_norm(x, g, scale, shift):
    ms = jnp.mean(x * x, axis=-1, keepdims=True)
    return (x * lax.rsqrt(ms + EPS) * g) * (1.0 + scale) + shift


def _inproj_kernel(x_ref, g_ref, sc_ref, sh_ref, w_ref, wabt_ref,
                   glu_ref, q_ref, k_ref, v_ref, dqkv_ref, z_ref, ab_ref, abt_ref):
    hb = _mod_norm(x_ref[0], g_ref[...], sc_ref[0], sh_ref[0]).astype(BF16)

    def proj(a, b):
        return jnp.dot(hb, w_ref[:, a:b], preferred_element_type=F32)

    glu_ref[0] = proj(0, OFF_NA)
    q_ref[0] = proj(OFF_NA, OFF_NA + NA_W).astype(BF16)
    k_ref[0] = proj(OFF_NA + NA_W, OFF_NA + 2 * NA_W)
    v_ref[0] = proj(OFF_NA + 2 * NA_W, OFF_DN_QKV)
    dqkv_ref[0] = proj(OFF_DN_QKV, OFF_DN_Z).astype(BF16)
    z_ref[0] = proj(OFF_DN_Z, OFF_DN_AB).astype(BF16)
    ab_ref[0] = proj(OFF_DN_AB, PROJ_W)
    abt_ref[0] = lax.dot_general(wabt_ref[...], hb, _NT, preferred_element_type=F32)


def _bmap(bm):
    if bm == 1:
        return lambda b, t: (0, 0, 0)
    return lambda b, t: (b, 0, 0)


def _merge_short_sequences(x, tm, shared_mod):
    bsz, seq, d = x.shape
    if shared_mod and seq < tm:
        return x.reshape(bsz * seq // tm, tm, d)
    return x


def _inproj(x, g, scale, shift, w_b, wabt_b, layer):
    bsz0, seq0, _ = x.shape
    tm = PROJ_TILE
    x = _merge_short_sequences(x, tm, scale.shape[0] == 1)
    outs = _inproj_call(x, g, scale, shift, w_b, wabt_b, layer, tm)
    if x.shape[0] == bsz0:
        return outs
    per = tm // seq0
    row_outs = [o.reshape(bsz0, seq0, o.shape[-1]) for o in outs[:-1]]
    abt = outs[-1].reshape(x.shape[0], 4 * DN_HEADS, per, seq0)
    abt = jnp.swapaxes(abt, 1, 2).reshape(bsz0, 4 * DN_HEADS, seq0)
    return row_outs + [abt]


def _inproj_call(x, g, scale, shift, w_b, wabt_b, layer, tm):
    bsz, seq, d = x.shape
    widths = (OFF_NA, NA_W, NA_W, NA_W, OFF_DN_Z - OFF_DN_QKV, DN_V_W, 4 * DN_HEADS)
    dtypes = (F32, BF16, F32, F32, BF16, BF16, F32)
    out_shape = [jax.ShapeDtypeStruct((bsz, seq, w), dt) for w, dt in zip(widths, dtypes)]
    out_shape.append(jax.ShapeDtypeStruct((bsz, 4 * DN_HEADS, seq), F32))
    out_specs = [pl.BlockSpec((1, tm, w), lambda b, t: (b, t, 0)) for w in widths]
    out_specs.append(pl.BlockSpec((1, 4 * DN_HEADS, tm), lambda b, t: (b, 0, t)))
    mm = _bmap(scale.shape[0])
    return pl.pallas_call(
        _inproj_kernel,
        grid=(bsz, seq // tm),
        in_specs=[pl.BlockSpec((1, tm, d), lambda b, t: (b, t, 0)),
                  pl.BlockSpec((1, d), lambda b, t: (0, 0)),
                  pl.BlockSpec((1, 1, d), mm),
                  pl.BlockSpec((1, 1, d), mm),
                  pl.BlockSpec((None, d, PROJ_W), lambda b, t: (layer, 0, 0)),
                  pl.BlockSpec((None, 4 * DN_HEADS, d), lambda b, t: (layer, 0, 0))],
        out_specs=out_specs,
        out_shape=out_shape,
        compiler_params=_cparams(2),
        name="inproj",
    )(x, g, scale, shift, w_b, wabt_b)


_CC_HALO = 16
_CC_SUB = 64


def _cconv_kernel(main_ref, prev_ref, next_ref, w_ref, b_ref, g_ref, beta_ref, o_ref, scr, *, tt, nt):
    t = pl.program_id(1)

    def glu(a):
        return a[:, :CONV_W] * _sigmoid(a[:, CONV_W:])

    scr[0, 0:_CC_HALO, :] = jnp.where(t > 0, glu(prev_ref[0]), 0.0)
    scr[0, _CC_HALO:_CC_HALO + tt, :] = glu(main_ref[0])
    scr[0, _CC_HALO + tt:2 * _CC_HALO + tt, :] = jnp.where(t < nt - 1, glu(next_ref[0]), 0.0)
    n_shift = tt + 2 * _CC_HALO - SUBLANES
    for r in range(1, SUBLANES):
        scr[r, 0:n_shift, :] = scr[0, r:r + n_shift, :]
    pad = (CONV_K - 1) // 2
    for s in range(tt // _CC_SUB):
        base = s * _CC_SUB + _CC_HALO - pad
        acc = jnp.zeros((_CC_SUB, CONV_W), F32) + b_ref[...]
        for k in range(CONV_K):
            r = (base + k) % SUBLANES
            a0 = base + k - r
            acc = acc + w_ref[k:k + 1, :] * scr[r, a0:a0 + _CC_SUB, :]
        mu = jnp.mean(acc, axis=-1, keepdims=True)
        xc = acc - mu
        var = jnp.mean(xc * xc, axis=-1, keepdims=True)
        y = xc * lax.rsqrt(var + EPS) * g_ref[...] + beta_ref[...]
        o_ref[0, s * _CC_SUB:(s + 1) * _CC_SUB, :] = _silu(y).astype(BF16)


def _halo_specs(tt, seq, halo, width):
    per = tt // halo
    last = seq // halo - 1
    prev = pl.BlockSpec((1, halo, width), lambda b, t: (b, jnp.maximum(t * per - 1, 0), 0))
    nxt = pl.BlockSpec((1, halo, width), lambda b, t: (b, jnp.minimum((t + 1) * per, last), 0))
    return prev, nxt


def _cconv(glu, w, b, g, beta):
    bsz, seq, _ = glu.shape
    tt = ROW_TILE
    nt = seq // tt
    prev, nxt = _halo_specs(tt, seq, _CC_HALO, 2 * CONV_W)
    vec = pl.BlockSpec((1, CONV_W), lambda b_, t: (0, 0))
    return pl.pallas_call(
        functools.partial(_cconv_kernel, tt=tt, nt=nt),
        grid=(bsz, nt),
        in_specs=[pl.BlockSpec((1, tt, 2 * CONV_W), lambda b_, t: (b_, t, 0)), prev, nxt,
                  pl.BlockSpec((CONV_K, CONV_W), lambda b_, t: (0, 0)), vec, vec, vec],
        out_specs=pl.BlockSpec((1, tt, CONV_W), lambda b_, t: (b_, t, 0)),
        out_shape=jax.ShapeDtypeStruct((bsz, seq, CONV_W), BF16),
        scratch_shapes=[pltpu.VMEM((SUBLANES, tt + 2 * _CC_HALO, CONV_W), F32)],
        compiler_params=_cparams(2),
        name="conformer_conv",
    )(glu, glu, glu, w, b.reshape(1, -1), g.reshape(1, -1), beta.reshape(1, -1))


def _head_lane_id(shape):
    return lax.shift_right_logical(lax.broadcasted_iota(jnp.int32, shape, 1), 6)


def _ctx_attn_kernel(q_ref, k_ref, v_ref, o_ref):
    q = q_ref[0].astype(F32) * (NA_HEAD_DIM ** -0.5)
    k = k_ref[0].astype(BF16)
    v = v_ref[0].astype(BF16)
    head = _head_lane_id(q.shape)
    out = jnp.zeros(q.shape, F32)
    for h in range(NA_HEADS):
        s = lax.dot_general(jnp.where(head == h, q, 0.0).astype(BF16), k, _NT, preferred_element_type=F32)
        m = jnp.max(s, axis=-1, keepdims=True)
        p = jnp.exp(s - m)
        den = jnp.sum(p, axis=-1, keepdims=True)
        oh = jnp.dot(p.astype(BF16), v, preferred_element_type=F32)
        out = jnp.where(head == h, oh / den, out)
    o_ref[0] = out.astype(BF16)


def _ctx_attn(q, k, v):
    bsz, seq, w = q.shape
    spec = pl.BlockSpec((1, seq, w), lambda b: (b, 0, 0))
    return pl.pallas_call(
        _ctx_attn_kernel,
        grid=(bsz,),
        in_specs=[spec, spec, spec],
        out_specs=spec,
        out_shape=jax.ShapeDtypeStruct((bsz, seq, w), BF16),
        compiler_params=_cparams(1),
        name="context_attention",
    )(q, k, v)


def _na_window_start(blk, rows):
    return jnp.clip(blk * NA_QROWS - NA_ROWS // 2, 0, rows - NA_WIN_ROWS)


def _na_kernel(q_ref, k_ref, v_ref, kc_ref, vc_ref, eb_ref, o_ref, *, rows):
    blk = pl.program_id(1)
    ws = pl.multiple_of(_na_window_start(blk, rows) * GRID_W, GRID_W)
    nwin = NA_WIN_ROWS * GRID_W
    q = q_ref[0].astype(F32) * (NA_HEAD_DIM ** -0.5)
    kl = k_ref[0, pl.ds(ws, nwin), :].astype(BF16)
    vl = v_ref[0, pl.ds(ws, nwin), :].astype(BF16)
    kc = kc_ref[...].astype(BF16)
    vc = vc_ref[...].astype(BF16)
    head = _head_lane_id(q.shape)
    out = jnp.zeros(q.shape, F32)
    for h in range(NA_HEADS):
        qh = jnp.where(head == h, q, 0.0).astype(BF16)
        sl = lax.dot_general(qh, kl, _NT, preferred_element_type=F32) + eb_ref[0, h]
        sc = lax.dot_general(qh, kc, _NT, preferred_element_type=F32)
        m = jnp.maximum(jnp.max(sl, axis=-1, keepdims=True), jnp.max(sc, axis=-1, keepdims=True))
        p_l = jnp.exp(sl - m)
        p_c = jnp.exp(sc - m)
        den = jnp.sum(p_l, axis=-1, keepdims=True) + jnp.sum(p_c, axis=-1, keepdims=True)
        oh = (jnp.dot(p_l.astype(BF16), vl, preferred_element_type=F32)
              + jnp.dot(p_c.astype(BF16), vc, preferred_element_type=F32))
        out = jnp.where(head == h, oh / den, out)
    o_ref[0] = out.astype(BF16)


def _na_bias_tables(rel_bias, rows):
    n_heads = rel_bias.shape[0]
    c = np.arange(GRID_W)
    cs = np.clip(c - NA_COLS // 2, 0, GRID_W - NA_COLS)
    col_valid = (c[None, :] >= cs[:, None]) & (c[None, :] < cs[:, None] + NA_COLS)
    padw = GRID_W - NA_COLS
    padded = jnp.pad(rel_bias, ((0, 0), (0, 0), (padw, padw)))
    toeplitz = jnp.stack([padded[:, :, GRID_W - 1 - ci:2 * GRID_W - 1 - ci] for ci in range(GRID_W)], axis=2)
    toeplitz = jnp.where(jnp.asarray(col_valid)[None, None], toeplitz, NEG_INF)
    masked = jnp.full((n_heads, GRID_W, GRID_W), NEG_INF, F32)
    nblk = rows // NA_QROWS
    tables = []
    for blk in (0, 1, nblk - 1):
        ws = int(np.clip(blk * NA_QROWS - NA_ROWS // 2, 0, rows - NA_WIN_ROWS))
        q_rows = []
        for a in range(NA_QROWS):
            r = blk * NA_QROWS + a
            rs = int(np.clip(r - NA_ROWS // 2, 0, rows - NA_ROWS))
            blocks = []
            for j in range(NA_WIN_ROWS):
                kr = ws + j
                blocks.append(toeplitz[:, kr - r + NA_ROWS - 1] if rs <= kr < rs + NA_ROWS else masked)
            q_rows.append(jnp.concatenate(blocks, axis=-1))
        tables.append(jnp.concatenate(q_rows, axis=1))
    return jnp.stack(tables, axis=0)


def _na_attn(q, k, v, cache_k, cache_v, layer, eb):
    bsz, seq, w = q.shape
    rows = seq // GRID_W
    nblk = rows // NA_QROWS
    tq = NA_QROWS * GRID_W
    past = cache_k.shape[2]
    full = pl.BlockSpec((1, seq, w), lambda b, r: (b, 0, 0))
    ctx = pl.BlockSpec((None, None, past, w), lambda b, r: (b, layer, 0, 0))

    def eb_map(b, r):
        return (jnp.where(r == 0, 0, jnp.where(r == nblk - 1, 2, 1)), 0, 0, 0)

    return pl.pallas_call(
        functools.partial(_na_kernel, rows=rows),
        grid=(bsz, nblk),
        in_specs=[pl.BlockSpec((1, tq, w), lambda b, r: (b, r, 0)), full, full, ctx, ctx,
                  pl.BlockSpec((1, NA_HEADS, tq, NA_WIN_ROWS * GRID_W), eb_map)],
        out_specs=pl.BlockSpec((1, tq, w), lambda b, r: (b, r, 0)),
        out_shape=jax.ShapeDtypeStruct((bsz, seq, w), BF16),
        compiler_params=_cparams(2),
        name="neighbourhood_attention",
    )(q, k, v, cache_k, cache_v, eb)


_DN_HALO = 16
_DN_RSUB = 128
_DN_CGRP = 2 * LANES


def _dnprep_kernel(*refs, tt, nt, use_rope):
    if use_rope:
        main_ref, prev_ref, next_ref, w_ref, cos_ref, sin_ref, q_ref, k_ref, v_ref = refs
    else:
        main_ref, prev_ref, next_ref, w_ref, q_ref, k_ref, v_ref = refs
    t = pl.program_id(1)
    main = main_ref[0]
    halo = jnp.concatenate([jnp.where(t > 0, prev_ref[0], jnp.zeros_like(prev_ref[0])),
                            jnp.where(t < nt - 1, next_ref[0], jnp.zeros_like(next_ref[0]))], axis=0)
    pad = (DN_CONV_K - 1) // 2
    row = lax.broadcasted_iota(jnp.int32, (tt, tt), 0)
    col = lax.broadcasted_iota(jnp.int32, (tt, tt), 1)
    hrow = lax.broadcasted_iota(jnp.int32, (tt, 2 * _DN_HALO), 0)
    hcol = lax.broadcasted_iota(jnp.int32, (tt, 2 * _DN_HALO), 1)
    htok = jnp.where(hcol < _DN_HALO, hcol - _DN_HALO, tt + hcol - _DN_HALO)
    sel = {}
    for kk in range(DN_CONV_K):
        if kk != pad:
            sel[kk] = (jnp.where(col == row + (kk - pad), 1.0, 0.0).astype(BF16),
                       jnp.where(htok == hrow + (kk - pad), 1.0, 0.0).astype(BF16))
    outs = (q_ref, k_ref, v_ref)
    if use_rope:
        lane = lax.broadcasted_iota(jnp.int32, (_DN_RSUB, LANES), 1)
        first_half = (lane & (DN_DK // 4)) == 0
    for cg in range(3 * DN_QK_W // _DN_CGRP):
        gcols = slice(cg * _DN_CGRP, (cg + 1) * _DN_CGRP)
        taps = {}
        for kk in range(DN_CONV_K):
            if kk == pad:
                taps[kk] = main[:, gcols].astype(F32)
            else:
                taps[kk] = (jnp.dot(sel[kk][0], main[:, gcols], preferred_element_type=F32)
                            + jnp.dot(sel[kk][1], halo[:, gcols], preferred_element_type=F32))
        for sub in range(_DN_CGRP // LANES):
            cb = cg * (_DN_CGRP // LANES) + sub
            cols = slice(cb * LANES, (cb + 1) * LANES)
            lcols = slice(sub * LANES, (sub + 1) * LANES)
            for rs in range(tt // _DN_RSUB):
                rsl = slice(rs * _DN_RSUB, (rs + 1) * _DN_RSUB)
                acc = jnp.zeros((_DN_RSUB, LANES), F32)
                for kk in range(DN_CONV_K):
                    acc = acc + w_ref[kk:kk + 1, cols] * taps[kk][rsl, lcols]
                y = _silu(acc)
                if cb < 2 * DN_HEADS:
                    y = y * lax.rsqrt(jnp.sum(y * y, axis=-1, keepdims=True) + EPS)
                    if use_rope:
                        quarter = DN_DK // 4
                        partner = jnp.where(first_half, pltpu.roll(y, LANES - quarter, 1), pltpu.roll(y, quarter, 1))
                        y = y * cos_ref[rsl, :] + partner * sin_ref[rsl, :]
                    if cb < DN_HEADS:
                        y = y * (DN_DK ** -0.5)
                hh = cb % DN_HEADS
                outs[cb // DN_HEADS][0, rsl, hh * LANES:(hh + 1) * LANES] = y


def _rope_lane_tables(seq):
    pos = np.arange(seq)
    n_freq = DN_DK // 4
    inv = ROPE_BASE ** (-np.arange(n_freq) / n_freq)
    ang = np.stack([(pos // GRID_W)[:, None] * inv[None, :], (pos % GRID_W)[:, None] * inv[None, :]], axis=1)
    cos = np.cos(ang).astype(np.float32)
    sin = np.sin(ang).astype(np.float32)
    lane = np.arange(DN_DK)
    axis, half, freq = lane // (2 * n_freq), (lane // n_freq) % 2, lane % n_freq
    cos_l = cos[:, axis, freq]
    sin_l = sin[:, axis, freq] * np.where(half == 0, -1.0, 1.0).astype(np.float32)[None, :]
    return jnp.asarray(cos_l), jnp.asarray(sin_l)


def _dnprep(dqkv, conv_w, use_rope):
    bsz, seq, width = dqkv.shape
    tt = ROW_TILE
    nt = seq // tt
    prev, nxt = _halo_specs(tt, seq, _DN_HALO, width)
    in_specs = [pl.BlockSpec((1, tt, width), lambda b, t: (b, t, 0)), prev, nxt,
                pl.BlockSpec((DN_CONV_K, width), lambda b, t: (0, 0))]
    args = [dqkv, dqkv, dqkv, conv_w]
    if use_rope:
        tab = pl.BlockSpec((tt, DN_DK), lambda b, t: (t, 0))
        in_specs += [tab, tab]
        args += list(_rope_lane_tables(seq))
    out = jax.ShapeDtypeStruct((bsz, seq, DN_QK_W), F32)
    ospec = pl.BlockSpec((1, tt, DN_QK_W), lambda b, t: (b, t, 0))
    return pl.pallas_call(
        functools.partial(_dnprep_kernel, tt=tt, nt=nt, use_rope=use_rope),
        grid=(bsz, nt),
        in_specs=in_specs,
        out_specs=[ospec, ospec, ospec],
        out_shape=[out, out, out],
        compiler_params=_cparams(2),
        name="deltanet_prep",
    )(*args)


_CHUNKS = DN_TILE // DN_CHUNK
_LOG_CHUNK = int(math.log2(DN_CHUNK))
_N_GATES = 4 * DN_HEADS


def _bmm(a, b):
    return lax.dot_general(a.astype(BF16), b.astype(BF16), (((2,), (1,)), ((0,), (0,))),
                           preferred_element_type=F32)


def _bmm_nt(a, b):
    return lax.dot_general(a.astype(BF16), b.astype(BF16), (((2,), (2,)), ((0,), (0,))),
                           preferred_element_type=F32)


def _bmm_tn(a, b):
    return lax.dot_general(a.astype(BF16), b.astype(BF16), (((1,), (1,)), ((0,), (0,))),
                           preferred_element_type=F32)


def _pair_cols(col_a, col_b, width):
    lane = lax.broadcasted_iota(jnp.int32, (col_a.shape[0], width), 1)
    return jnp.where(lane < width // 2, col_a, col_b)


def _block_diag(y):
    lane = lax.broadcasted_iota(jnp.int32, y.shape, 2)
    first = lane < y.shape[2] // 2
    return jnp.concatenate([jnp.where(first, y, 0.0), jnp.where(first, 0.0, y)], axis=1)


def _chunk_cumsum(x, axis, reverse):
    n = x.shape[axis]
    pos = lax.broadcasted_iota(jnp.int32, x.shape, axis) & (DN_CHUNK - 1)
    step = 1
    while step < DN_CHUNK:
        if reverse:
            x = x + jnp.where(pos < DN_CHUNK - step, pltpu.roll(x, n - step, axis), 0.0)
        else:
            x = x + jnp.where(pos >= step, pltpu.roll(x, step, axis), 0.0)
        step *= 2
    return x


def _dn_tile_pair(fwd, bwd, par, of_ref, ob_ref, s_scr):
    alog_ref, dtb_ref, alogt_ref, dtbt_ref = par
    cs = DN_CHUNK
    pw = 2 * cs
    hp_n = DN_HEADS // 2
    npd = _CHUNKS * hp_n
    ii = lax.broadcasted_iota(jnp.int32, (cs, pw), 0)
    jj = lax.broadcasted_iota(jnp.int32, (cs, pw), 1) & (cs - 1)
    sx = jnp.stack([jnp.where(ii >= jj, ii ^ jj, -1), jnp.where(ii <= jj, ii ^ jj, -1)], axis=0)[:, None]
    incl = sx >= 0
    before = sx > 0

    gc_l, gct_l, beta_l, gc2_l, beta2_l, tot2_l, q_l, k_l, v_l = ([] for _ in range(9))
    for d, (q_ref, k_ref, v_ref, ab_ref, abt_ref) in enumerate((fwd, bwd)):
        ab = ab_ref[0]
        abt = abt_ref[0]
        g_col = -jnp.exp(alog_ref[...]) * _softplus(ab + dtb_ref[...])
        g_row = -jnp.exp(alogt_ref[...]) * _softplus(abt + dtbt_ref[...])
        beta_all = _sigmoid(ab)
        gc_rows = jnp.concatenate([_chunk_cumsum(g_row[:, i * LANES:(i + 1) * LANES], 1, d == 1)
                                   for i in range(DN_TILE // LANES)], axis=1)
        qx = q_ref[0]
        kx = k_ref[0]
        vx = v_ref[0]
        for c in range(_CHUNKS):
            rows = slice(c * cs, (c + 1) * cs)
            gc_c = _chunk_cumsum(g_col[rows], 0, d == 1)
            gct_c = gc_rows[:, rows]
            last = 0 if d == 1 else cs - 1
            tot_c = gc_c[last:last + 1, :]
            for hp in range(hp_n):
                ga = d * DN_HEADS + 2 * hp
                gb = ga + 1
                ba = ga + 2 * DN_HEADS
                bb = ba + 1
                gc_l.append(_pair_cols(gc_c[:, ga:ga + 1], gc_c[:, gb:gb + 1], pw))
                gct_l.append(jnp.concatenate([gct_c[ga:ga + 1, :], gct_c[gb:gb + 1, :]], axis=1))
                beta_l.append(_pair_cols(beta_all[rows, ba:ba + 1], beta_all[rows, bb:bb + 1], pw))
                gc2_l.append(_pair_cols(gc_c[:, ga:ga + 1], gc_c[:, gb:gb + 1], 2 * LANES))
                beta2_l.append(_pair_cols(beta_all[rows, ba:ba + 1], beta_all[rows, bb:bb + 1], 2 * LANES))
                tot2_l.append(_pair_cols(tot_c[:, ga:ga + 1], tot_c[:, gb:gb + 1], 2 * LANES))
                cols = slice(hp * 2 * LANES, (hp + 1) * 2 * LANES)
                q_l.append(qx[rows, cols])
                k_l.append(kx[rows, cols])
                v_l.append(vx[rows, cols])

    def stk(xs):
        return jnp.stack(xs, axis=0)

    gc, gc_t, beta = stk(gc_l), stk(gct_l), stk(beta_l)
    gc2, beta2, tot2 = stk(gc2_l), stk(beta2_l), stk(tot2_l)
    q, k, v = stk(q_l), stk(k_l), stk(v_l)

    def by_dir(x):
        return x.reshape((2, npd) + x.shape[1:])

    def flat(x):
        return x.reshape((2 * npd,) + x.shape[2:])

    k_bd = _block_diag(k)
    kk = by_dir(_bmm_nt(k, k_bd))
    qk = by_dir(_bmm_nt(q, k_bd))
    decay = jnp.where(incl, jnp.exp(jnp.where(incl, by_dir(gc - gc_t), 0.0)), 0.0)
    a_mat = jnp.where(before, (by_dir(beta) * kk) * decay, 0.0)
    a_qk = flat(qk * decay)

    e_mat = flat(-jnp.where(sx == 1, a_mat, 0.0))
    for lb in range(1, _LOG_CHUNK):
        l_b = flat(jnp.where(lax.shift_right_arithmetic(sx, lb) == 1, a_mat, 0.0))
        p_mat = l_b + _bmm(l_b, _block_diag(e_mat))
        e_mat = e_mat - p_mat - _bmm(e_mat, _block_diag(p_mat))

    eg2 = jnp.exp(gc2)
    vb = v * beta2
    kb = k * (beta2 * eg2)
    zero2 = jnp.zeros(vb.shape, F32)
    rhs = jnp.concatenate([vb[..., :LANES], kb[..., :LANES], vb[..., LANES:], kb[..., LANES:]], axis=-1)
    rhs_bd = jnp.concatenate([jnp.concatenate([vb[..., :LANES], kb[..., :LANES], zero2], axis=-1),
                              jnp.concatenate([zero2, vb[..., LANES:], kb[..., LANES:]], axis=-1)], axis=1)
    sol = rhs + _bmm(e_mat, rhs_bd)
    q_dec = q * eg2
    k_dec = k * jnp.exp(tot2 - gc2)
    c_dec2 = jnp.exp(tot2)

    def heads(x, chunks, off, stride):
        out = []
        for d in range(2):
            for hp in range(hp_n):
                p = d * npd + chunks[d] * hp_n + hp
                out += [x[p][:, off:off + LANES], x[p][:, off + stride:off + stride + LANES]]
        return jnp.stack(out, axis=0)

    s = s_scr[...].reshape(2 * DN_HEADS, DN_DK, DN_DV)
    zero_u = jnp.zeros((cs, LANES), F32)
    for step in range(_CHUNKS):
        chunks = (step, _CHUNKS - 1 - step)
        u_base = heads(sol, chunks, 0, 2 * LANES)
        w_dec = heads(sol, chunks, LANES, 2 * LANES)
        qd = heads(q_dec, chunks, 0, LANES)
        kd = heads(k_dec, chunks, 0, LANES)
        cd = heads(c_dec2, chunks, 0, LANES)[:, :, :1]
        ws_qs = _bmm(jnp.concatenate([w_dec, qd], axis=1), s)
        u = u_base - ws_qs[:, :cs]
        u_bd = jnp.stack([jnp.concatenate([jnp.concatenate([u[2 * i], zero_u], axis=-1),
                                           jnp.concatenate([zero_u, u[2 * i + 1]], axis=-1)], axis=0)
                          for i in range(DN_HEADS)], axis=0)
        aqk_sel = jnp.stack([a_qk[d * npd + chunks[d] * hp_n + hp] for d in range(2) for hp in range(hp_n)], axis=0)
        o_loc = _bmm(aqk_sel, u_bd)
        s = s * cd + _bmm_tn(kd, u)
        for d, o_ref in enumerate((of_ref, ob_ref)):
            c = chunks[d]
            for h in range(DN_HEADS):
                i = d * DN_HEADS + h
                o_ref[0, c * cs:(c + 1) * cs, h * LANES:(h + 1) * LANES] = (
                    ws_qs[i, cs:] + o_loc[i // 2][:, (h % 2) * LANES:(h % 2 + 1) * LANES]).astype(o_ref.dtype)
    s_scr[...] = s.reshape(2, DN_HEADS, DN_DK, DN_DV)


def _dnscan_kernel(*refs, nt, has_s0, want_state):
    it = iter(refs)
    fwd = [next(it) for _ in range(5)]
    bwd = [next(it) for _ in range(5)]
    par = [next(it) for _ in range(4)]
    s0_ref = next(it) if has_s0 else None
    of_ref = next(it)
    ob_ref = next(it)
    st_ref = next(it) if want_state else None
    s_scr = next(it)
    t = pl.program_id(1)

    @pl.when(t == 0)
    def _():
        if has_s0:
            s_scr[...] = s0_ref[0]
        else:
            s_scr[...] = jnp.zeros(s_scr.shape, F32)

    _dn_tile_pair(fwd, bwd, par, of_ref, ob_ref, s_scr)

    if want_state:
        @pl.when(t == nt - 1)
        def _():
            st_ref[0] = s_scr[...]


def _dnscan(q, k, v, ab, abt, a_log, dt_bias, s0, want_state):
    bsz, seq, _ = q.shape
    nt = seq // DN_TILE

    def tile_specs(tmap):
        qs = pl.BlockSpec((1, DN_TILE, DN_V_W), lambda b, t: (b, tmap(t), 0))
        return [qs, qs, qs,
                pl.BlockSpec((1, DN_TILE, _N_GATES), lambda b, t: (b, tmap(t), 0)),
                pl.BlockSpec((1, _N_GATES, DN_TILE), lambda b, t: (b, 0, tmap(t)))]

    zeros8 = jnp.zeros((2 * DN_HEADS,), F32)
    alog16 = jnp.concatenate([a_log.reshape(-1), zeros8])
    dtb16 = jnp.concatenate([dt_bias.reshape(-1), zeros8])
    prow = pl.BlockSpec((1, _N_GATES), lambda b, t: (0, 0))
    pcol = pl.BlockSpec((_N_GATES, 1), lambda b, t: (0, 0))
    in_specs = tile_specs(lambda t: t) + tile_specs(lambda t: nt - 1 - t) + [prow, prow, pcol, pcol]
    args = [q, k, v, ab, abt, q, k, v, ab, abt,
            alog16.reshape(1, _N_GATES), dtb16.reshape(1, _N_GATES),
            alog16.reshape(_N_GATES, 1), dtb16.reshape(_N_GATES, 1)]
    st_block = (1, 2, DN_HEADS, DN_DK, DN_DV)
    if s0 is not None:
        in_specs.append(pl.BlockSpec(st_block, lambda b, t: (b, 0, 0, 0, 0)))
        args.append(s0)
    o_shape = jax.ShapeDtypeStruct((bsz, seq, DN_V_W), BF16)
    out_specs = [pl.BlockSpec((1, DN_TILE, DN_V_W), lambda b, t: (b, t, 0)),
                 pl.BlockSpec((1, DN_TILE, DN_V_W), lambda b, t: (b, nt - 1 - t, 0))]
    out_shape = [o_shape, o_shape]
    if want_state:
        out_specs.append(pl.BlockSpec(st_block, lambda b, t: (b, 0, 0, 0, 0)))
        out_shape.append(jax.ShapeDtypeStruct((bsz, 2, DN_HEADS, DN_DK, DN_DV), F32))
    return pl.pallas_call(
        functools.partial(_dnscan_kernel, nt=nt, has_s0=s0 is not None, want_state=want_state),
        grid=(bsz, nt),
        in_specs=in_specs,
        out_specs=out_specs,
        out_shape=out_shape,
        scratch_shapes=[pltpu.VMEM((2, DN_HEADS, DN_DK, DN_DV), F32)],
        compiler_params=_cparams(2),
        name="deltanet_scan",
    )(*args)


def _outproj_kernel(x_ref, a_ref, b_ref, of_ref, ob_ref, z_ref, ng_ref, ga_ref, w_ref, o_ref):
    o = of_ref[0].astype(F32) + ob_ref[0].astype(F32)
    z = z_ref[0].astype(F32)
    acc = jnp.dot(a_ref[0].astype(BF16), w_ref[0:CONV_W, :], preferred_element_type=F32)
    acc = acc + jnp.dot(b_ref[0].astype(BF16), w_ref[CONV_W:CONV_W + NA_W, :], preferred_element_type=F32)
    for h in range(DN_HEADS):
        cols = slice(h * DN_DV, (h + 1) * DN_DV)
        oh = o[:, cols]
        y = oh * lax.rsqrt(jnp.mean(oh * oh, axis=-1, keepdims=True) + EPS) * ng_ref[...] * _silu(z[:, cols])
        r0 = CONV_W + NA_W + h * DN_DV
        acc = acc + jnp.dot(y.astype(BF16), w_ref[r0:r0 + DN_DV, :], preferred_element_type=F32)
    o_ref[0] = x_ref[0] + ga_ref[0] * acc


def _outproj(x, out_a, out_b, o_f, o_b, z, norm_g, gate, w_b, layer):
    shape0 = x.shape
    tm = PROJ_TILE
    x, out_a, out_b, o_f, o_b, z = (_merge_short_sequences(a, tm, gate.shape[0] == 1)
                                    for a in (x, out_a, out_b, o_f, o_b, z))
    bsz, seq, d = x.shape

    def row(w):
        return pl.BlockSpec((1, tm, w), lambda b, t: (b, t, 0))

    return pl.pallas_call(
        _outproj_kernel,
        grid=(bsz, seq // tm),
        in_specs=[row(d), row(CONV_W), row(NA_W), row(DN_V_W), row(DN_V_W), row(DN_V_W),
                  pl.BlockSpec((1, DN_DV), lambda b, t: (0, 0)),
                  pl.BlockSpec((1, 1, d), _bmap(gate.shape[0])),
                  pl.BlockSpec((None, MIX_W, d), lambda b, t: (layer, 0, 0))],
        out_specs=row(d),
        out_shape=jax.ShapeDtypeStruct((bsz, seq, d), F32),
        compiler_params=_cparams(2),
        name="outproj",
    )(x, out_a, out_b, o_f, o_b, z, norm_g.reshape(1, -1), gate, w_b).reshape(shape0)


def _first_argmax(x, lane, valid):
    xm = jnp.where(valid, x, -jnp.inf)
    m = jnp.max(xm, axis=-1, keepdims=True)
    idx = jnp.min(jnp.where(valid & (xm == m), lane, float(LANES)), axis=-1, keepdims=True)
    return m, idx


def _route(h, rw_hi_ref, rw_lo_ref, rb_ref):
    h_hi = h.astype(BF16)
    h_lo = (h - h_hi.astype(F32)).astype(BF16)
    logits = (jnp.dot(h_hi, rw_hi_ref[...], preferred_element_type=F32)
              + jnp.dot(h_lo, rw_hi_ref[...], preferred_element_type=F32)
              + jnp.dot(h_hi, rw_lo_ref[...], preferred_element_type=F32)) + rb_ref[...]
    lane = lax.broadcasted_iota(jnp.int32, logits.shape, 1).astype(F32)
    is_grp = lane < N_GROUPS
    gmax, g_sel = _first_argmax(logits, lane, is_grp)
    pg_sel = 1.0 / jnp.sum(jnp.where(is_grp, jnp.exp(logits - gmax), 0.0), axis=-1, keepdims=True)
    e_lane = lane - N_GROUPS
    in_grp = (e_lane >= g_sel * EXPERTS_PER_GROUP) & (e_lane < (g_sel + 1) * EXPERTS_PER_GROUP)
    m1, i1 = _first_argmax(logits, lane, in_grp)
    m2, i2 = _first_argmax(logits, lane, in_grp & (lane != i1))
    e2 = jnp.exp(m2 - m1)
    w1 = pg_sel / (1.0 + e2)
    w2 = pg_sel * e2 / (1.0 + e2)
    return g_sel, jnp.where(lane == i1, w1, 0.0) + jnp.where(lane == i2, w2, 0.0)


def _moe_kernel(x_ref, g_ref, sc_ref, sh_ref, ga_ref, rw_hi_ref, rw_lo_ref, rb_ref, tri_ref, w1_ref, w3_ref, w2_ref,
                gf_ref, o_ref, h_scr, ghi_scr, glo_scr, oh_scr, rank_scr, oht_scr, rankt_scr, acc_scr, cnt_scr,
                *, final_norm):
    grp = pl.program_id(1)
    tm = x_ref.shape[0]

    @pl.when(grp == 0)
    def _():
        h = _mod_norm(x_ref[...], g_ref[...], sc_ref[0], sh_ref[0])
        h_scr[...] = h.astype(BF16)
        g_sel, gates = _route(h, rw_hi_ref, rw_lo_ref, rb_ref)
        g_hi = gates.astype(BF16)
        ghi_scr[...] = g_hi
        glo_scr[...] = (gates - g_hi.astype(F32)).astype(BF16)
        acc_scr[...] = jnp.zeros(acc_scr.shape, F32)
        lane = lax.broadcasted_iota(jnp.int32, (tm, LANES), 1).astype(F32)
        one_hot = jnp.where(lane == g_sel, 1.0, 0.0)
        oh_b = one_hot.astype(BF16)
        oh_scr[...] = one_hot
        rank_scr[...] = jnp.dot(tri_ref[...], oh_b, preferred_element_type=F32)
        pick = jnp.where(lax.broadcasted_iota(jnp.int32, (SUBLANES, LANES), 0)
                         == lax.broadcasted_iota(jnp.int32, (SUBLANES, LANES), 1), 1.0, 0.0).astype(BF16)
        oh_t = lax.dot_general(pick, oh_b, _NT, preferred_element_type=F32)
        oht_scr[...] = oh_t
        rankt_scr[...] = lax.dot_general(oh_t.astype(BF16), tri_ref[...], _NT, preferred_element_type=F32)
        for k in range(N_GROUPS):
            cnt_scr[k] = jnp.sum(one_hot[:, k:k + 1]).astype(jnp.int32)

    hb = h_scr[...]
    lane = lax.broadcasted_iota(jnp.int32, (tm, LANES), 1)
    in_col = jnp.sum(jnp.where(lane == grp, oh_scr[...], 0.0), axis=-1, keepdims=True)
    rank_col = jnp.sum(jnp.where(lane == grp, rank_scr[...], 0.0), axis=-1, keepdims=True)
    in_row = oht_scr[pl.ds(grp, 1), :]
    rank_row = rankt_scr[pl.ds(grp, 1), :]
    slot_r = lax.broadcasted_iota(jnp.int32, (MOE_ROWS, tm), 0).astype(F32)
    slot_c = lax.broadcasted_iota(jnp.int32, (tm, MOE_ROWS), 1).astype(F32)
    glane = lax.broadcasted_iota(jnp.int32, (MOE_ROWS, LANES), 1)

    def chunk(c, carry):
        base = (c * MOE_ROWS).astype(F32)
        take = jnp.where((in_row > 0.5) & (rank_row - base == slot_r), 1.0, 0.0).astype(BF16)
        put = jnp.where((in_col > 0.5) & (rank_col - base == slot_c), 1.0, 0.0).astype(BF16)
        xs = jnp.dot(take, hb, preferred_element_type=F32).astype(BF16)
        gates = (jnp.dot(take, ghi_scr[...], preferred_element_type=F32)
                 + jnp.dot(take, glo_scr[...], preferred_element_type=F32))
        y = None
        for half in range(EXPERTS_PER_GROUP // 2):
            hid = []
            for j in (2 * half, 2 * half + 1):
                gate_e = jnp.sum(jnp.where(glane == N_GROUPS + grp * EXPERTS_PER_GROUP + j, gates, 0.0),
                                 axis=-1, keepdims=True)
                up = _silu(jnp.dot(xs, w1_ref[0, j], preferred_element_type=F32)) * jnp.dot(
                    xs, w3_ref[0, j], preferred_element_type=F32)
                hid.append((up * gate_e).astype(BF16))
            rows = slice(2 * half * D_EXPERT, 2 * (half + 1) * D_EXPERT)
            part = jnp.dot(jnp.concatenate(hid, axis=-1), w2_ref[0, 0, rows, :], preferred_element_type=F32)
            y = part if y is None else y + part
        acc_scr[...] += jnp.dot(put, y.astype(BF16), preferred_element_type=F32)
        return carry

    n_chunks = (cnt_scr[grp] + MOE_ROWS - 1) // MOE_ROWS
    lax.fori_loop(0, n_chunks, chunk, 0)

    @pl.when(grp == N_GROUPS - 1)
    def _():
        y = x_ref[...] + ga_ref[0] * acc_scr[...]
        if final_norm:
            y = y * lax.rsqrt(jnp.mean(y * y, axis=-1, keepdims=True) + EPS) * gf_ref[...]
        o_ref[...] = y


def _moe(x, g, scale, shift, gate, rw_hi, rw_lo, rb, w1_b, w3_b, w2_b, g_final, layer, final_norm):
    bsz, seq, d = x.shape
    x2 = x.reshape(bsz * seq, d)
    if scale.shape[0] == 1:
        tm = MOE_TILE
        mm = lambda i, e: (0, 0, 0)
    else:
        tm = min(MOE_TILE, seq)
        per_seq = seq // tm
        mm = lambda i, e: (i // per_seq, 0, 0)
    tri = jnp.asarray(np.tril(np.ones((tm, tm), np.float32), -1), BF16)
    vec = pl.BlockSpec((1, d), lambda i, e: (0, 0))
    mod = pl.BlockSpec((1, 1, d), mm)
    rspec = pl.BlockSpec((d, LANES), lambda i, e: (0, 0))
    gw = EXPERTS_PER_GROUP * D_EXPERT
    out = pl.pallas_call(
        functools.partial(_moe_kernel, final_norm=final_norm),
        grid=(bsz * seq // tm, N_GROUPS),
        in_specs=[pl.BlockSpec((tm, d), lambda i, e: (i, 0)), vec, mod, mod, mod,
                  rspec, rspec, pl.BlockSpec((1, LANES), lambda i, e: (0, 0)),
                  pl.BlockSpec((tm, tm), lambda i, e: (0, 0)),
                  pl.BlockSpec((None, 1, EXPERTS_PER_GROUP, d, D_EXPERT), lambda i, e: (layer, e, 0, 0, 0)),
                  pl.BlockSpec((None, 1, EXPERTS_PER_GROUP, d, D_EXPERT), lambda i, e: (layer, e, 0, 0, 0)),
                  pl.BlockSpec((1, 1, gw, d), lambda i, e: (layer, e, 0, 0)),
                  vec],
        out_specs=pl.BlockSpec((tm, d), lambda i, e: (i, 0)),
        out_shape=jax.ShapeDtypeStruct((bsz * seq, d), F32),
        scratch_shapes=[pltpu.VMEM((tm, d), BF16), pltpu.VMEM((tm, LANES), BF16), pltpu.VMEM((tm, LANES), BF16),
                        pltpu.VMEM((tm, LANES), F32), pltpu.VMEM((tm, LANES), F32),
                        pltpu.VMEM((SUBLANES, tm), F32), pltpu.VMEM((SUBLANES, tm), F32),
                        pltpu.VMEM((tm, d), F32), pltpu.SMEM((N_GROUPS,), jnp.int32)],
        compiler_params=_cparams(2),
        name="moe",
    )(x2, g, scale, shift, gate, rw_hi, rw_lo, rb, tri, w1_b, w3_b, w2_b, g_final.reshape(1, -1))
    return out.reshape(bsz, seq, d)


def _split_hi_lo(w):
    hi = w.astype(BF16)
    return hi, (w - hi.astype(F32)).astype(BF16)


def kernel(x_prompt, x_sample, cache_na_k, cache_na_v, state_delta, c, c_ctx, w_mod, b_mod, g_norm1, g_norm2,
           w_in, conv_a_w, conv_a_b, ln_a_g, ln_a_b, na_rel_bias, dn_conv_w, dn_a_log, dn_dt_bias, dn_norm_g,
           w_out, router_wg, router_bg, router_we, router_be, w1, w3, w2, g_final):
    n_dec = x_sample.shape[0]
    d = D_MODEL
    cond8 = jnp.concatenate([c_ctx[None, :], c, jnp.zeros((8 - 1 - n_dec, d), F32)], axis=0)
    mods = _adaln(cond8, w_mod, b_mod).reshape(DEPTH, 8, 6, d)

    w_in_b = w_in.astype(BF16)
    w_abt_b = jnp.swapaxes(w_in[:, :, OFF_DN_AB:], 1, 2).astype(BF16)
    w_out_b = w_out.astype(BF16)
    w1_b = w1.astype(BF16).reshape(DEPTH, N_GROUPS, EXPERTS_PER_GROUP, d, D_EXPERT)
    w3_b = w3.astype(BF16).reshape(DEPTH, N_GROUPS, EXPERTS_PER_GROUP, d, D_EXPERT)
    w2_b = w2.astype(BF16).reshape(DEPTH, N_GROUPS, EXPERTS_PER_GROUP * D_EXPERT, d)
    pad = LANES - N_GROUPS - N_EXPERTS
    rw = jnp.concatenate([router_wg, router_we, jnp.zeros((DEPTH, d, pad), F32)], axis=-1)
    rw_hi, rw_lo = _split_hi_lo(rw)
    rb = jnp.concatenate([router_bg, router_be, jnp.zeros((DEPTH, pad), F32)], axis=-1)

    past = cache_na_k.shape[2]
    cache_k = cache_na_k.reshape(n_dec, DEPTH, past, NA_W)
    cache_v = cache_na_v.reshape(n_dec, DEPTH, past, NA_W)
    rows = x_sample.shape[1] // GRID_W

    def layer(x, l, is_ctx):
        m = mods[l, 0:1] if is_ctx else mods[l, 1:1 + n_dec]
        sh1, sc1, ga1, sh2, sc2, ga2 = (m[:, i:i + 1, :] for i in range(6))
        glu, q_na, k_na, v_na, dqkv, dz, ab, abt = _inproj(x, g_norm1[l:l + 1], sc1, sh1, w_in_b, w_abt_b, l)
        out_a = _cconv(glu, conv_a_w[l], conv_a_b[l], ln_a_g[l], ln_a_b[l])
        if is_ctx:
            out_b = _ctx_attn(q_na, k_na, v_na)
        else:
            out_b = _na_attn(q_na, k_na, v_na, cache_k, cache_v, l, _na_bias_tables(na_rel_bias[l], rows))
        dq, dk, dv = _dnprep(dqkv, dn_conv_w[l], use_rope=not is_ctx)
        scan = _dnscan(dq, dk, dv, ab, abt, dn_a_log[l], dn_dt_bias[l],
                       None if is_ctx else state_delta[:, l], want_state=is_ctx)
        o_f, o_b = scan[0], scan[1]
        x = _outproj(x, out_a, out_b, o_f, o_b, dz, dn_norm_g[l], ga1, w_out_b, l)
        x = _moe(x, g_norm2[l:l + 1], sc2, sh2, ga2, rw_hi[l], rw_lo[l], rb[l:l + 1], w1_b, w3_b, w2_b,
                 g_final, layer=l, final_norm=(l == DEPTH - 1))
        return x, k_na, v_na, (scan[2] if is_ctx else None)

    xc = x_prompt
    new_k, new_v, new_s = [], [], []
    for l in range(DEPTH):
        xc, k_na, v_na, s_ctx = layer(xc, l, True)
        new_k.append(k_na.reshape(k_na.shape[0], k_na.shape[1], NA_HEADS, NA_HEAD_DIM))
        new_v.append(v_na.reshape(v_na.shape[0], v_na.shape[1], NA_HEADS, NA_HEAD_DIM))
        new_s.append(s_ctx)
    xs = x_sample
    for l in range(DEPTH):
        xs, _, _, _ = layer(xs, l, False)
    return (xc, xs, jnp.stack(new_k, axis=1), jnp.stack(new_v, axis=1), jnp.stack(new_s, axis=1))
```

```python
import functools
import math

import numpy as np
import jax
import jax.numpy as jnp
from jax import lax
from jax.experimental import pallas as pl
from jax.experimental.pallas import tpu as pltpu

F32 = jnp.float32
BF16 = jnp.bfloat16

D_MODEL = 1024
DEPTH = 2
GRID_W = 64
CONV_W = 256
CONV_K = 31
NA_HEADS = 4
NA_HEAD_DIM = 64
NA_W = NA_HEADS * NA_HEAD_DIM
NA_ROWS = 8
NA_COLS = 16
DN_HEADS = 4
DN_DK = 128
DN_DV = 128
DN_QK_W = DN_HEADS * DN_DK
DN_V_W = DN_HEADS * DN_DV
DN_CONV_K = 5
DN_CHUNK = 64
ROPE_BASE = 10000.0
MIX_W = CONV_W + NA_W + DN_V_W
OFF_NA = 2 * CONV_W
OFF_DN_QKV = OFF_NA + 3 * NA_W
OFF_DN_Z = OFF_DN_QKV + 2 * DN_QK_W + DN_V_W
OFF_DN_AB = OFF_DN_Z + DN_V_W
PROJ_W = OFF_DN_AB + 4 * DN_HEADS
N_GROUPS = 4
EXPERTS_PER_GROUP = 4
N_EXPERTS = N_GROUPS * EXPERTS_PER_GROUP
D_EXPERT = 256
EPS = 1e-6
NEG_INF = -1e30

LANES = 128
SUBLANES = 8
ROW_TILE = 256
PROJ_TILE = 512
DN_TILE = 256
NA_QROWS = 4
NA_WIN_ROWS = 12
MOE_TILE = 1024
MOE_ROWS = 256
VMEM_LIMIT = 48 * 1024 * 1024

_NT = (((1,), (1,)), ((), ()))


def _cparams(n_axes):
    return pltpu.CompilerParams(dimension_semantics=("arbitrary",) * n_axes,
                                vmem_limit_bytes=VMEM_LIMIT)


def _sigmoid(x):
    return 1.0 / (1.0 + jnp.exp(-x))


def _silu(x):
    return x * _sigmoid(x)


def _softplus(x):
    return jnp.maximum(x, 0.0) + jnp.log(1.0 + jnp.exp(-jnp.abs(x)))


def _dot(a, b):
    return jnp.dot(a.astype(BF16), b.astype(BF16), preferred_element_type=F32)


def _adaln_kernel(c_ref, w_ref, b_ref, o_ref):
    c = c_ref[...]
    o_ref[0] = _dot(_silu(c), w_ref[0]) + b_ref[0]


def _adaln(cond8, w_mod, b_mod):
    n_l, d, n6 = w_mod.shape
    tn = 1536
    return pl.pallas_call(
        _adaln_kernel,
        grid=(n_l, n6 // tn),
        in_specs=[pl.BlockSpec((8, d), lambda l, j: (0, 0)),
                  pl.BlockSpec((1, d, tn), lambda l, j: (l, 0, j)),
                  pl.BlockSpec((1, 1, tn), lambda l, j: (l, 0, j))],
        out_specs=pl.BlockSpec((1, 8, tn), lambda l, j: (l, 0, j)),
        out_shape=jax.ShapeDtypeStruct((n_l, 8, n6), F32),
        compiler_params=_cparams(2),
        name="adaln",
    )(cond8, w_mod, b_mod.reshape(n_l, 1, n6))


def _mod_norm(x, g, scale, shift):
    ms = jnp.mean(x * x, axis=-1, keepdims=True)
    return (x * lax.rsqrt(ms + EPS) * g) * (1.0 + scale) + shift


def _inproj_kernel(x_ref, g_ref, sc_ref, sh_ref, w_ref, wabt_ref,
                   glu_ref, q_ref, k_ref, v_ref, dqkv_ref, z_ref, ab_ref, abt_ref):
    hb = _mod_norm(x_ref[0], g_ref[...], sc_ref[0], sh_ref[0]).astype(BF16)

    def proj(a, b):
        return jnp.dot(hb, w_ref[:, a:b], preferred_element_type=F32)

    glu_ref[0] = proj(0, OFF_NA)
    q_ref[0] = proj(OFF_NA, OFF_NA + NA_W).astype(BF16)
    k_ref[0] = proj(OFF_NA + NA_W, OFF_NA + 2 * NA_W)
    v_ref[0] = proj(OFF_NA + 2 * NA_W, OFF_DN_QKV)
    dqkv_ref[0] = proj(OFF_DN_QKV, OFF_DN_Z).astype(BF16)
    z_ref[0] = proj(OFF_DN_Z, OFF_DN_AB).astype(BF16)
    ab_ref[0] = proj(OFF_DN_AB, PROJ_W)
    abt_ref[0] = lax.dot_general(wabt_ref[...], hb, _NT, preferred_element_type=F32)


def _bmap(bm):
    if bm == 1:
        return lambda b, t: (0, 0, 0)
    return lambda b, t: (b, 0, 0)


def _merge_short_sequences(x, tm, shared_mod):
    bsz, seq, d = x.shape
    if shared_mod and seq < tm:
        return x.reshape(bsz * seq // tm, tm, d)
    return x


def _inproj(x, g, scale, shift, w_b, wabt_b, layer):
    bsz0, seq0, _ = x.shape
    tm = PROJ_TILE
    x = _merge_short_sequences(x, tm, scale.shape[0] == 1)
    outs = _inproj_call(x, g, scale, shift, w_b, wabt_b, layer, tm)
    if x.shape[0] == bsz0:
        return outs
    per = tm // seq0
    row_outs = [o.reshape(bsz0, seq0, o.shape[-1]) for o in outs[:-1]]
    abt = outs[-1].reshape(x.shape[0], 4 * DN_HEADS, per, seq0)
    abt = jnp.swapaxes(abt, 1, 2).reshape(bsz0, 4 * DN_HEADS, seq0)
    return row_outs + [abt]


def _inproj_call(x, g, scale, shift, w_b, wabt_b, layer, tm):
    bsz, seq, d = x.shape
    widths = (OFF_NA, NA_W, NA_W, NA_W, OFF_DN_Z - OFF_DN_QKV, DN_V_W, 4 * DN_HEADS)
    dtypes = (F32, BF16, F32, F32, BF16, BF16, F32)
    out_shape = [jax.ShapeDtypeStruct((bsz, seq, w), dt) for w, dt in zip(widths, dtypes)]
    out_shape.append(jax.ShapeDtypeStruct((bsz, 4 * DN_HEADS, seq), F32))
    out_specs = [pl.BlockSpec((1, tm, w), lambda b, t: (b, t, 0)) for w in widths]
    out_specs.append(pl.BlockSpec((1, 4 * DN_HEADS, tm), lambda b, t: (b, 0, t)))
    mm = _bmap(scale.shape[0])
    return pl.pallas_call(
        _inproj_kernel,
        grid=(bsz, seq // tm),
        in_specs=[pl.BlockSpec((1, tm, d), lambda b, t: (b, t, 0)),
                  pl.BlockSpec((1, d), lambda b, t: (0, 0)),
                  pl.BlockSpec((1, 1, d), mm),
                  pl.BlockSpec((1, 1, d), mm),
                  pl.BlockSpec((None, d, PROJ_W), lambda b, t: (layer, 0, 0)),
                  pl.BlockSpec((None, 4 * DN_HEADS, d), lambda b, t: (layer, 0, 0))],
        out_specs=out_specs,
        out_shape=out_shape,
        compiler_params=_cparams(2),
        name="inproj",
    )(x, g, scale, shift, w_b, wabt_b)


_CC_HALO = 16
_CC_SUB = 64


def _cconv_kernel(main_ref, prev_ref, next_ref, w_ref, b_ref, g_ref, beta_ref, o_ref, scr, *, tt, nt):
    t = pl.program_id(1)

    def glu(a):
        return a[:, :CONV_W] * _sigmoid(a[:, CONV_W:])

    scr[0, 0:_CC_HALO, :] = jnp.where(t > 0, glu(prev_ref[0]), 0.0)
    scr[0, _CC_HALO:_CC_HALO + tt, :] = glu(main_ref[0])
    scr[0, _CC_HALO + tt:2 * _CC_HALO + tt, :] = jnp.where(t < nt - 1, glu(next_ref[0]), 0.0)
    n_shift = tt + 2 * _CC_HALO - SUBLANES
    for r in range(1, SUBLANES):
        scr[r, 0:n_shift, :] = scr[0, r:r + n_shift, :]
    pad = (CONV_K - 1) // 2
    for s in range(tt // _CC_SUB):
        base = s * _CC_SUB + _CC_HALO - pad
        acc = jnp.zeros((_CC_SUB, CONV_W), F32) + b_ref[...]
        for k in range(CONV_K):
            r = (base + k) % SUBLANES
            a0 = base + k - r
            acc = acc + w_ref[k:k + 1, :] * scr[r, a0:a0 + _CC_SUB, :]
        mu = jnp.mean(acc, axis=-1, keepdims=True)
        xc = acc - mu
        var = jnp.mean(xc * xc, axis=-1, keepdims=True)
        y = xc * lax.rsqrt(var + EPS) * g_ref[...] + beta_ref[...]
        o_ref[0, s * _CC_SUB:(s + 1) * _CC_SUB, :] = _silu(y).astype(BF16)


def _halo_specs(tt, seq, halo, width):
    per = tt // halo
    last = seq // halo - 1
    prev = pl.BlockSpec((1, halo, width), lambda b, t: (b, jnp.maximum(t * per - 1, 0), 0))
    nxt = pl.BlockSpec((1, halo, width), lambda b, t: (b, jnp.minimum((t + 1) * per, last), 0))
    return prev, nxt


def _cconv(glu, w, b, g, beta):
    bsz, seq, _ = glu.shape
    tt = ROW_TILE
    nt = seq // tt
    prev, nxt = _halo_specs(tt, seq, _CC_HALO, 2 * CONV_W)
    vec = pl.BlockSpec((1, CONV_W), lambda b_, t: (0, 0))
    return pl.pallas_call(
        functools.partial(_cconv_kernel, tt=tt, nt=nt),
        grid=(bsz, nt),
        in_specs=[pl.BlockSpec((1, tt, 2 * CONV_W), lambda b_, t: (b_, t, 0)), prev, nxt,
                  pl.BlockSpec((CONV_K, CONV_W), lambda b_, t: (0, 0)), vec, vec, vec],
        out_specs=pl.BlockSpec((1, tt, CONV_W), lambda b_, t: (b_, t, 0)),
        out_shape=jax.ShapeDtypeStruct((bsz, seq, CONV_W), BF16),
        scratch_shapes=[pltpu.VMEM((SUBLANES, tt + 2 * _CC_HALO, CONV_W), F32)],
        compiler_params=_cparams(2),
        name="conformer_conv",
    )(glu, glu, glu, w, b.reshape(1, -1), g.reshape(1, -1), beta.reshape(1, -1))


def _head_lane_id(shape):
    return lax.shift_right_logical(lax.broadcasted_iota(jnp.int32, shape, 1), 6)


def _ctx_attn_kernel(q_ref, k_ref, v_ref, o_ref):
    q = q_ref[0].astype(F32) * (NA_HEAD_DIM ** -0.5)
    k = k_ref[0].astype(BF16)
    v = v_ref[0].astype(BF16)
    head = _head_lane_id(q.shape)
    out = jnp.zeros(q.shape, F32)
    for h in range(NA_HEADS):
        s = lax.dot_general(jnp.where(head == h, q, 0.0).astype(BF16), k, _NT, preferred_element_type=F32)
        m = jnp.max(s, axis=-1, keepdims=True)
        p = jnp.exp(s - m)
        den = jnp.sum(p, axis=-1, keepdims=True)
        oh = jnp.dot(p.astype(BF16), v, preferred_element_type=F32)
        out = jnp.where(head == h, oh / den, out)
    o_ref[0] = out.astype(BF16)


def _ctx_attn(q, k, v):
    bsz, seq, w = q.shape
    spec = pl.BlockSpec((1, seq, w), lambda b: (b, 0, 0))
    return pl.pallas_call(
        _ctx_attn_kernel,
        grid=(bsz,),
        in_specs=[spec, spec, spec],
        out_specs=spec,
        out_shape=jax.ShapeDtypeStruct((bsz, seq, w), BF16),
        compiler_params=_cparams(1),
        name="context_attention",
    )(q, k, v)


def _na_window_start(blk, rows):
    return jnp.clip(blk * NA_QROWS - NA_ROWS // 2, 0, rows - NA_WIN_ROWS)


def _na_kernel(q_ref, k_ref, v_ref, kc_ref, vc_ref, eb_ref, o_ref, *, rows):
    blk = pl.program_id(1)
    ws = pl.multiple_of(_na_window_start(blk, rows) * GRID_W, GRID_W)
    nwin = NA_WIN_ROWS * GRID_W
    q = q_ref[0].astype(F32) * (NA_HEAD_DIM ** -0.5)
    kl = k_ref[0, pl.ds(ws, nwin), :].astype(BF16)
    vl = v_ref[0, pl.ds(ws, nwin), :].astype(BF16)
    kc = kc_ref[...].astype(BF16)
    vc = vc_ref[...].astype(BF16)
    head = _head_lane_id(q.shape)
    out = jnp.zeros(q.shape, F32)
    for h in range(NA_HEADS):
        qh = jnp.where(head == h, q, 0.0).astype(BF16)
        sl = lax.dot_general(qh, kl, _NT, preferred_element_type=F32) + eb_ref[0, h]
        sc = lax.dot_general(qh, kc, _NT, preferred_element_type=F32)
        m = jnp.maximum(jnp.max(sl, axis=-1, keepdims=True), jnp.max(sc, axis=-1, keepdims=True))
        p_l = jnp.exp(sl - m)
        p_c = jnp.exp(sc - m)
        den = jnp.sum(p_l, axis=-1, keepdims=True) + jnp.sum(p_c, axis=-1, keepdims=True)
        oh = (jnp.dot(p_l.astype(BF16), vl, preferred_element_type=F32)
              + jnp.dot(p_c.astype(BF16), vc, preferred_element_type=F32))
        out = jnp.where(head == h, oh / den, out)
    o_ref[0] = out.astype(BF16)


def _na_bias_tables(rel_bias, rows):
    n_heads = rel_bias.shape[0]
    c = np.arange(GRID_W)
    cs = np.clip(c - NA_COLS // 2, 0, GRID_W - NA_COLS)
    col_valid = (c[None, :] >= cs[:, None]) & (c[None, :] < cs[:, None] + NA_COLS)
    padw = GRID_W - NA_COLS
    padded = jnp.pad(rel_bias, ((0, 0), (0, 0), (padw, padw)))
    toeplitz = jnp.stack([padded[:, :, GRID_W - 1 - ci:2 * GRID_W - 1 - ci] for ci in range(GRID_W)], axis=2)
    toeplitz = jnp.where(jnp.asarray(col_valid)[None, None], toeplitz, NEG_INF)
    masked = jnp.full((n_heads, GRID_W, GRID_W), NEG_INF, F32)
    nblk = rows // NA_QROWS
    tables = []
    for blk in (0, 1, nblk - 1):
        ws = int(np.clip(blk * NA_QROWS - NA_ROWS // 2, 0, rows - NA_WIN_ROWS))
        q_rows = []
        for a in range(NA_QROWS):
            r = blk * NA_QROWS + a
            rs = int(np.clip(r - NA_ROWS // 2, 0, rows - NA_ROWS))
            blocks = []
            for j in range(NA_WIN_ROWS):
                kr = ws + j
                blocks.append(toeplitz[:, kr - r + NA_ROWS - 1] if rs <= kr < rs + NA_ROWS else masked)
            q_rows.append(jnp.concatenate(blocks, axis=-1))
        tables.append(jnp.concatenate(q_rows, axis=1))
    return jnp.stack(tables, axis=0)


def _na_attn(q, k, v, cache_k, cache_v, layer, eb):
    bsz, seq, w = q.shape
    rows = seq // GRID_W
    nblk = rows // NA_QROWS
    tq = NA_QROWS * GRID_W
    past = cache_k.shape[2]
    full = pl.BlockSpec((1, seq, w), lambda b, r: (b, 0, 0))
    ctx = pl.BlockSpec((None, None, past, w), lambda b, r: (b, layer, 0, 0))

    def eb_map(b, r):
        return (jnp.where(r == 0, 0, jnp.where(r == nblk - 1, 2, 1)), 0, 0, 0)

    return pl.pallas_call(
        functools.partial(_na_kernel, rows=rows),
        grid=(bsz, nblk),
        in_specs=[pl.BlockSpec((1, tq, w), lambda b, r: (b, r, 0)), full, full, ctx, ctx,
                  pl.BlockSpec((1, NA_HEADS, tq, NA_WIN_ROWS * GRID_W), eb_map)],
        out_specs=pl.BlockSpec((1, tq, w), lambda b, r: (b, r, 0)),
        out_shape=jax.ShapeDtypeStruct((bsz, seq, w), BF16),
        compiler_params=_cparams(2),
        name="neighbourhood_attention",
    )(q, k, v, cache_k, cache_v, eb)


_DN_HALO = 16
_DN_RSUB = 128
_DN_CGRP = 2 * LANES


def _dnprep_kernel(*refs, tt, nt, use_rope):
    if use_rope:
        main_ref, prev_ref, next_ref, w_ref, cos_ref, sin_ref, q_ref, k_ref, v_ref = refs
    else:
        main_ref, prev_ref, next_ref, w_ref, q_ref, k_ref, v_ref = refs
    t = pl.program_id(1)
    main = main_ref[0]
    halo = jnp.concatenate([jnp.where(t > 0, prev_ref[0], jnp.zeros_like(prev_ref[0])),
                            jnp.where(t < nt - 1, next_ref[0], jnp.zeros_like(next_ref[0]))], axis=0)
    pad = (DN_CONV_K - 1) // 2
    row = lax.broadcasted_iota(jnp.int32, (tt, tt), 0)
    col = lax.broadcasted_iota(jnp.int32, (tt, tt), 1)
    hrow = lax.broadcasted_iota(jnp.int32, (tt, 2 * _DN_HALO), 0)
    hcol = lax.broadcasted_iota(jnp.int32, (tt, 2 * _DN_HALO), 1)
    htok = jnp.where(hcol < _DN_HALO, hcol - _DN_HALO, tt + hcol - _DN_HALO)
    sel = {}
    for kk in range(DN_CONV_K):
        if kk != pad:
            sel[kk] = (jnp.where(col == row + (kk - pad), 1.0, 0.0).astype(BF16),
                       jnp.where(htok == hrow + (kk - pad), 1.0, 0.0).astype(BF16))
    outs = (q_ref, k_ref, v_ref)
    if use_rope:
        lane = lax.broadcasted_iota(jnp.int32, (_DN_RSUB, LANES), 1)
        first_half = (lane & (DN_DK // 4)) == 0
    for cg in range(3 * DN_QK_W // _DN_CGRP):
        gcols = slice(cg * _DN_CGRP, (cg + 1) * _DN_CGRP)
        taps = {}
        for kk in range(DN_CONV_K):
            if kk == pad:
                taps[kk] = main[:, gcols].astype(F32)
            else:
                taps[kk] = (jnp.dot(sel[kk][0], main[:, gcols], preferred_element_type=F32)
                            + jnp.dot(sel[kk][1], halo[:, gcols], preferred_element_type=F32))
        for sub in range(_DN_CGRP // LANES):
            cb = cg * (_DN_CGRP // LANES) + sub
            cols = slice(cb * LANES, (cb + 1) * LANES)
            lcols = slice(sub * LANES, (sub + 1) * LANES)
            for rs in range(tt // _DN_RSUB):
                rsl = slice(rs * _DN_RSUB, (rs + 1) * _DN_RSUB)
                acc = jnp.zeros((_DN_RSUB, LANES), F32)
                for kk in range(DN_CONV_K):
                    acc = acc + w_ref[kk:kk + 1, cols] * taps[kk][rsl, lcols]
                y = _silu(acc)
                if cb < 2 * DN_HEADS:
                    y = y * lax.rsqrt(jnp.sum(y * y, axis=-1, keepdims=True) + EPS)
                    if use_rope:
                        quarter = DN_DK // 4
                        partner = jnp.where(first_half, pltpu.roll(y, LANES - quarter, 1), pltpu.roll(y, quarter, 1))
                        y = y * cos_ref[rsl, :] + partner * sin_ref[rsl, :]
                    if cb < DN_HEADS:
                        y = y * (DN_DK ** -0.5)
                hh = cb % DN_HEADS
                outs[cb // DN_HEADS][0, rsl, hh * LANES:(hh + 1) * LANES] = y


def _rope_lane_tables(seq):
    pos = np.arange(seq)
    n_freq = DN_DK // 4
    inv = ROPE_BASE ** (-np.arange(n_freq) / n_freq)
    ang = np.stack([(pos // GRID_W)[:, None] * inv[None, :], (pos % GRID_W)[:, None] * inv[None, :]], axis=1)
    cos = np.cos(ang).astype(np.float32)
    sin = np.sin(ang).astype(np.float32)
    lane = np.arange(DN_DK)
    axis, half, freq = lane // (2 * n_freq), (lane // n_freq) % 2, lane % n_freq
    cos_l = cos[:, axis, freq]
    sin_l = sin[:, axis, freq] * np.where(half == 0, -1.0, 1.0).astype(np.float32)[None, :]
    return jnp.asarray(cos_l), jnp.asarray(sin_l)


def _dnprep(dqkv, conv_w, use_rope):
    bsz, seq, width = dqkv.shape
    tt = ROW_TILE
    nt = seq // tt
    prev, nxt = _halo_specs(tt, seq, _DN_HALO, width)
    in_specs = [pl.BlockSpec((1, tt, width), lambda b, t: (b, t, 0)), prev, nxt,
                pl.BlockSpec((DN_CONV_K, width), lambda b, t: (0, 0))]
    args = [dqkv, dqkv, dqkv, conv_w]
    if use_rope:
        tab = pl.BlockSpec((tt, DN_DK), lambda b, t: (t, 0))
        in_specs += [tab, tab]
        args += list(_rope_lane_tables(seq))
    out = jax.ShapeDtypeStruct((bsz, seq, DN_QK_W), F32)
    ospec = pl.BlockSpec((1, tt, DN_QK_W), lambda b, t: (b, t, 0))
    return pl.pallas_call(
        functools.partial(_dnprep_kernel, tt=tt, nt=nt, use_rope=use_rope),
        grid=(bsz, nt),
        in_specs=in_specs,
        out_specs=[ospec, ospec, ospec],
        out_shape=[out, out, out],
        compiler_params=_cparams(2),
        name="deltanet_prep",
    )(*args)


_CHUNKS = DN_TILE // DN_CHUNK
_LOG_CHUNK = int(math.log2(DN_CHUNK))
_N_GATES = 4 * DN_HEADS


def _bmm(a, b):
    return lax.dot_general(a.astype(BF16), b.astype(BF16), (((2,), (1,)), ((0,), (0,))),
                           preferred_element_type=F32)


def _bmm_nt(a, b):
    return lax.dot_general(a.astype(BF16), b.astype(BF16), (((2,), (2,)), ((0,), (0,))),
                           preferred_element_type=F32)


def _bmm_tn(a, b):
    return lax.dot_general(a.astype(BF16), b.astype(BF16), (((1,), (1,)), ((0,), (0,))),
                           preferred_element_type=F32)


def _pair_cols(col_a, col_b, width):
    lane = lax.broadcasted_iota(jnp.int32, (col_a.shape[0], width), 1)
    return jnp.where(lane < width // 2, col_a, col_b)


def _block_diag(y):
    lane = lax.broadcasted_iota(jnp.int32, y.shape, 2)
    first = lane < y.shape[2] // 2
    return jnp.concatenate([jnp.where(first, y, 0.0), jnp.where(first, 0.0, y)], axis=1)


def _chunk_cumsum(x, axis, reverse):
    n = x.shape[axis]
    pos = lax.broadcasted_iota(jnp.int32, x.shape, axis) & (DN_CHUNK - 1)
    step = 1
    while step < DN_CHUNK:
        if reverse:
            x = x + jnp.where(pos < DN_CHUNK - step, pltpu.roll(x, n - step, axis), 0.0)
        else:
            x = x + jnp.where(pos >= step, pltpu.roll(x, step, axis), 0.0)
        step *= 2
    return x


def _dn_tile_pair(fwd, bwd, par, of_ref, ob_ref, s_scr):
    alog_ref, dtb_ref, alogt_ref, dtbt_ref = par
    cs = DN_CHUNK
    pw = 2 * cs
    hp_n = DN_HEADS // 2
    npd = _CHUNKS * hp_n
    ii = lax.broadcasted_iota(jnp.int32, (cs, pw), 0)
    jj = lax.broadcasted_iota(jnp.int32, (cs, pw), 1) & (cs - 1)
    sx = jnp.stack([jnp.where(ii >= jj, ii ^ jj, -1), jnp.where(ii <= jj, ii ^ jj, -1)], axis=0)[:, None]
    incl = sx >= 0
    before = sx > 0

    gc_l, gct_l, beta_l, gc2_l, beta2_l, tot2_l, q_l, k_l, v_l = ([] for _ in range(9))
    for d, (q_ref, k_ref, v_ref, ab_ref, abt_ref) in enumerate((fwd, bwd)):
        ab = ab_ref[0]
        abt = abt_ref[0]
        g_col = -jnp.exp(alog_ref[...]) * _softplus(ab + dtb_ref[...])
        g_row = -jnp.exp(alogt_ref[...]) * _softplus(abt + dtbt_ref[...])
        beta_all = _sigmoid(ab)
        gc_rows = jnp.concatenate([_chunk_cumsum(g_row[:, i * LANES:(i + 1) * LANES], 1, d == 1)
                                   for i in range(DN_TILE // LANES)], axis=1)
        qx = q_ref[0]
        kx = k_ref[0]
        vx = v_ref[0]
        for c in range(_CHUNKS):
            rows = slice(c * cs, (c + 1) * cs)
            gc_c = _chunk_cumsum(g_col[rows], 0, d == 1)
            gct_c = gc_rows[:, rows]
            last = 0 if d == 1 else cs - 1
            tot_c = gc_c[last:last + 1, :]
            for hp in range(hp_n):
                ga = d * DN_HEADS + 2 * hp
                gb = ga + 1
                ba = ga + 2 * DN_HEADS
                bb = ba + 1
                gc_l.append(_pair_cols(gc_c[:, ga:ga + 1], gc_c[:, gb:gb + 1], pw))
                gct_l.append(jnp.concatenate([gct_c[ga:ga + 1, :], gct_c[gb:gb + 1, :]], axis=1))
                beta_l.append(_pair_cols(beta_all[rows, ba:ba + 1], beta_all[rows, bb:bb + 1], pw))
                gc2_l.append(_pair_cols(gc_c[:, ga:ga + 1], gc_c[:, gb:gb + 1], 2 * LANES))
                beta2_l.append(_pair_cols(beta_all[rows, ba:ba + 1], beta_all[rows, bb:bb + 1], 2 * LANES))
                tot2_l.append(_pair_cols(tot_c[:, ga:ga + 1], tot_c[:, gb:gb + 1], 2 * LANES))
                cols = slice(hp * 2 * LANES, (hp + 1) * 2 * LANES)
                q_l.append(qx[rows, cols])
                k_l.append(kx[rows, cols])
                v_l.append(vx[rows, cols])

    def stk(xs):
        return jnp.stack(xs, axis=0)

    gc, gc_t, beta = stk(gc_l), stk(gct_l), stk(beta_l)
    gc2, beta2, tot2 = stk(gc2_l), stk(beta2_l), stk(tot2_l)
    q, k, v = stk(q_l), stk(k_l), stk(v_l)

    def by_dir(x):
        return x.reshape((2, npd) + x.shape[1:])

    def flat(x):
        return x.reshape((2 * npd,) + x.shape[2:])

    k_bd = _block_diag(k)
    kk = by_dir(_bmm_nt(k, k_bd))
    qk = by_dir(_bmm_nt(q, k_bd))
    decay = jnp.where(incl, jnp.exp(jnp.where(incl, by_dir(gc - gc_t), 0.0)), 0.0)
    a_mat = jnp.where(before, (by_dir(beta) * kk) * decay, 0.0)
    a_qk = flat(qk * decay)

    e_mat = flat(-jnp.where(sx == 1, a_mat, 0.0))
    for lb in range(1, _LOG_CHUNK):
        l_b = flat(jnp.where(lax.shift_right_arithmetic(sx, lb) == 1, a_mat, 0.0))
        p_mat = l_b + _bmm(l_b, _block_diag(e_mat))
        e_mat = e_mat - p_mat - _bmm(e_mat, _block_diag(p_mat))

    eg2 = jnp.exp(gc2)
    vb = v * beta2
    kb = k * (beta2 * eg2)
    zero2 = jnp.zeros(vb.shape, F32)
    rhs = jnp.concatenate([vb[..., :LANES], kb[..., :LANES], vb[..., LANES:], kb[..., LANES:]], axis=-1)
    rhs_bd = jnp.concatenate([jnp.concatenate([vb[..., :LANES], kb[..., :LANES], zero2], axis=-1),
                              jnp.concatenate([zero2, vb[..., LANES:], kb[..., LANES:]], axis=-1)], axis=1)
    sol = rhs + _bmm(e_mat, rhs_bd)
    q_dec = q * eg2
    k_dec = k * jnp.exp(tot2 - gc2)
    c_dec2 = jnp.exp(tot2)

    def heads(x, chunks, off, stride):
        out = []
        for d in range(2):
            for hp in range(hp_n):
                p = d * npd + chunks[d] * hp_n + hp
                out += [x[p][:, off:off + LANES], x[p][:, off + stride:off + stride + LANES]]
        return jnp.stack(out, axis=0)

    s = s_scr[...].reshape(2 * DN_HEADS, DN_DK, DN_DV)
    zero_u = jnp.zeros((cs, LANES), F32)
    for step in range(_CHUNKS):
        chunks = (step, _CHUNKS - 1 - step)
        u_base = heads(sol, chunks, 0, 2 * LANES)
        w_dec = heads(sol, chunks, LANES, 2 * LANES)
        qd = heads(q_dec, chunks, 0, LANES)
        kd = heads(k_dec, chunks, 0, LANES)
        cd = heads(c_dec2, chunks, 0, LANES)[:, :, :1]
        ws_qs = _bmm(jnp.concatenate([w_dec, qd], axis=1), s)
        u = u_base - ws_qs[:, :cs]
        u_bd = jnp.stack([jnp.concatenate([jnp.concatenate([u[2 * i], zero_u], axis=-1),
                                           jnp.concatenate([zero_u, u[2 * i + 1]], axis=-1)], axis=0)
                          for i in range(DN_HEADS)], axis=0)
        aqk_sel = jnp.stack([a_qk[d * npd + chunks[d] * hp_n + hp] for d in range(2) for hp in range(hp_n)], axis=0)
        o_loc = _bmm(aqk_sel, u_bd)
        s = s * cd + _bmm_tn(kd, u)
        for d, o_ref in enumerate((of_ref, ob_ref)):
            c = chunks[d]
            for h in range(DN_HEADS):
                i = d * DN_HEADS + h
                o_ref[0, c * cs:(c + 1) * cs, h * LANES:(h + 1) * LANES] = (
                    ws_qs[i, cs:] + o_loc[i // 2][:, (h % 2) * LANES:(h % 2 + 1) * LANES]).astype(o_ref.dtype)
    s_scr[...] = s.reshape(2, DN_HEADS, DN_DK, DN_DV)


def _dnscan_kernel(*refs, nt, has_s0, want_state):
    it = iter(refs)
    fwd = [next(it) for _ in range(5)]
    bwd = [next(it) for _ in range(5)]
    par = [next(it) for _ in range(4)]
    s0_ref = next(it) if has_s0 else None
    of_ref = next(it)
    ob_ref = next(it)
    st_ref = next(it) if want_state else None
    s_scr = next(it)
    t = pl.program_id(1)

    @pl.when(t == 0)
    def _():
        if has_s0:
            s_scr[...] = s0_ref[0]
        else:
            s_scr[...] = jnp.zeros(s_scr.shape, F32)

    _dn_tile_pair(fwd, bwd, par, of_ref, ob_ref, s_scr)

    if want_state:
        @pl.when(t == nt - 1)
        def _():
            st_ref[0] = s_scr[...]


def _dnscan(q, k, v, ab, abt, a_log, dt_bias, s0, want_state):
    bsz, seq, _ = q.shape
    nt = seq // DN_TILE

    def tile_specs(tmap):
        qs = pl.BlockSpec((1, DN_TILE, DN_V_W), lambda b, t: (b, tmap(t), 0))
        return [qs, qs, qs,
                pl.BlockSpec((1, DN_TILE, _N_GATES), lambda b, t: (b, tmap(t), 0)),
                pl.BlockSpec((1, _N_GATES, DN_TILE), lambda b, t: (b, 0, tmap(t)))]

    zeros8 = jnp.zeros((2 * DN_HEADS,), F32)
    alog16 = jnp.concatenate([a_log.reshape(-1), zeros8])
    dtb16 = jnp.concatenate([dt_bias.reshape(-1), zeros8])
    prow = pl.BlockSpec((1, _N_GATES), lambda b, t: (0, 0))
    pcol = pl.BlockSpec((_N_GATES, 1), lambda b, t: (0, 0))
    in_specs = tile_specs(lambda t: t) + tile_specs(lambda t: nt - 1 - t) + [prow, prow, pcol, pcol]
    args = [q, k, v, ab, abt, q, k, v, ab, abt,
            alog16.reshape(1, _N_GATES), dtb16.reshape(1, _N_GATES),
            alog16.reshape(_N_GATES, 1), dtb16.reshape(_N_GATES, 1)]
    st_block = (1, 2, DN_HEADS, DN_DK, DN_DV)
    if s0 is not None:
        in_specs.append(pl.BlockSpec(st_block, lambda b, t: (b, 0, 0, 0, 0)))
        args.append(s0)
    o_shape = jax.ShapeDtypeStruct((bsz, seq, DN_V_W), BF16)
    out_specs = [pl.BlockSpec((1, DN_TILE, DN_V_W), lambda b, t: (b, t, 0)),
                 pl.BlockSpec((1, DN_TILE, DN_V_W), lambda b, t: (b, nt - 1 - t, 0))]
    out_shape = [o_shape, o_shape]
    if want_state:
        out_specs.append(pl.BlockSpec(st_block, lambda b, t: (b, 0, 0, 0, 0)))
        out_shape.append(jax.ShapeDtypeStruct((bsz, 2, DN_HEADS, DN_DK, DN_DV), F32))
    return pl.pallas_call(
        functools.partial(_dnscan_kernel, nt=nt, has_s0=s0 is not None, want_state=want_state),
        grid=(bsz, nt),
        in_specs=in_specs,
        out_specs=out_specs,
        out_shape=out_shape,
        scratch_shapes=[pltpu.VMEM((2, DN_HEADS, DN_DK, DN_DV), F32)],
        compiler_params=_cparams(2),
        name="deltanet_scan",
    )(*args)


def _outproj_kernel(x_ref, a_ref, b_ref, of_ref, ob_ref, z_ref, ng_ref, ga_ref, w_ref, o_ref):
    o = of_ref[0].astype(F32) + ob_ref[0].astype(F32)
    z = z_ref[0].astype(F32)
    acc = jnp.dot(a_ref[0].astype(BF16), w_ref[0:CONV_W, :], preferred_element_type=F32)
    acc = acc + jnp.dot(b_ref[0].astype(BF16), w_ref[CONV_W:CONV_W + NA_W, :], preferred_element_type=F32)
    for h in range(DN_HEADS):
        cols = slice(h * DN_DV, (h + 1) * DN_DV)
        oh = o[:, cols]
        y = oh * lax.rsqrt(jnp.mean(oh * oh, axis=-1, keepdims=True) + EPS) * ng_ref[...] * _silu(z[:, cols])
        r0 = CONV_W + NA_W + h * DN_DV
        acc = acc + jnp.dot(y.astype(BF16), w_ref[r0:r0 + DN_DV, :], preferred_element_type=F32)
    o_ref[0] = x_ref[0] + ga_ref[0] * acc


def _outproj(x, out_a, out_b, o_f, o_b, z, norm_g, gate, w_b, layer):
    shape0 = x.shape
    tm = PROJ_TILE
    x, out_a, out_b, o_f, o_b, z = (_merge_short_sequences(a, tm, gate.shape[0] == 1)
                                    for a in (x, out_a, out_b, o_f, o_b, z))
    bsz, seq, d = x.shape

    def row(w):
        return pl.BlockSpec((1, tm, w), lambda b, t: (b, t, 0))

    return pl.pallas_call(
        _outproj_kernel,
        grid=(bsz, seq // tm),
        in_specs=[row(d), row(CONV_W), row(NA_W), row(DN_V_W), row(DN_V_W), row(DN_V_W),
                  pl.BlockSpec((1, DN_DV), lambda b, t: (0, 0)),
                  pl.BlockSpec((1, 1, d), _bmap(gate.shape[0])),
                  pl.BlockSpec((None, MIX_W, d), lambda b, t: (layer, 0, 0))],
        out_specs=row(d),
        out_shape=jax.ShapeDtypeStruct((bsz, seq, d), F32),
        compiler_params=_cparams(2),
        name="outproj",
    )(x, out_a, out_b, o_f, o_b, z, norm_g.reshape(1, -1), gate, w_b).reshape(shape0)


def _first_argmax(x, lane, valid):
    xm = jnp.where(valid, x, -jnp.inf)
    m = jnp.max(xm, axis=-1, keepdims=True)
    idx = jnp.min(jnp.where(valid & (xm == m), lane, float(LANES)), axis=-1, keepdims=True)
    return m, idx


def _route(h, rw_hi_ref, rw_lo_ref, rb_ref):
    h_hi = h.astype(BF16)
    h_lo = (h - h_hi.astype(F32)).astype(BF16)
    logits = (jnp.dot(h_hi, rw_hi_ref[...], preferred_element_type=F32)
              + jnp.dot(h_lo, rw_hi_ref[...], preferred_element_type=F32)
              + jnp.dot(h_hi, rw_lo_ref[...], preferred_element_type=F32)) + rb_ref[...]
    lane = lax.broadcasted_iota(jnp.int32, logits.shape, 1).astype(F32)
    is_grp = lane < N_GROUPS
    gmax, g_sel = _first_argmax(logits, lane, is_grp)
    pg_sel = 1.0 / jnp.sum(jnp.where(is_grp, jnp.exp(logits - gmax), 0.0), axis=-1, keepdims=True)
    e_lane = lane - N_GROUPS
    in_grp = (e_lane >= g_sel * EXPERTS_PER_GROUP) & (e_lane < (g_sel + 1) * EXPERTS_PER_GROUP)
    m1, i1 = _first_argmax(logits, lane, in_grp)
    m2, i2 = _first_argmax(logits, lane, in_grp & (lane != i1))
    e2 = jnp.exp(m2 - m1)
    w1 = pg_sel / (1.0 + e2)
    w2 = pg_sel * e2 / (1.0 + e2)
    return g_sel, jnp.where(lane == i1, w1, 0.0) + jnp.where(lane == i2, w2, 0.0)


def _moe_kernel(x_ref, g_ref, sc_ref, sh_ref, ga_ref, rw_hi_ref, rw_lo_ref, rb_ref, tri_ref, w1_ref, w3_ref, w2_ref,
                gf_ref, o_ref, h_scr, ghi_scr, glo_scr, oh_scr, rank_scr, oht_scr, rankt_scr, acc_scr, cnt_scr,
                *, final_norm):
    grp = pl.program_id(1)
    tm = x_ref.shape[0]

    @pl.when(grp == 0)
    def _():
        h = _mod_norm(x_ref[...], g_ref[...], sc_ref[0], sh_ref[0])
        h_scr[...] = h.astype(BF16)
        g_sel, gates = _route(h, rw_hi_ref, rw_lo_ref, rb_ref)
        g_hi = gates.astype(BF16)
        ghi_scr[...] = g_hi
        glo_scr[...] = (gates - g_hi.astype(F32)).astype(BF16)
        acc_scr[...] = jnp.zeros(acc_scr.shape, F32)
        lane = lax.broadcasted_iota(jnp.int32, (tm, LANES), 1).astype(F32)
        one_hot = jnp.where(lane == g_sel, 1.0, 0.0)
        oh_b = one_hot.astype(BF16)
        oh_scr[...] = one_hot
        rank_scr[...] = jnp.dot(tri_ref[...], oh_b, preferred_element_type=F32)
        pick = jnp.where(lax.broadcasted_iota(jnp.int32, (SUBLANES, LANES), 0)
                         == lax.broadcasted_iota(jnp.int32, (SUBLANES, LANES), 1), 1.0, 0.0).astype(BF16)
        oh_t = lax.dot_general(pick, oh_b, _NT, preferred_element_type=F32)
        oht_scr[...] = oh_t
        rankt_scr[...] = lax.dot_general(oh_t.astype(BF16), tri_ref[...], _NT, preferred_element_type=F32)
        for k in range(N_GROUPS):
            cnt_scr[k] = jnp.sum(one_hot[:, k:k + 1]).astype(jnp.int32)

    hb = h_scr[...]
    lane = lax.broadcasted_iota(jnp.int32, (tm, LANES), 1)
    in_col = jnp.sum(jnp.where(lane == grp, oh_scr[...], 0.0), axis=-1, keepdims=True)
    rank_col = jnp.sum(jnp.where(lane == grp, rank_scr[...], 0.0), axis=-1, keepdims=True)
    in_row = oht_scr[pl.ds(grp, 1), :]
    rank_row = rankt_scr[pl.ds(grp, 1), :]
    def chunk(first_rank, n_rows):
        slot_r = lax.broadcasted_iota(jnp.int32, (n_rows, tm), 0).astype(F32)
        slot_c = lax.broadcasted_iota(jnp.int32, (tm, n_rows), 1).astype(F32)
        glane = lax.broadcasted_iota(jnp.int32, (n_rows, LANES), 1)
        base = first_rank.astype(F32)
        take = jnp.where((in_row > 0.5) & (rank_row - base == slot_r), 1.0, 0.0).astype(BF16)
        put = jnp.where((in_col > 0.5) & (rank_col - base == slot_c), 1.0, 0.0).astype(BF16)
        xs = jnp.dot(take, hb, preferred_element_type=F32).astype(BF16)
        gates = (jnp.dot(take, ghi_scr[...], preferred_element_type=F32)
                 + jnp.dot(take, glo_scr[...], preferred_element_type=F32))
        y = None
        for half in range(EXPERTS_PER_GROUP // 2):
            hid = []
            for j in (2 * half, 2 * half + 1):
                gate_e = jnp.sum(jnp.where(glane == N_GROUPS + grp * EXPERTS_PER_GROUP + j, gates, 0.0),
                                 axis=-1, keepdims=True)
                up = _silu(jnp.dot(xs, w1_ref[0, j], preferred_element_type=F32)) * jnp.dot(
                    xs, w3_ref[0, j], preferred_element_type=F32)
                hid.append((up * gate_e).astype(BF16))
            rows = slice(2 * half * D_EXPERT, 2 * (half + 1) * D_EXPERT)
            part = jnp.dot(jnp.concatenate(hid, axis=-1), w2_ref[0, 0, rows, :], preferred_element_type=F32)
            y = part if y is None else y + part
        acc_scr[...] += jnp.dot(put, y.astype(BF16), preferred_element_type=F32)

    count = cnt_scr[grp]
    rest = lax.rem(count, MOE_ROWS)
    n_full = count // MOE_ROWS + (rest > MOE_ROWS // 2).astype(jnp.int32)

    def full_chunk(c, carry):
        chunk(c * MOE_ROWS, MOE_ROWS)
        return carry

    lax.fori_loop(0, n_full, full_chunk, 0)

    @pl.when((rest > 0) & (rest <= MOE_ROWS // 2))
    def _():
        chunk(n_full * MOE_ROWS, MOE_ROWS // 2)

    @pl.when(grp == N_GROUPS - 1)
    def _():
        y = x_ref[...] + ga_ref[0] * acc_scr[...]
        if final_norm:
            y = y * lax.rsqrt(jnp.mean(y * y, axis=-1, keepdims=True) + EPS) * gf_ref[...]
        o_ref[...] = y


def _moe(x, g, scale, shift, gate, rw_hi, rw_lo, rb, w1_b, w3_b, w2_b, g_final, layer, final_norm):
    bsz, seq, d = x.shape
    x2 = x.reshape(bsz * seq, d)
    if scale.shape[0] == 1:
        tm = MOE_TILE
        mm = lambda i, e: (0, 0, 0)
    else:
        tm = min(MOE_TILE, seq)
        per_seq = seq // tm
        mm = lambda i, e: (i // per_seq, 0, 0)
    tri = jnp.asarray(np.tril(np.ones((tm, tm), np.float32), -1), BF16)
    vec = pl.BlockSpec((1, d), lambda i, e: (0, 0))
    mod = pl.BlockSpec((1, 1, d), mm)
    rspec = pl.BlockSpec((d, LANES), lambda i, e: (0, 0))
    gw = EXPERTS_PER_GROUP * D_EXPERT
    out = pl.pallas_call(
        functools.partial(_moe_kernel, final_norm=final_norm),
        grid=(bsz * seq // tm, N_GROUPS),
        in_specs=[pl.BlockSpec((tm, d), lambda i, e: (i, 0)), vec, mod, mod, mod,
                  rspec, rspec, pl.BlockSpec((1, LANES), lambda i, e: (0, 0)),
                  pl.BlockSpec((tm, tm), lambda i, e: (0, 0)),
                  pl.BlockSpec((None, 1, EXPERTS_PER_GROUP, d, D_EXPERT), lambda i, e: (layer, e, 0, 0, 0)),
                  pl.BlockSpec((None, 1, EXPERTS_PER_GROUP, d, D_EXPERT), lambda i, e: (layer, e, 0, 0, 0)),
                  pl.BlockSpec((1, 1, gw, d), lambda i, e: (layer, e, 0, 0)),
                  vec],
        out_specs=pl.BlockSpec((tm, d), lambda i, e: (i, 0)),
        out_shape=jax.ShapeDtypeStruct((bsz * seq, d), F32),
        scratch_shapes=[pltpu.VMEM((tm, d), BF16), pltpu.VMEM((tm, LANES), BF16), pltpu.VMEM((tm, LANES), BF16),
                        pltpu.VMEM((tm, LANES), F32), pltpu.VMEM((tm, LANES), F32),
                        pltpu.VMEM((SUBLANES, tm), F32), pltpu.VMEM((SUBLANES, tm), F32),
                        pltpu.VMEM((tm, d), F32), pltpu.SMEM((N_GROUPS,), jnp.int32)],
        compiler_params=_cparams(2),
        name="moe",
    )(x2, g, scale, shift, gate, rw_hi, rw_lo, rb, tri, w1_b, w3_b, w2_b, g_final.reshape(1, -1))
    return out.reshape(bsz, seq, d)


def _split_hi_lo(w):
    hi = w.astype(BF16)
    return hi, (w - hi.astype(F32)).astype(BF16)


def kernel(x_prompt, x_sample, cache_na_k, cache_na_v, state_delta, c, c_ctx, w_mod, b_mod, g_norm1, g_norm2,
           w_in, conv_a_w, conv_a_b, ln_a_g, ln_a_b, na_rel_bias, dn_conv_w, dn_a_log, dn_dt_bias, dn_norm_g,
           w_out, router_wg, router_bg, router_we, router_be, w1, w3, w2, g_final):
    n_dec = x_sample.shape[0]
    d = D_MODEL
    cond8 = jnp.concatenate([c_ctx[None, :], c, jnp.zeros((8 - 1 - n_dec, d), F32)], axis=0)
    mods = _adaln(cond8, w_mod, b_mod).reshape(DEPTH, 8, 6, d)

    w_in_b = w_in.astype(BF16)
    w_abt_b = jnp.swapaxes(w_in[:, :, OFF_DN_AB:], 1, 2).astype(BF16)
    w_out_b = w_out.astype(BF16)
    w1_b = w1.astype(BF16).reshape(DEPTH, N_GROUPS, EXPERTS_PER_GROUP, d, D_EXPERT)
    w3_b = w3.astype(BF16).reshape(DEPTH, N_GROUPS, EXPERTS_PER_GROUP, d, D_EXPERT)
    w2_b = w2.astype(BF16).reshape(DEPTH, N_GROUPS, EXPERTS_PER_GROUP * D_EXPERT, d)
    pad = LANES - N_GROUPS - N_EXPERTS
    rw = jnp.concatenate([router_wg, router_we, jnp.zeros((DEPTH, d, pad), F32)], axis=-1)
    rw_hi, rw_lo = _split_hi_lo(rw)
    rb = jnp.concatenate([router_bg, router_be, jnp.zeros((DEPTH, pad), F32)], axis=-1)

    past = cache_na_k.shape[2]
    cache_k = cache_na_k.reshape(n_dec, DEPTH, past, NA_W)
    cache_v = cache_na_v.reshape(n_dec, DEPTH, past, NA_W)
    rows = x_sample.shape[1] // GRID_W

    def layer(x, l, is_ctx):
        m = mods[l, 0:1] if is_ctx else mods[l, 1:1 + n_dec]
        sh1, sc1, ga1, sh2, sc2, ga2 = (m[:, i:i + 1, :] for i in range(6))
        glu, q_na, k_na, v_na, dqkv, dz, ab, abt = _inproj(x, g_norm1[l:l + 1], sc1, sh1, w_in_b, w_abt_b, l)
        out_a = _cconv(glu, conv_a_w[l], conv_a_b[l], ln_a_g[l], ln_a_b[l])
        if is_ctx:
            out_b = _ctx_attn(q_na, k_na, v_na)
        else:
            out_b = _na_attn(q_na, k_na, v_na, cache_k, cache_v, l, _na_bias_tables(na_rel_bias[l], rows))
        dq, dk, dv = _dnprep(dqkv, dn_conv_w[l], use_rope=not is_ctx)
        scan = _dnscan(dq, dk, dv, ab, abt, dn_a_log[l], dn_dt_bias[l],
                       None if is_ctx else state_delta[:, l], want_state=is_ctx)
        o_f, o_b = scan[0], scan[1]
        x = _outproj(x, out_a, out_b, o_f, o_b, dz, dn_norm_g[l], ga1, w_out_b, l)
        x = _moe(x, g_norm2[l:l + 1], sc2, sh2, ga2, rw_hi[l], rw_lo[l], rb[l:l + 1], w1_b, w3_b, w2_b,
                 g_final, layer=l, final_norm=(l == DEPTH - 1))
        return x, k_na, v_na, (scan[2] if is_ctx else None)

    xc = x_prompt
    new_k, new_v, new_s = [], [], []
    for l in range(DEPTH):
        xc, k_na, v_na, s_ctx = layer(xc, l, True)
        new_k.append(k_na.reshape(k_na.shape[0], k_na.shape[1], NA_HEADS, NA_HEAD_DIM))
        new_v.append(v_na.reshape(v_na.shape[0], v_na.shape[1], NA_HEADS, NA_HEAD_DIM))
        new_s.append(s_ctx)
    xs = x_sample
    for l in range(DEPTH):
        xs, _, _, _ = layer(xs, l, False)
    return (xc, xs, jnp.stack(new_k, axis=1), jnp.stack(new_v, axis=1), jnp.stack(new_s, axis=1))
```

```python
import functools
import math

import numpy as np
import jax
import jax.numpy as jnp
from jax import lax
from jax.experimental import pallas as pl
from jax.experimental.pallas import tpu as pltpu

F32 = jnp.float32
BF16 = jnp.bfloat16

D_MODEL = 1024
DEPTH = 2
GRID_W = 64
CONV_W = 256
CONV_K = 31
NA_HEADS = 4
NA_HEAD_DIM = 64
NA_W = NA_HEADS * NA_HEAD_DIM
NA_ROWS = 8
NA_COLS = 16
DN_HEADS = 4
DN_DK = 128
DN_DV = 128
DN_QK_W = DN_HEADS * DN_DK
DN_V_W = DN_HEADS * DN_DV
DN_CONV_K = 5
DN_CHUNK = 64
ROPE_BASE = 10000.0
MIX_W = CONV_W + NA_W + DN_V_W
OFF_NA = 2 * CONV_W
OFF_DN_QKV = OFF_NA + 3 * NA_W
OFF_DN_Z = OFF_DN_QKV + 2 * DN_QK_W + DN_V_W
OFF_DN_AB = OFF_DN_Z + DN_V_W
PROJ_W = OFF_DN_AB + 4 * DN_HEADS
N_GROUPS = 4
EXPERTS_PER_GROUP = 4
N_EXPERTS = N_GROUPS * EXPERTS_PER_GROUP
D_EXPERT = 256
EPS = 1e-6
NEG_INF = -1e30

LANES = 128
SUBLANES = 8
ROW_TILE = 256
CONV_TILE = 512
PROJ_TILE = 512
DN_TILE = 512
NA_QROWS = 4
NA_WIN_ROWS = 12
MOE_TILE = 1024
MOE_ROWS = 256
VMEM_LIMIT = 48 * 1024 * 1024

_NT = (((1,), (1,)), ((), ()))


def _cparams(n_axes):
    return pltpu.CompilerParams(dimension_semantics=("arbitrary",) * n_axes,
                                vmem_limit_bytes=VMEM_LIMIT)


def _sigmoid(x):
    return 1.0 / (1.0 + jnp.exp(-x))


def _silu(x):
    return x * _sigmoid(x)


def _softplus(x):
    return jnp.maximum(x, 0.0) + jnp.log(1.0 + jnp.exp(-jnp.abs(x)))


def _dot(a, b):
    return jnp.dot(a.astype(BF16), b.astype(BF16), preferred_element_type=F32)


def _adaln_kernel(c_ref, w_ref, b_ref, o_ref):
    c = c_ref[...]
    o_ref[0] = _dot(_silu(c), w_ref[0]) + b_ref[0]


def _adaln(cond8, w_mod, b_mod):
    n_l, d, n6 = w_mod.shape
    tn = 1536
    return pl.pallas_call(
        _adaln_kernel,
        grid=(n_l, n6 // tn),
        in_specs=[pl.BlockSpec((8, d), lambda l, j: (0, 0)),
                  pl.BlockSpec((1, d, tn), lambda l, j: (l, 0, j)),
                  pl.BlockSpec((1, 1, tn), lambda l, j: (l, 0, j))],
        out_specs=pl.BlockSpec((1, 8, tn), lambda l, j: (l, 0, j)),
        out_shape=jax.ShapeDtypeStruct((n_l, 8, n6), F32),
        compiler_params=_cparams(2),
        name="adaln",
    )(cond8, w_mod, b_mod.reshape(n_l, 1, n6))


def _mod_norm(x, g, scale, shift):
    ms = jnp.mean(x * x, axis=-1, keepdims=True)
    return (x * lax.rsqrt(ms + EPS) * g) * (1.0 + scale) + shift


def _inproj_kernel(x_ref, g_ref, sc_ref, sh_ref, w_ref, wabt_ref,
                   glu_ref, q_ref, k_ref, v_ref, dqkv_ref, z_ref, ab_ref, abt_ref):
    hb = _mod_norm(x_ref[0], g_ref[...], sc_ref[0], sh_ref[0]).astype(BF16)

    def proj(a, b):
        return jnp.dot(hb, w_ref[:, a:b], preferred_element_type=F32)

    glu_ref[0] = proj(0, OFF_NA)
    q_ref[0] = proj(OFF_NA, OFF_NA + NA_W).astype(BF16)
    k_ref[0] = proj(OFF_NA + NA_W, OFF_NA + 2 * NA_W)
    v_ref[0] = proj(OFF_NA + 2 * NA_W, OFF_DN_QKV)
    dqkv_ref[0] = proj(OFF_DN_QKV, OFF_DN_Z).astype(BF16)
    z_ref[0] = proj(OFF_DN_Z, OFF_DN_AB).astype(BF16)
    ab_ref[0] = proj(OFF_DN_AB, PROJ_W)
    abt_ref[0] = lax.dot_general(wabt_ref[...], hb, _NT, preferred_element_type=F32)


def _bmap(bm):
    if bm == 1:
        return lambda b, t: (0, 0, 0)
    return lambda b, t: (b, 0, 0)


def _merge_short_sequences(x, tm, shared_mod):
    bsz, seq, d = x.shape
    if shared_mod and seq < tm:
        return x.reshape(bsz * seq // tm, tm, d)
    return x


def _inproj(x, g, scale, shift, w_b, wabt_b, layer):
    bsz0, seq0, _ = x.shape
    tm = PROJ_TILE
    x = _merge_short_sequences(x, tm, scale.shape[0] == 1)
    outs = _inproj_call(x, g, scale, shift, w_b, wabt_b, layer, tm)
    if x.shape[0] == bsz0:
        return outs
    per = tm // seq0
    row_outs = [o.reshape(bsz0, seq0, o.shape[-1]) for o in outs[:-1]]
    abt = outs[-1].reshape(x.shape[0], 4 * DN_HEADS, per, seq0)
    abt = jnp.swapaxes(abt, 1, 2).reshape(bsz0, 4 * DN_HEADS, seq0)
    return row_outs + [abt]


def _inproj_call(x, g, scale, shift, w_b, wabt_b, layer, tm):
    bsz, seq, d = x.shape
    widths = (OFF_NA, NA_W, NA_W, NA_W, OFF_DN_Z - OFF_DN_QKV, DN_V_W, 4 * DN_HEADS)
    dtypes = (F32, BF16, F32, F32, BF16, BF16, F32)
    out_shape = [jax.ShapeDtypeStruct((bsz, seq, w), dt) for w, dt in zip(widths, dtypes)]
    out_shape.append(jax.ShapeDtypeStruct((bsz, 4 * DN_HEADS, seq), F32))
    out_specs = [pl.BlockSpec((1, tm, w), lambda b, t: (b, t, 0)) for w in widths]
    out_specs.append(pl.BlockSpec((1, 4 * DN_HEADS, tm), lambda b, t: (b, 0, t)))
    mm = _bmap(scale.shape[0])
    return pl.pallas_call(
        _inproj_kernel,
        grid=(bsz, seq // tm),
        in_specs=[pl.BlockSpec((1, tm, d), lambda b, t: (b, t, 0)),
                  pl.BlockSpec((1, d), lambda b, t: (0, 0)),
                  pl.BlockSpec((1, 1, d), mm),
                  pl.BlockSpec((1, 1, d), mm),
                  pl.BlockSpec((None, d, PROJ_W), lambda b, t: (layer, 0, 0)),
                  pl.BlockSpec((None, 4 * DN_HEADS, d), lambda b, t: (layer, 0, 0))],
        out_specs=out_specs,
        out_shape=out_shape,
        compiler_params=_cparams(2),
        name="inproj",
    )(x, g, scale, shift, w_b, wabt_b)


_CC_HALO = 16
_CC_SUB = 64


def _cconv_kernel(main_ref, prev_ref, next_ref, w_ref, b_ref, g_ref, beta_ref, o_ref, scr, *, tt, nt):
    t = pl.program_id(1)

    def glu(a):
        return a[:, :CONV_W] * _sigmoid(a[:, CONV_W:])

    scr[0, 0:_CC_HALO, :] = jnp.where(t > 0, glu(prev_ref[0]), 0.0)
    scr[0, _CC_HALO:_CC_HALO + tt, :] = glu(main_ref[0])
    scr[0, _CC_HALO + tt:2 * _CC_HALO + tt, :] = jnp.where(t < nt - 1, glu(next_ref[0]), 0.0)
    n_shift = tt + 2 * _CC_HALO - SUBLANES
    for r in range(1, SUBLANES):
        scr[r, 0:n_shift, :] = scr[0, r:r + n_shift, :]
    pad = (CONV_K - 1) // 2
    for s in range(tt // _CC_SUB):
        base = s * _CC_SUB + _CC_HALO - pad
        acc = jnp.zeros((_CC_SUB, CONV_W), F32) + b_ref[...]
        for k in range(CONV_K):
            r = (base + k) % SUBLANES
            a0 = base + k - r
            acc = acc + w_ref[k:k + 1, :] * scr[r, a0:a0 + _CC_SUB, :]
        mu = jnp.mean(acc, axis=-1, keepdims=True)
        xc = acc - mu
        var = jnp.mean(xc * xc, axis=-1, keepdims=True)
        y = xc * lax.rsqrt(var + EPS) * g_ref[...] + beta_ref[...]
        o_ref[0, s * _CC_SUB:(s + 1) * _CC_SUB, :] = _silu(y).astype(BF16)


def _halo_specs(tt, seq, halo, width):
    per = tt // halo
    last = seq // halo - 1
    prev = pl.BlockSpec((1, halo, width), lambda b, t: (b, jnp.maximum(t * per - 1, 0), 0))
    nxt = pl.BlockSpec((1, halo, width), lambda b, t: (b, jnp.minimum((t + 1) * per, last), 0))
    return prev, nxt


def _cconv(glu, w, b, g, beta):
    bsz, seq, _ = glu.shape
    tt = min(CONV_TILE, seq)
    nt = seq // tt
    prev, nxt = _halo_specs(tt, seq, _CC_HALO, 2 * CONV_W)
    vec = pl.BlockSpec((1, CONV_W), lambda b_, t: (0, 0))
    return pl.pallas_call(
        functools.partial(_cconv_kernel, tt=tt, nt=nt),
        grid=(bsz, nt),
        in_specs=[pl.BlockSpec((1, tt, 2 * CONV_W), lambda b_, t: (b_, t, 0)), prev, nxt,
                  pl.BlockSpec((CONV_K, CONV_W), lambda b_, t: (0, 0)), vec, vec, vec],
        out_specs=pl.BlockSpec((1, tt, CONV_W), lambda b_, t: (b_, t, 0)),
        out_shape=jax.ShapeDtypeStruct((bsz, seq, CONV_W), BF16),
        scratch_shapes=[pltpu.VMEM((SUBLANES, tt + 2 * _CC_HALO, CONV_W), F32)],
        compiler_params=_cparams(2),
        name="conformer_conv",
    )(glu, glu, glu, w, b.reshape(1, -1), g.reshape(1, -1), beta.reshape(1, -1))


def _head_lane_id(shape):
    return lax.shift_right_logical(lax.broadcasted_iota(jnp.int32, shape, 1), 6)


def _ctx_attn_kernel(q_ref, k_ref, v_ref, o_ref):
    q = q_ref[0].astype(F32) * (NA_HEAD_DIM ** -0.5)
    k = k_ref[0].astype(BF16)
    v = v_ref[0].astype(BF16)
    head = _head_lane_id(q.shape)
    out = jnp.zeros(q.shape, F32)
    for h in range(NA_HEADS):
        s = lax.dot_general(jnp.where(head == h, q, 0.0).astype(BF16), k, _NT, preferred_element_type=F32)
        m = jnp.max(s, axis=-1, keepdims=True)
        p = jnp.exp(s - m)
        den = jnp.sum(p, axis=-1, keepdims=True)
        oh = jnp.dot(p.astype(BF16), v, preferred_element_type=F32)
        out = jnp.where(head == h, oh / den, out)
    o_ref[0] = out.astype(BF16)


def _ctx_attn(q, k, v):
    bsz, seq, w = q.shape
    spec = pl.BlockSpec((1, seq, w), lambda b: (b, 0, 0))
    return pl.pallas_call(
        _ctx_attn_kernel,
        grid=(bsz,),
        in_specs=[spec, spec, spec],
        out_specs=spec,
        out_shape=jax.ShapeDtypeStruct((bsz, seq, w), BF16),
        compiler_params=_cparams(1),
        name="context_attention",
    )(q, k, v)


def _na_window_start(blk, rows):
    return jnp.clip(blk * NA_QROWS - NA_ROWS // 2, 0, rows - NA_WIN_ROWS)


def _na_kernel(q_ref, k_ref, v_ref, kc_ref, vc_ref, eb_ref, o_ref, *, rows):
    blk = pl.program_id(1)
    ws = pl.multiple_of(_na_window_start(blk, rows) * GRID_W, GRID_W)
    nwin = NA_WIN_ROWS * GRID_W
    q = q_ref[0].astype(F32) * (NA_HEAD_DIM ** -0.5)
    kl = k_ref[0, pl.ds(ws, nwin), :].astype(BF16)
    vl = v_ref[0, pl.ds(ws, nwin), :].astype(BF16)
    kc = kc_ref[...].astype(BF16)
    vc = vc_ref[...].astype(BF16)
    head = _head_lane_id(q.shape)
    out = jnp.zeros(q.shape, F32)
    for h in range(NA_HEADS):
        qh = jnp.where(head == h, q, 0.0).astype(BF16)
        sl = lax.dot_general(qh, kl, _NT, preferred_element_type=F32) + eb_ref[0, h]
        sc = lax.dot_general(qh, kc, _NT, preferred_element_type=F32)
        m = jnp.maximum(jnp.max(sl, axis=-1, keepdims=True), jnp.max(sc, axis=-1, keepdims=True))
        p_l = jnp.exp(sl - m)
        p_c = jnp.exp(sc - m)
        den = jnp.sum(p_l, axis=-1, keepdims=True) + jnp.sum(p_c, axis=-1, keepdims=True)
        oh = (jnp.dot(p_l.astype(BF16), vl, preferred_element_type=F32)
              + jnp.dot(p_c.astype(BF16), vc, preferred_element_type=F32))
        out = jnp.where(head == h, oh / den, out)
    o_ref[0] = out.astype(BF16)


def _na_bias_tables(rel_bias, rows):
    n_heads = rel_bias.shape[0]
    c = np.arange(GRID_W)
    cs = np.clip(c - NA_COLS // 2, 0, GRID_W - NA_COLS)
    col_valid = (c[None, :] >= cs[:, None]) & (c[None, :] < cs[:, None] + NA_COLS)
    padw = GRID_W - NA_COLS
    padded = jnp.pad(rel_bias, ((0, 0), (0, 0), (padw, padw)))
    toeplitz = jnp.stack([padded[:, :, GRID_W - 1 - ci:2 * GRID_W - 1 - ci] for ci in range(GRID_W)], axis=2)
    toeplitz = jnp.where(jnp.asarray(col_valid)[None, None], toeplitz, NEG_INF)
    masked = jnp.full((n_heads, GRID_W, GRID_W), NEG_INF, F32)
    nblk = rows // NA_QROWS
    tables = []
    for blk in (0, 1, nblk - 1):
        ws = int(np.clip(blk * NA_QROWS - NA_ROWS // 2, 0, rows - NA_WIN_ROWS))
        q_rows = []
        for a in range(NA_QROWS):
            r = blk * NA_QROWS + a
            rs = int(np.clip(r - NA_ROWS // 2, 0, rows - NA_ROWS))
            blocks = []
            for j in range(NA_WIN_ROWS):
                kr = ws + j
                blocks.append(toeplitz[:, kr - r + NA_ROWS - 1] if rs <= kr < rs + NA_ROWS else masked)
            q_rows.append(jnp.concatenate(blocks, axis=-1))
        tables.append(jnp.concatenate(q_rows, axis=1))
    return jnp.stack(tables, axis=0)


def _na_attn(q, k, v, cache_k, cache_v, layer, eb):
    bsz, seq, w = q.shape
    rows = seq // GRID_W
    nblk = rows // NA_QROWS
    tq = NA_QROWS * GRID_W
    past = cache_k.shape[2]
    full = pl.BlockSpec((1, seq, w), lambda b, r: (b, 0, 0))
    ctx = pl.BlockSpec((None, None, past, w), lambda b, r: (b, layer, 0, 0))

    def eb_map(b, r):
        return (jnp.where(r == 0, 0, jnp.where(r == nblk - 1, 2, 1)), 0, 0, 0)

    return pl.pallas_call(
        functools.partial(_na_kernel, rows=rows),
        grid=(bsz, nblk),
        in_specs=[pl.BlockSpec((1, tq, w), lambda b, r: (b, r, 0)), full, full, ctx, ctx,
                  pl.BlockSpec((1, NA_HEADS, tq, NA_WIN_ROWS * GRID_W), eb_map)],
        out_specs=pl.BlockSpec((1, tq, w), lambda b, r: (b, r, 0)),
        out_shape=jax.ShapeDtypeStruct((bsz, seq, w), BF16),
        compiler_params=_cparams(2),
        name="neighbourhood_attention",
    )(q, k, v, cache_k, cache_v, eb)


_DN_HALO = 16
_DN_RSUB = 128
_DN_CGRP = 2 * LANES


def _dnprep_kernel(*refs, tt, nt, use_rope):
    if use_rope:
        main_ref, prev_ref, next_ref, w_ref, cos_ref, sin_ref, q_ref, k_ref, v_ref = refs
    else:
        main_ref, prev_ref, next_ref, w_ref, q_ref, k_ref, v_ref = refs
    t = pl.program_id(1)
    main = main_ref[0]
    halo = jnp.concatenate([jnp.where(t > 0, prev_ref[0], jnp.zeros_like(prev_ref[0])),
                            jnp.where(t < nt - 1, next_ref[0], jnp.zeros_like(next_ref[0]))], axis=0)
    pad = (DN_CONV_K - 1) // 2
    row = lax.broadcasted_iota(jnp.int32, (tt, tt), 0)
    col = lax.broadcasted_iota(jnp.int32, (tt, tt), 1)
    hrow = lax.broadcasted_iota(jnp.int32, (tt, 2 * _DN_HALO), 0)
    hcol = lax.broadcasted_iota(jnp.int32, (tt, 2 * _DN_HALO), 1)
    htok = jnp.where(hcol < _DN_HALO, hcol - _DN_HALO, tt + hcol - _DN_HALO)
    sel = {}
    for kk in range(DN_CONV_K):
        if kk != pad:
            sel[kk] = (jnp.where(col == row + (kk - pad), 1.0, 0.0).astype(BF16),
                       jnp.where(htok == hrow + (kk - pad), 1.0, 0.0).astype(BF16))
    outs = (q_ref, k_ref, v_ref)
    if use_rope:
        lane = lax.broadcasted_iota(jnp.int32, (_DN_RSUB, LANES), 1)
        first_half = (lane & (DN_DK // 4)) == 0
    for cg in range(3 * DN_QK_W // _DN_CGRP):
        gcols = slice(cg * _DN_CGRP, (cg + 1) * _DN_CGRP)
        taps = {}
        for kk in range(DN_CONV_K):
            if kk == pad:
                taps[kk] = main[:, gcols].astype(F32)
            else:
                taps[kk] = (jnp.dot(sel[kk][0], main[:, gcols], preferred_element_type=F32)
                            + jnp.dot(sel[kk][1], halo[:, gcols], preferred_element_type=F32))
        for sub in range(_DN_CGRP // LANES):
            cb = cg * (_DN_CGRP // LANES) + sub
            cols = slice(cb * LANES, (cb + 1) * LANES)
            lcols = slice(sub * LANES, (sub + 1) * LANES)
            for rs in range(tt // _DN_RSUB):
                rsl = slice(rs * _DN_RSUB, (rs + 1) * _DN_RSUB)
                acc = jnp.zeros((_DN_RSUB, LANES), F32)
                for kk in range(DN_CONV_K):
                    acc = acc + w_ref[kk:kk + 1, cols] * taps[kk][rsl, lcols]
                y = _silu(acc)
                if cb < 2 * DN_HEADS:
                    y = y * lax.rsqrt(jnp.sum(y * y, axis=-1, keepdims=True) + EPS)
                    if use_rope:
                        quarter = DN_DK // 4
                        partner = jnp.where(first_half, pltpu.roll(y, LANES - quarter, 1), pltpu.roll(y, quarter, 1))
                        y = y * cos_ref[rsl, :] + partner * sin_ref[rsl, :]
                    if cb < DN_HEADS:
                        y = y * (DN_DK ** -0.5)
                hh = cb % DN_HEADS
                outs[cb // DN_HEADS][0, rsl, hh * LANES:(hh + 1) * LANES] = y


def _rope_lane_tables(seq):
    pos = np.arange(seq)
    n_freq = DN_DK // 4
    inv = ROPE_BASE ** (-np.arange(n_freq) / n_freq)
    ang = np.stack([(pos // GRID_W)[:, None] * inv[None, :], (pos % GRID_W)[:, None] * inv[None, :]], axis=1)
    cos = np.cos(ang).astype(np.float32)
    sin = np.sin(ang).astype(np.float32)
    lane = np.arange(DN_DK)
    axis, half, freq = lane // (2 * n_freq), (lane // n_freq) % 2, lane % n_freq
    cos_l = cos[:, axis, freq]
    sin_l = sin[:, axis, freq] * np.where(half == 0, -1.0, 1.0).astype(np.float32)[None, :]
    return jnp.asarray(cos_l), jnp.asarray(sin_l)


def _dnprep(dqkv, conv_w, use_rope):
    bsz, seq, width = dqkv.shape
    tt = ROW_TILE
    nt = seq // tt
    prev, nxt = _halo_specs(tt, seq, _DN_HALO, width)
    in_specs = [pl.BlockSpec((1, tt, width), lambda b, t: (b, t, 0)), prev, nxt,
                pl.BlockSpec((DN_CONV_K, width), lambda b, t: (0, 0))]
    args = [dqkv, dqkv, dqkv, conv_w]
    if use_rope:
        tab = pl.BlockSpec((tt, DN_DK), lambda b, t: (t, 0))
        in_specs += [tab, tab]
        args += list(_rope_lane_tables(seq))
    out = jax.ShapeDtypeStruct((bsz, seq, DN_QK_W), F32)
    ospec = pl.BlockSpec((1, tt, DN_QK_W), lambda b, t: (b, t, 0))
    return pl.pallas_call(
        functools.partial(_dnprep_kernel, tt=tt, nt=nt, use_rope=use_rope),
        grid=(bsz, nt),
        in_specs=in_specs,
        out_specs=[ospec, ospec, ospec],
        out_shape=[out, out, out],
        compiler_params=_cparams(2),
        name="deltanet_prep",
    )(*args)


_LOG_CHUNK = int(math.log2(DN_CHUNK))
_N_GATES = 4 * DN_HEADS


def _bmm(a, b):
    return lax.dot_general(a.astype(BF16), b.astype(BF16), (((2,), (1,)), ((0,), (0,))),
                           preferred_element_type=F32)


def _bmm_nt(a, b):
    return lax.dot_general(a.astype(BF16), b.astype(BF16), (((2,), (2,)), ((0,), (0,))),
                           preferred_element_type=F32)


def _bmm_tn(a, b):
    return lax.dot_general(a.astype(BF16), b.astype(BF16), (((1,), (1,)), ((0,), (0,))),
                           preferred_element_type=F32)


def _pair_cols(col_a, col_b, width):
    lane = lax.broadcasted_iota(jnp.int32, (col_a.shape[0], width), 1)
    return jnp.where(lane < width // 2, col_a, col_b)


def _block_diag(y):
    lane = lax.broadcasted_iota(jnp.int32, y.shape, 2)
    first = lane < y.shape[2] // 2
    return jnp.concatenate([jnp.where(first, y, 0.0), jnp.where(first, 0.0, y)], axis=1)


def _chunk_cumsum(x, axis, reverse):
    n = x.shape[axis]
    pos = lax.broadcasted_iota(jnp.int32, x.shape, axis) & (DN_CHUNK - 1)
    step = 1
    while step < DN_CHUNK:
        if reverse:
            x = x + jnp.where(pos < DN_CHUNK - step, pltpu.roll(x, n - step, axis), 0.0)
        else:
            x = x + jnp.where(pos >= step, pltpu.roll(x, step, axis), 0.0)
        step *= 2
    return x


def _dn_tile_pair(fwd, bwd, par, of_ref, ob_ref, s_scr):
    alog_ref, dtb_ref, alogt_ref, dtbt_ref = par
    cs = DN_CHUNK
    pw = 2 * cs
    hp_n = DN_HEADS // 2
    tile = of_ref.shape[1]
    n_chunks = tile // cs
    npd = n_chunks * hp_n
    ii = lax.broadcasted_iota(jnp.int32, (cs, pw), 0)
    jj = lax.broadcasted_iota(jnp.int32, (cs, pw), 1) & (cs - 1)
    sx = jnp.stack([jnp.where(ii >= jj, ii ^ jj, -1), jnp.where(ii <= jj, ii ^ jj, -1)], axis=0)[:, None]
    incl = sx >= 0
    before = sx > 0

    gc_l, gct_l, beta_l, gc2_l, beta2_l, tot2_l, q_l, k_l, v_l = ([] for _ in range(9))
    for d, (q_ref, k_ref, v_ref, ab_ref, abt_ref) in enumerate((fwd, bwd)):
        ab = ab_ref[0]
        abt = abt_ref[0]
        g_col = -jnp.exp(alog_ref[...]) * _softplus(ab + dtb_ref[...])
        g_row = -jnp.exp(alogt_ref[...]) * _softplus(abt + dtbt_ref[...])
        beta_all = _sigmoid(ab)
        gc_rows = jnp.concatenate([_chunk_cumsum(g_row[:, i * LANES:(i + 1) * LANES], 1, d == 1)
                                   for i in range(tile // LANES)], axis=1)
        qx = q_ref[0]
        kx = k_ref[0]
        vx = v_ref[0]
        for c in range(n_chunks):
            rows = slice(c * cs, (c + 1) * cs)
            gc_c = _chunk_cumsum(g_col[rows], 0, d == 1)
            gct_c = gc_rows[:, rows]
            last = 0 if d == 1 else cs - 1
            tot_c = gc_c[last:last + 1, :]
            for hp in range(hp_n):
                ga = d * DN_HEADS + 2 * hp
                gb = ga + 1
                ba = ga + 2 * DN_HEADS
                bb = ba + 1
                gc_l.append(_pair_cols(gc_c[:, ga:ga + 1], gc_c[:, gb:gb + 1], pw))
                gct_l.append(jnp.concatenate([gct_c[ga:ga + 1, :], gct_c[gb:gb + 1, :]], axis=1))
                beta_l.append(_pair_cols(beta_all[rows, ba:ba + 1], beta_all[rows, bb:bb + 1], pw))
                gc2_l.append(_pair_cols(gc_c[:, ga:ga + 1], gc_c[:, gb:gb + 1], 2 * LANES))
                beta2_l.append(_pair_cols(beta_all[rows, ba:ba + 1], beta_all[rows, bb:bb + 1], 2 * LANES))
                tot2_l.append(_pair_cols(tot_c[:, ga:ga + 1], tot_c[:, gb:gb + 1], 2 * LANES))
                cols = slice(hp * 2 * LANES, (hp + 1) * 2 * LANES)
                q_l.append(qx[rows, cols])
                k_l.append(kx[rows, cols])
                v_l.append(vx[rows, cols])

    def stk(xs):
        return jnp.stack(xs, axis=0)

    gc, gc_t, beta = stk(gc_l), stk(gct_l), stk(beta_l)
    gc2, beta2, tot2 = stk(gc2_l), stk(beta2_l), stk(tot2_l)
    q, k, v = stk(q_l), stk(k_l), stk(v_l)

    def by_dir(x):
        return x.reshape((2, npd) + x.shape[1:])

    def flat(x):
        return x.reshape((2 * npd,) + x.shape[2:])

    k_bd = _block_diag(k)
    kk = by_dir(_bmm_nt(k, k_bd))
    qk = by_dir(_bmm_nt(q, k_bd))
    decay = jnp.where(incl, jnp.exp(jnp.where(incl, by_dir(gc - gc_t), 0.0)), 0.0)
    a_mat = jnp.where(before, (by_dir(beta) * kk) * decay, 0.0)
    a_qk = flat(qk * decay)

    e_mat = flat(-jnp.where(sx == 1, a_mat, 0.0))
    for lb in range(1, _LOG_CHUNK):
        l_b = flat(jnp.where(lax.shift_right_arithmetic(sx, lb) == 1, a_mat, 0.0))
        p_mat = l_b + _bmm(l_b, _block_diag(e_mat))
        e_mat = e_mat - p_mat - _bmm(e_mat, _block_diag(p_mat))

    eg2 = jnp.exp(gc2)
    vb = v * beta2
    kb = k * (beta2 * eg2)
    zero2 = jnp.zeros(vb.shape, F32)
    rhs = jnp.concatenate([vb[..., :LANES], kb[..., :LANES], vb[..., LANES:], kb[..., LANES:]], axis=-1)
    rhs_bd = jnp.concatenate([jnp.concatenate([vb[..., :LANES], kb[..., :LANES], zero2], axis=-1),
                              jnp.concatenate([zero2, vb[..., LANES:], kb[..., LANES:]], axis=-1)], axis=1)
    sol = rhs + _bmm(e_mat, rhs_bd)
    q_dec = q * eg2
    k_dec = k * jnp.exp(tot2 - gc2)
    c_dec2 = jnp.exp(tot2)

    def heads(x, chunks, off, stride):
        out = []
        for d in range(2):
            for hp in range(hp_n):
                p = d * npd + chunks[d] * hp_n + hp
                out += [x[p][:, off:off + LANES], x[p][:, off + stride:off + stride + LANES]]
        return jnp.stack(out, axis=0)

    s = s_scr[...].reshape(2 * DN_HEADS, DN_DK, DN_DV)
    zero_u = jnp.zeros((cs, LANES), F32)
    for step in range(n_chunks):
        chunks = (step, n_chunks - 1 - step)
        u_base = heads(sol, chunks, 0, 2 * LANES)
        w_dec = heads(sol, chunks, LANES, 2 * LANES)
        qd = heads(q_dec, chunks, 0, LANES)
        kd = heads(k_dec, chunks, 0, LANES)
        cd = heads(c_dec2, chunks, 0, LANES)[:, :, :1]
        ws_qs = _bmm(jnp.concatenate([w_dec, qd], axis=1), s)
        u = u_base - ws_qs[:, :cs]
        u_bd = jnp.stack([jnp.concatenate([jnp.concatenate([u[2 * i], zero_u], axis=-1),
                                           jnp.concatenate([zero_u, u[2 * i + 1]], axis=-1)], axis=0)
                          for i in range(DN_HEADS)], axis=0)
        aqk_sel = jnp.stack([a_qk[d * npd + chunks[d] * hp_n + hp] for d in range(2) for hp in range(hp_n)], axis=0)
        o_loc = _bmm(aqk_sel, u_bd)
        s = s * cd + _bmm_tn(kd, u)
        for d, o_ref in enumerate((of_ref, ob_ref)):
            c = chunks[d]
            for h in range(DN_HEADS):
                i = d * DN_HEADS + h
                o_ref[0, c * cs:(c + 1) * cs, h * LANES:(h + 1) * LANES] = (
                    ws_qs[i, cs:] + o_loc[i // 2][:, (h % 2) * LANES:(h % 2 + 1) * LANES]).astype(o_ref.dtype)
    s_scr[...] = s.reshape(2, DN_HEADS, DN_DK, DN_DV)


def _dnscan_kernel(*refs, nt, has_s0, want_state):
    it = iter(refs)
    fwd = [next(it) for _ in range(5)]
    bwd = [next(it) for _ in range(5)]
    par = [next(it) for _ in range(4)]
    s0_ref = next(it) if has_s0 else None
    of_ref = next(it)
    ob_ref = next(it)
    st_ref = next(it) if want_state else None
    s_scr = next(it)
    t = pl.program_id(1)

    @pl.when(t == 0)
    def _():
        if has_s0:
            s_scr[...] = s0_ref[0]
        else:
            s_scr[...] = jnp.zeros(s_scr.shape, F32)

    _dn_tile_pair(fwd, bwd, par, of_ref, ob_ref, s_scr)

    if want_state:
        @pl.when(t == nt - 1)
        def _():
            st_ref[0] = s_scr[...]


def _dnscan(q, k, v, ab, abt, a_log, dt_bias, s0, want_state):
    bsz, seq, _ = q.shape
    tile = min(DN_TILE, seq)
    nt = seq // tile

    def tile_specs(tmap):
        qs = pl.BlockSpec((1, tile, DN_V_W), lambda b, t: (b, tmap(t), 0))
        return [qs, qs, qs,
                pl.BlockSpec((1, tile, _N_GATES), lambda b, t: (b, tmap(t), 0)),
                pl.BlockSpec((1, _N_GATES, tile), lambda b, t: (b, 0, tmap(t)))]

    zeros8 = jnp.zeros((2 * DN_HEADS,), F32)
    alog16 = jnp.concatenate([a_log.reshape(-1), zeros8])
    dtb16 = jnp.concatenate([dt_bias.reshape(-1), zeros8])
    prow = pl.BlockSpec((1, _N_GATES), lambda b, t: (0, 0))
    pcol = pl.BlockSpec((_N_GATES, 1), lambda b, t: (0, 0))
    in_specs = tile_specs(lambda t: t) + tile_specs(lambda t: nt - 1 - t) + [prow, prow, pcol, pcol]
    args = [q, k, v, ab, abt, q, k, v, ab, abt,
            alog16.reshape(1, _N_GATES), dtb16.reshape(1, _N_GATES),
            alog16.reshape(_N_GATES, 1), dtb16.reshape(_N_GATES, 1)]
    st_block = (1, 2, DN_HEADS, DN_DK, DN_DV)
    if s0 is not None:
        in_specs.append(pl.BlockSpec(st_block, lambda b, t: (b, 0, 0, 0, 0)))
        args.append(s0)
    o_shape = jax.ShapeDtypeStruct((bsz, seq, DN_V_W), BF16)
    out_specs = [pl.BlockSpec((1, tile, DN_V_W), lambda b, t: (b, t, 0)),
                 pl.BlockSpec((1, tile, DN_V_W), lambda b, t: (b, nt - 1 - t, 0))]
    out_shape = [o_shape, o_shape]
    if want_state:
        out_specs.append(pl.BlockSpec(st_block, lambda b, t: (b, 0, 0, 0, 0)))
        out_shape.append(jax.ShapeDtypeStruct((bsz, 2, DN_HEADS, DN_DK, DN_DV), F32))
    return pl.pallas_call(
        functools.partial(_dnscan_kernel, nt=nt, has_s0=s0 is not None, want_state=want_state),
        grid=(bsz, nt),
        in_specs=in_specs,
        out_specs=out_specs,
        out_shape=out_shape,
        scratch_shapes=[pltpu.VMEM((2, DN_HEADS, DN_DK, DN_DV), F32)],
        compiler_params=_cparams(2),
        name="deltanet_scan",
    )(*args)


def _outproj_kernel(x_ref, a_ref, b_ref, of_ref, ob_ref, z_ref, ng_ref, ga_ref, w_ref, o_ref):
    o = of_ref[0].astype(F32) + ob_ref[0].astype(F32)
    z = z_ref[0].astype(F32)
    acc = jnp.dot(a_ref[0].astype(BF16), w_ref[0:CONV_W, :], preferred_element_type=F32)
    acc = acc + jnp.dot(b_ref[0].astype(BF16), w_ref[CONV_W:CONV_W + NA_W, :], preferred_element_type=F32)
    for h in range(DN_HEADS):
        cols = slice(h * DN_DV, (h + 1) * DN_DV)
        oh = o[:, cols]
        y = oh * lax.rsqrt(jnp.mean(oh * oh, axis=-1, keepdims=True) + EPS) * ng_ref[...] * _silu(z[:, cols])
        r0 = CONV_W + NA_W + h * DN_DV
        acc = acc + jnp.dot(y.astype(BF16), w_ref[r0:r0 + DN_DV, :], preferred_element_type=F32)
    o_ref[0] = x_ref[0] + ga_ref[0] * acc


def _outproj(x, out_a, out_b, o_f, o_b, z, norm_g, gate, w_b, layer):
    shape0 = x.shape
    tm = PROJ_TILE
    x, out_a, out_b, o_f, o_b, z = (_merge_short_sequences(a, tm, gate.shape[0] == 1)
                                    for a in (x, out_a, out_b, o_f, o_b, z))
    bsz, seq, d = x.shape

    def row(w):
        return pl.BlockSpec((1, tm, w), lambda b, t: (b, t, 0))

    return pl.pallas_call(
        _outproj_kernel,
        grid=(bsz, seq // tm),
        in_specs=[row(d), row(CONV_W), row(NA_W), row(DN_V_W), row(DN_V_W), row(DN_V_W),
                  pl.BlockSpec((1, DN_DV), lambda b, t: (0, 0)),
                  pl.BlockSpec((1, 1, d), _bmap(gate.shape[0])),
                  pl.BlockSpec((None, MIX_W, d), lambda b, t: (layer, 0, 0))],
        out_specs=row(d),
        out_shape=jax.ShapeDtypeStruct((bsz, seq, d), F32),
        compiler_params=_cparams(2),
        name="outproj",
    )(x, out_a, out_b, o_f, o_b, z, norm_g.reshape(1, -1), gate, w_b).reshape(shape0)


def _first_argmax(x, lane, valid):
    xm = jnp.where(valid, x, -jnp.inf)
    m = jnp.max(xm, axis=-1, keepdims=True)
    idx = jnp.min(jnp.where(valid & (xm == m), lane, float(LANES)), axis=-1, keepdims=True)
    return m, idx


def _route(h, rw_hi_ref, rw_lo_ref, rb_ref):
    h_hi = h.astype(BF16)
    h_lo = (h - h_hi.astype(F32)).astype(BF16)
    logits = (jnp.dot(h_hi, rw_hi_ref[...], preferred_element_type=F32)
              + jnp.dot(h_lo, rw_hi_ref[...], preferred_element_type=F32)
              + jnp.dot(h_hi, rw_lo_ref[...], preferred_element_type=F32)) + rb_ref[...]
    lane = lax.broadcasted_iota(jnp.int32, logits.shape, 1).astype(F32)
    is_grp = lane < N_GROUPS
    gmax, g_sel = _first_argmax(logits, lane, is_grp)
    pg_sel = 1.0 / jnp.sum(jnp.where(is_grp, jnp.exp(logits - gmax), 0.0), axis=-1, keepdims=True)
    e_lane = lane - N_GROUPS
    in_grp = (e_lane >= g_sel * EXPERTS_PER_GROUP) & (e_lane < (g_sel + 1) * EXPERTS_PER_GROUP)
    m1, i1 = _first_argmax(logits, lane, in_grp)
    m2, i2 = _first_argmax(logits, lane, in_grp & (lane != i1))
    e2 = jnp.exp(m2 - m1)
    w1 = pg_sel / (1.0 + e2)
    w2 = pg_sel * e2 / (1.0 + e2)
    return g_sel, jnp.where(lane == i1, w1, 0.0) + jnp.where(lane == i2, w2, 0.0)


def _moe_kernel(x_ref, g_ref, sc_ref, sh_ref, ga_ref, rw_hi_ref, rw_lo_ref, rb_ref, tri_ref, w1_ref, w3_ref, w2_ref,
                gf_ref, o_ref, h_scr, ghi_scr, glo_scr, oh_scr, rank_scr, oht_scr, rankt_scr, acc_scr, cnt_scr,
                *, final_norm):
    grp = pl.program_id(1)
    tm = x_ref.shape[0]

    @pl.when(grp == 0)
    def _():
        h = _mod_norm(x_ref[...], g_ref[...], sc_ref[0], sh_ref[0])
        h_scr[...] = h.astype(BF16)
        g_sel, gates = _route(h, rw_hi_ref, rw_lo_ref, rb_ref)
        g_hi = gates.astype(BF16)
        ghi_scr[...] = g_hi
        glo_scr[...] = (gates - g_hi.astype(F32)).astype(BF16)
        acc_scr[...] = jnp.zeros(acc_scr.shape, F32)
        lane = lax.broadcasted_iota(jnp.int32, (tm, LANES), 1).astype(F32)
        one_hot = jnp.where(lane == g_sel, 1.0, 0.0)
        oh_b = one_hot.astype(BF16)
        oh_scr[...] = one_hot
        rank_scr[...] = jnp.dot(tri_ref[...], oh_b, preferred_element_type=F32)
        pick = jnp.where(lax.broadcasted_iota(jnp.int32, (SUBLANES, LANES), 0)
                         == lax.broadcasted_iota(jnp.int32, (SUBLANES, LANES), 1), 1.0, 0.0).astype(BF16)
        oh_t = lax.dot_general(pick, oh_b, _NT, preferred_element_type=F32)
        oht_scr[...] = oh_t
        rankt_scr[...] = lax.dot_general(oh_t.astype(BF16), tri_ref[...], _NT, preferred_element_type=F32)
        for k in range(N_GROUPS):
            cnt_scr[k] = jnp.sum(one_hot[:, k:k + 1]).astype(jnp.int32)

    hb = h_scr[...]
    lane = lax.broadcasted_iota(jnp.int32, (tm, LANES), 1)
    in_col = jnp.sum(jnp.where(lane == grp, oh_scr[...], 0.0), axis=-1, keepdims=True)
    rank_col = jnp.sum(jnp.where(lane == grp, rank_scr[...], 0.0), axis=-1, keepdims=True)
    in_row = oht_scr[pl.ds(grp, 1), :]
    rank_row = rankt_scr[pl.ds(grp, 1), :]
    def chunk(first_rank, n_rows):
        slot_r = lax.broadcasted_iota(jnp.int32, (n_rows, tm), 0).astype(F32)
        slot_c = lax.broadcasted_iota(jnp.int32, (tm, n_rows), 1).astype(F32)
        glane = lax.broadcasted_iota(jnp.int32, (n_rows, LANES), 1)
        base = first_rank.astype(F32)
        take = jnp.where((in_row > 0.5) & (rank_row - base == slot_r), 1.0, 0.0).astype(BF16)
        put = jnp.where((in_col > 0.5) & (rank_col - base == slot_c), 1.0, 0.0).astype(BF16)
        xs = jnp.dot(take, hb, preferred_element_type=F32).astype(BF16)
        gates = (jnp.dot(take, ghi_scr[...], preferred_element_type=F32)
                 + jnp.dot(take, glo_scr[...], preferred_element_type=F32))
        y = None
        for half in range(EXPERTS_PER_GROUP // 2):
            hid = []
            for j in (2 * half, 2 * half + 1):
                gate_e = jnp.sum(jnp.where(glane == N_GROUPS + grp * EXPERTS_PER_GROUP + j, gates, 0.0),
                                 axis=-1, keepdims=True)
                up = _silu(jnp.dot(xs, w1_ref[0, j], preferred_element_type=F32)) * jnp.dot(
                    xs, w3_ref[0, j], preferred_element_type=F32)
                hid.append((up * gate_e).astype(BF16))
            rows = slice(2 * half * D_EXPERT, 2 * (half + 1) * D_EXPERT)
            part = jnp.dot(jnp.concatenate(hid, axis=-1), w2_ref[0, 0, rows, :], preferred_element_type=F32)
            y = part if y is None else y + part
        acc_scr[...] += jnp.dot(put, y.astype(BF16), preferred_element_type=F32)

    count = cnt_scr[grp]
    rest = lax.rem(count, MOE_ROWS)
    n_full = count // MOE_ROWS + (rest > MOE_ROWS // 2).astype(jnp.int32)

    def full_chunk(c, carry):
        chunk(c * MOE_ROWS, MOE_ROWS)
        return carry

    lax.fori_loop(0, n_full, full_chunk, 0)

    @pl.when((rest > 0) & (rest <= MOE_ROWS // 2))
    def _():
        chunk(n_full * MOE_ROWS, MOE_ROWS // 2)

    @pl.when(grp == N_GROUPS - 1)
    def _():
        y = x_ref[...] + ga_ref[0] * acc_scr[...]
        if final_norm:
            y = y * lax.rsqrt(jnp.mean(y * y, axis=-1, keepdims=True) + EPS) * gf_ref[...]
        o_ref[...] = y


def _moe(x, g, scale, shift, gate, rw_hi, rw_lo, rb, w1_b, w3_b, w2_b, g_final, layer, final_norm):
    bsz, seq, d = x.shape
    x2 = x.reshape(bsz * seq, d)
    if scale.shape[0] == 1:
        tm = MOE_TILE
        mm = lambda i, e: (0, 0, 0)
    else:
        tm = min(MOE_TILE, seq)
        per_seq = seq // tm
        mm = lambda i, e: (i // per_seq, 0, 0)
    tri = jnp.asarray(np.tril(np.ones((tm, tm), np.float32), -1), BF16)
    vec = pl.BlockSpec((1, d), lambda i, e: (0, 0))
    mod = pl.BlockSpec((1, 1, d), mm)
    rspec = pl.BlockSpec((d, LANES), lambda i, e: (0, 0))
    gw = EXPERTS_PER_GROUP * D_EXPERT
    out = pl.pallas_call(
        functools.partial(_moe_kernel, final_norm=final_norm),
        grid=(bsz * seq // tm, N_GROUPS),
        in_specs=[pl.BlockSpec((tm, d), lambda i, e: (i, 0)), vec, mod, mod, mod,
                  rspec, rspec, pl.BlockSpec((1, LANES), lambda i, e: (0, 0)),
                  pl.BlockSpec((tm, tm), lambda i, e: (0, 0)),
                  pl.BlockSpec((None, 1, EXPERTS_PER_GROUP, d, D_EXPERT), lambda i, e: (layer, e, 0, 0, 0)),
                  pl.BlockSpec((None, 1, EXPERTS_PER_GROUP, d, D_EXPERT), lambda i, e: (layer, e, 0, 0, 0)),
                  pl.BlockSpec((1, 1, gw, d), lambda i, e: (layer, e, 0, 0)),
                  vec],
        out_specs=pl.BlockSpec((tm, d), lambda i, e: (i, 0)),
        out_shape=jax.ShapeDtypeStruct((bsz * seq, d), F32),
        scratch_shapes=[pltpu.VMEM((tm, d), BF16), pltpu.VMEM((tm, LANES), BF16), pltpu.VMEM((tm, LANES), BF16),
                        pltpu.VMEM((tm, LANES), F32), pltpu.VMEM((tm, LANES), F32),
                        pltpu.VMEM((SUBLANES, tm), F32), pltpu.VMEM((SUBLANES, tm), F32),
                        pltpu.VMEM((tm, d), F32), pltpu.SMEM((N_GROUPS,), jnp.int32)],
        compiler_params=_cparams(2),
        name="moe",
    )(x2, g, scale, shift, gate, rw_hi, rw_lo, rb, tri, w1_b, w3_b, w2_b, g_final.reshape(1, -1))
    return out.reshape(bsz, seq, d)


def _split_hi_lo(w):
    hi = w.astype(BF16)
    return hi, (w - hi.astype(F32)).astype(BF16)


def kernel(x_prompt, x_sample, cache_na_k, cache_na_v, state_delta, c, c_ctx, w_mod, b_mod, g_norm1, g_norm2,
           w_in, conv_a_w, conv_a_b, ln_a_g, ln_a_b, na_rel_bias, dn_conv_w, dn_a_log, dn_dt_bias, dn_norm_g,
           w_out, router_wg, router_bg, router_we, router_be, w1, w3, w2, g_final):
    n_dec = x_sample.shape[0]
    d = D_MODEL
    cond8 = jnp.concatenate([c_ctx[None, :], c, jnp.zeros((8 - 1 - n_dec, d), F32)], axis=0)
    mods = _adaln(cond8, w_mod, b_mod).reshape(DEPTH, 8, 6, d)

    w_in_b = w_in.astype(BF16)
    w_abt_b = jnp.swapaxes(w_in[:, :, OFF_DN_AB:], 1, 2).astype(BF16)
    w_out_b = w_out.astype(BF16)
    w1_b = w1.astype(BF16).reshape(DEPTH, N_GROUPS, EXPERTS_PER_GROUP, d, D_EXPERT)
    w3_b = w3.astype(BF16).reshape(DEPTH, N_GROUPS, EXPERTS_PER_GROUP, d, D_EXPERT)
    w2_b = w2.astype(BF16).reshape(DEPTH, N_GROUPS, EXPERTS_PER_GROUP * D_EXPERT, d)
    pad = LANES - N_GROUPS - N_EXPERTS
    rw = jnp.concatenate([router_wg, router_we, jnp.zeros((DEPTH, d, pad), F32)], axis=-1)
    rw_hi, rw_lo = _split_hi_lo(rw)
    rb = jnp.concatenate([router_bg, router_be, jnp.zeros((DEPTH, pad), F32)], axis=-1)

    past = cache_na_k.shape[2]
    cache_k = cache_na_k.reshape(n_dec, DEPTH, past, NA_W)
    cache_v = cache_na_v.reshape(n_dec, DEPTH, past, NA_W)
    rows = x_sample.shape[1] // GRID_W

    def layer(x, l, is_ctx):
        m = mods[l, 0:1] if is_ctx else mods[l, 1:1 + n_dec]
        sh1, sc1, ga1, sh2, sc2, ga2 = (m[:, i:i + 1, :] for i in range(6))
        glu, q_na, k_na, v_na, dqkv, dz, ab, abt = _inproj(x, g_norm1[l:l + 1], sc1, sh1, w_in_b, w_abt_b, l)
        out_a = _cconv(glu, conv_a_w[l], conv_a_b[l], ln_a_g[l], ln_a_b[l])
        if is_ctx:
            out_b = _ctx_attn(q_na, k_na, v_na)
        else:
            out_b = _na_attn(q_na, k_na, v_na, cache_k, cache_v, l, _na_bias_tables(na_rel_bias[l], rows))
        dq, dk, dv = _dnprep(dqkv, dn_conv_w[l], use_rope=not is_ctx)
        scan = _dnscan(dq, dk, dv, ab, abt, dn_a_log[l], dn_dt_bias[l],
                       None if is_ctx else state_delta[:, l], want_state=is_ctx)
        o_f, o_b = scan[0], scan[1]
        x = _outproj(x, out_a, out_b, o_f, o_b, dz, dn_norm_g[l], ga1, w_out_b, l)
        x = _moe(x, g_norm2[l:l + 1], sc2, sh2, ga2, rw_hi[l], rw_lo[l], rb[l:l + 1], w1_b, w3_b, w2_b,
                 g_final, layer=l, final_norm=(l == DEPTH - 1))
        return x, k_na, v_na, (scan[2] if is_ctx else None)

    xc = x_prompt
    new_k, new_v, new_s = [], [], []
    for l in range(DEPTH):
        xc, k_na, v_na, s_ctx = layer(xc, l, True)
        new_k.append(k_na.reshape(k_na.shape[0], k_na.shape[1], NA_HEADS, NA_HEAD_DIM))
        new_v.append(v_na.reshape(v_na.shape[0], v_na.shape[1], NA_HEADS, NA_HEAD_DIM))
        new_s.append(s_ctx)
    xs = x_sample
    for l in range(DEPTH):
        xs, _, _, _ = layer(xs, l, False)
    return (xc, xs, jnp.stack(new_k, axis=1), jnp.stack(new_v, axis=1), jnp.stack(new_s, axis=1))
```

```python
import functools
import math

import numpy as np
import jax
import jax.numpy as jnp
from jax import lax
from jax.experimental import pallas as pl
from jax.experimental.pallas import tpu as pltpu

F32 = jnp.float32
BF16 = jnp.bfloat16

D_MODEL = 1024
DEPTH = 2
GRID_W = 64
CONV_W = 256
CONV_K = 31
NA_HEADS = 4
NA_HEAD_DIM = 64
NA_W = NA_HEADS * NA_HEAD_DIM
NA_ROWS = 8
NA_COLS = 16
DN_HEADS = 4
DN_DK = 128
DN_DV = 128
DN_QK_W = DN_HEADS * DN_DK
DN_V_W = DN_HEADS * DN_DV
DN_CONV_K = 5
DN_CHUNK = 64
ROPE_BASE = 10000.0
MIX_W = CONV_W + NA_W + DN_V_W
OFF_NA = 2 * CONV_W
OFF_DN_QKV = OFF_NA + 3 * NA_W
OFF_DN_Z = OFF_DN_QKV + 2 * DN_QK_W + DN_V_W
OFF_DN_AB = OFF_DN_Z + DN_V_W
PROJ_W = OFF_DN_AB + 4 * DN_HEADS
N_GROUPS = 4
EXPERTS_PER_GROUP = 4
N_EXPERTS = N_GROUPS * EXPERTS_PER_GROUP
D_EXPERT = 256
EPS = 1e-6
NEG_INF = -1e30

LANES = 128
SUBLANES = 8
ROW_TILE = 256
CONV_TILE = 512
PROJ_TILE = 512
DN_TILE = 512
NA_QROWS = 4
NA_WIN_ROWS = 12
MOE_TILE = 1024
MOE_ROWS = 256
VMEM_LIMIT = 48 * 1024 * 1024
MOE_VMEM_LIMIT = 56 * 1024 * 1024

_NT = (((1,), (1,)), ((), ()))


def _cparams(n_axes):
    return pltpu.CompilerParams(dimension_semantics=("arbitrary",) * n_axes,
                                vmem_limit_bytes=VMEM_LIMIT)


def _sigmoid(x):
    return 1.0 / (1.0 + jnp.exp(-x))


def _silu(x):
    return x * _sigmoid(x)


def _softplus(x):
    return jnp.maximum(x, 0.0) + jnp.log(1.0 + jnp.exp(-jnp.abs(x)))


def _dot(a, b):
    return jnp.dot(a.astype(BF16), b.astype(BF16), preferred_element_type=F32)


def _adaln_kernel(c_ref, w_ref, b_ref, o_ref):
    c = c_ref[...]
    o_ref[0] = _dot(_silu(c), w_ref[0]) + b_ref[0]


def _adaln(cond8, w_mod, b_mod):
    n_l, d, n6 = w_mod.shape
    tn = 1536
    return pl.pallas_call(
        _adaln_kernel,
        grid=(n_l, n6 // tn),
        in_specs=[pl.BlockSpec((8, d), lambda l, j: (0, 0)),
                  pl.BlockSpec((1, d, tn), lambda l, j: (l, 0, j)),
                  pl.BlockSpec((1, 1, tn), lambda l, j: (l, 0, j))],
        out_specs=pl.BlockSpec((1, 8, tn), lambda l, j: (l, 0, j)),
        out_shape=jax.ShapeDtypeStruct((n_l, 8, n6), F32),
        compiler_params=_cparams(2),
        name="adaln",
    )(cond8, w_mod, b_mod.reshape(n_l, 1, n6))


def _mod_norm(x, g, scale, shift):
    ms = jnp.mean(x * x, axis=-1, keepdims=True)
    return (x * lax.rsqrt(ms + EPS) * g) * (1.0 + scale) + shift


def _inproj_kernel(x_ref, g_ref, sc_ref, sh_ref, w_ref, wabt_ref,
                   glu_ref, q_ref, k_ref, v_ref, dqkv_ref, z_ref, ab_ref, abt_ref):
    hb = _mod_norm(x_ref[0], g_ref[...], sc_ref[0], sh_ref[0]).astype(BF16)

    def proj(a, b):
        return jnp.dot(hb, w_ref[:, a:b], preferred_element_type=F32)

    glu_ref[0] = proj(0, OFF_NA)
    q_ref[0] = proj(OFF_NA, OFF_NA + NA_W).astype(BF16)
    k_ref[0] = proj(OFF_NA + NA_W, OFF_NA + 2 * NA_W)
    v_ref[0] = proj(OFF_NA + 2 * NA_W, OFF_DN_QKV)
    dqkv_ref[0] = proj(OFF_DN_QKV, OFF_DN_Z).astype(BF16)
    z_ref[0] = proj(OFF_DN_Z, OFF_DN_AB).astype(BF16)
    ab_ref[0] = proj(OFF_DN_AB, PROJ_W)
    abt_ref[0] = lax.dot_general(wabt_ref[...], hb, _NT, preferred_element_type=F32)


def _bmap(bm):
    if bm == 1:
        return lambda b, t: (0, 0, 0)
    return lambda b, t: (b, 0, 0)


def _merge_short_sequences(x, tm, shared_mod):
    bsz, seq, d = x.shape
    if shared_mod and seq < tm:
        return x.reshape(bsz * seq // tm, tm, d)
    return x


def _inproj(x, g, scale, shift, w_b, wabt_b, layer):
    bsz0, seq0, _ = x.shape
    tm = PROJ_TILE
    x = _merge_short_sequences(x, tm, scale.shape[0] == 1)
    outs = _inproj_call(x, g, scale, shift, w_b, wabt_b, layer, tm)
    if x.shape[0] == bsz0:
        return outs
    per = tm // seq0
    row_outs = [o.reshape(bsz0, seq0, o.shape[-1]) for o in outs[:-1]]
    abt = outs[-1].reshape(x.shape[0], 4 * DN_HEADS, per, seq0)
    abt = jnp.swapaxes(abt, 1, 2).reshape(bsz0, 4 * DN_HEADS, seq0)
    return row_outs + [abt]


def _inproj_call(x, g, scale, shift, w_b, wabt_b, layer, tm):
    bsz, seq, d = x.shape
    widths = (OFF_NA, NA_W, NA_W, NA_W, OFF_DN_Z - OFF_DN_QKV, DN_V_W, 4 * DN_HEADS)
    dtypes = (F32, BF16, F32, F32, BF16, BF16, F32)
    out_shape = [jax.ShapeDtypeStruct((bsz, seq, w), dt) for w, dt in zip(widths, dtypes)]
    out_shape.append(jax.ShapeDtypeStruct((bsz, 4 * DN_HEADS, seq), F32))
    out_specs = [pl.BlockSpec((1, tm, w), lambda b, t: (b, t, 0)) for w in widths]
    out_specs.append(pl.BlockSpec((1, 4 * DN_HEADS, tm), lambda b, t: (b, 0, t)))
    mm = _bmap(scale.shape[0])
    return pl.pallas_call(
        _inproj_kernel,
        grid=(bsz, seq // tm),
        in_specs=[pl.BlockSpec((1, tm, d), lambda b, t: (b, t, 0)),
                  pl.BlockSpec((1, d), lambda b, t: (0, 0)),
                  pl.BlockSpec((1, 1, d), mm),
                  pl.BlockSpec((1, 1, d), mm),
                  pl.BlockSpec((None, d, PROJ_W), lambda b, t: (layer, 0, 0)),
                  pl.BlockSpec((None, 4 * DN_HEADS, d), lambda b, t: (layer, 0, 0))],
        out_specs=out_specs,
        out_shape=out_shape,
        compiler_params=_cparams(2),
        name="inproj",
    )(x, g, scale, shift, w_b, wabt_b)


_CC_HALO = 16
_CC_SUB = 64


def _cconv_kernel(main_ref, prev_ref, next_ref, w_ref, b_ref, g_ref, beta_ref, o_ref, scr, *, tt, nt):
    t = pl.program_id(1)

    def glu(a):
        return a[:, :CONV_W] * _sigmoid(a[:, CONV_W:])

    scr[0, 0:_CC_HALO, :] = jnp.where(t > 0, glu(prev_ref[0]), 0.0)
    scr[0, _CC_HALO:_CC_HALO + tt, :] = glu(main_ref[0])
    scr[0, _CC_HALO + tt:2 * _CC_HALO + tt, :] = jnp.where(t < nt - 1, glu(next_ref[0]), 0.0)
    n_shift = tt + 2 * _CC_HALO - SUBLANES
    for r in range(1, SUBLANES):
        scr[r, 0:n_shift, :] = scr[0, r:r + n_shift, :]
    pad = (CONV_K - 1) // 2
    for s in range(tt // _CC_SUB):
        base = s * _CC_SUB + _CC_HALO - pad
        acc = jnp.zeros((_CC_SUB, CONV_W), F32) + b_ref[...]
        for k in range(CONV_K):
            r = (base + k) % SUBLANES
            a0 = base + k - r
            acc = acc + w_ref[k:k + 1, :] * scr[r, a0:a0 + _CC_SUB, :]
        mu = jnp.mean(acc, axis=-1, keepdims=True)
        xc = acc - mu
        var = jnp.mean(xc * xc, axis=-1, keepdims=True)
        y = xc * lax.rsqrt(var + EPS) * g_ref[...] + beta_ref[...]
        o_ref[0, s * _CC_SUB:(s + 1) * _CC_SUB, :] = _silu(y).astype(BF16)


def _halo_specs(tt, seq, halo, width):
    per = tt // halo
    last = seq // halo - 1
    prev = pl.BlockSpec((1, halo, width), lambda b, t: (b, jnp.maximum(t * per - 1, 0), 0))
    nxt = pl.BlockSpec((1, halo, width), lambda b, t: (b, jnp.minimum((t + 1) * per, last), 0))
    return prev, nxt


def _cconv(glu, w, b, g, beta):
    bsz, seq, _ = glu.shape
    tt = min(CONV_TILE, seq)
    nt = seq // tt
    prev, nxt = _halo_specs(tt, seq, _CC_HALO, 2 * CONV_W)
    vec = pl.BlockSpec((1, CONV_W), lambda b_, t: (0, 0))
    return pl.pallas_call(
        functools.partial(_cconv_kernel, tt=tt, nt=nt),
        grid=(bsz, nt),
        in_specs=[pl.BlockSpec((1, tt, 2 * CONV_W), lambda b_, t: (b_, t, 0)), prev, nxt,
                  pl.BlockSpec((CONV_K, CONV_W), lambda b_, t: (0, 0)), vec, vec, vec],
        out_specs=pl.BlockSpec((1, tt, CONV_W), lambda b_, t: (b_, t, 0)),
        out_shape=jax.ShapeDtypeStruct((bsz, seq, CONV_W), BF16),
        scratch_shapes=[pltpu.VMEM((SUBLANES, tt + 2 * _CC_HALO, CONV_W), F32)],
        compiler_params=_cparams(2),
        name="conformer_conv",
    )(glu, glu, glu, w, b.reshape(1, -1), g.reshape(1, -1), beta.reshape(1, -1))


def _head_lane_id(shape):
    return lax.shift_right_logical(lax.broadcasted_iota(jnp.int32, shape, 1), 6)


def _ctx_attn_kernel(q_ref, k_ref, v_ref, o_ref):
    q = q_ref[0].astype(F32) * (NA_HEAD_DIM ** -0.5)
    k = k_ref[0].astype(BF16)
    v = v_ref[0].astype(BF16)
    head = _head_lane_id(q.shape)
    out = jnp.zeros(q.shape, F32)
    for h in range(NA_HEADS):
        s = lax.dot_general(jnp.where(head == h, q, 0.0).astype(BF16), k, _NT, preferred_element_type=F32)
        m = jnp.max(s, axis=-1, keepdims=True)
        p = jnp.exp(s - m)
        den = jnp.sum(p, axis=-1, keepdims=True)
        oh = jnp.dot(p.astype(BF16), v, preferred_element_type=F32)
        out = jnp.where(head == h, oh / den, out)
    o_ref[0] = out.astype(BF16)


def _ctx_attn(q, k, v):
    bsz, seq, w = q.shape
    spec = pl.BlockSpec((1, seq, w), lambda b: (b, 0, 0))
    return pl.pallas_call(
        _ctx_attn_kernel,
        grid=(bsz,),
        in_specs=[spec, spec, spec],
        out_specs=spec,
        out_shape=jax.ShapeDtypeStruct((bsz, seq, w), BF16),
        compiler_params=_cparams(1),
        name="context_attention",
    )(q, k, v)


def _na_window_start(blk, rows):
    return jnp.clip(blk * NA_QROWS - NA_ROWS // 2, 0, rows - NA_WIN_ROWS)


def _na_kernel(q_ref, k_ref, v_ref, kc_ref, vc_ref, eb_ref, o_ref, *, rows):
    blk = pl.program_id(1)
    ws = pl.multiple_of(_na_window_start(blk, rows) * GRID_W, GRID_W)
    nwin = NA_WIN_ROWS * GRID_W
    q = q_ref[0].astype(F32) * (NA_HEAD_DIM ** -0.5)
    kl = k_ref[0, pl.ds(ws, nwin), :].astype(BF16)
    vl = v_ref[0, pl.ds(ws, nwin), :].astype(BF16)
    kc = kc_ref[...].astype(BF16)
    vc = vc_ref[...].astype(BF16)
    head = _head_lane_id(q.shape)
    out = jnp.zeros(q.shape, F32)
    for h in range(NA_HEADS):
        qh = jnp.where(head == h, q, 0.0).astype(BF16)
        sl = lax.dot_general(qh, kl, _NT, preferred_element_type=F32) + eb_ref[0, h]
        sc = lax.dot_general(qh, kc, _NT, preferred_element_type=F32)
        m = jnp.maximum(jnp.max(sl, axis=-1, keepdims=True), jnp.max(sc, axis=-1, keepdims=True))
        p_l = jnp.exp(sl - m)
        p_c = jnp.exp(sc - m)
        den = jnp.sum(p_l, axis=-1, keepdims=True) + jnp.sum(p_c, axis=-1, keepdims=True)
        oh = (jnp.dot(p_l.astype(BF16), vl, preferred_element_type=F32)
              + jnp.dot(p_c.astype(BF16), vc, preferred_element_type=F32))
        out = jnp.where(head == h, oh / den, out)
    o_ref[0] = out.astype(BF16)


def _na_bias_tables(rel_bias, rows):
    n_heads = rel_bias.shape[0]
    c = np.arange(GRID_W)
    cs = np.clip(c - NA_COLS // 2, 0, GRID_W - NA_COLS)
    col_valid = (c[None, :] >= cs[:, None]) & (c[None, :] < cs[:, None] + NA_COLS)
    padw = GRID_W - NA_COLS
    padded = jnp.pad(rel_bias, ((0, 0), (0, 0), (padw, padw)))
    toeplitz = jnp.stack([padded[:, :, GRID_W - 1 - ci:2 * GRID_W - 1 - ci] for ci in range(GRID_W)], axis=2)
    toeplitz = jnp.where(jnp.asarray(col_valid)[None, None], toeplitz, NEG_INF)
    masked = jnp.full((n_heads, GRID_W, GRID_W), NEG_INF, F32)
    nblk = rows // NA_QROWS
    tables = []
    for blk in (0, 1, nblk - 1):
        ws = int(np.clip(blk * NA_QROWS - NA_ROWS // 2, 0, rows - NA_WIN_ROWS))
        q_rows = []
        for a in range(NA_QROWS):
            r = blk * NA_QROWS + a
            rs = int(np.clip(r - NA_ROWS // 2, 0, rows - NA_ROWS))
            blocks = []
            for j in range(NA_WIN_ROWS):
                kr = ws + j
                blocks.append(toeplitz[:, kr - r + NA_ROWS - 1] if rs <= kr < rs + NA_ROWS else masked)
            q_rows.append(jnp.concatenate(blocks, axis=-1))
        tables.append(jnp.concatenate(q_rows, axis=1))
    return jnp.stack(tables, axis=0)


def _na_attn(q, k, v, cache_k, cache_v, layer, eb):
    bsz, seq, w = q.shape
    rows = seq // GRID_W
    nblk = rows // NA_QROWS
    tq = NA_QROWS * GRID_W
    past = cache_k.shape[2]
    full = pl.BlockSpec((1, seq, w), lambda b, r: (b, 0, 0))
    ctx = pl.BlockSpec((None, None, past, w), lambda b, r: (b, layer, 0, 0))

    def eb_map(b, r):
        return (jnp.where(r == 0, 0, jnp.where(r == nblk - 1, 2, 1)), 0, 0, 0)

    return pl.pallas_call(
        functools.partial(_na_kernel, rows=rows),
        grid=(bsz, nblk),
        in_specs=[pl.BlockSpec((1, tq, w), lambda b, r: (b, r, 0)), full, full, ctx, ctx,
                  pl.BlockSpec((1, NA_HEADS, tq, NA_WIN_ROWS * GRID_W), eb_map)],
        out_specs=pl.BlockSpec((1, tq, w), lambda b, r: (b, r, 0)),
        out_shape=jax.ShapeDtypeStruct((bsz, seq, w), BF16),
        compiler_params=_cparams(2),
        name="neighbourhood_attention",
    )(q, k, v, cache_k, cache_v, eb)


_DN_HALO = 16
_DN_RSUB = 128
_DN_CGRP = 2 * LANES


def _dnprep_kernel(*refs, tt, nt, use_rope):
    if use_rope:
        main_ref, prev_ref, next_ref, w_ref, cos_ref, sin_ref, q_ref, k_ref, v_ref = refs
    else:
        main_ref, prev_ref, next_ref, w_ref, q_ref, k_ref, v_ref = refs
    t = pl.program_id(1)
    main = main_ref[0]
    halo = jnp.concatenate([jnp.where(t > 0, prev_ref[0], jnp.zeros_like(prev_ref[0])),
                            jnp.where(t < nt - 1, next_ref[0], jnp.zeros_like(next_ref[0]))], axis=0)
    pad = (DN_CONV_K - 1) // 2
    row = lax.broadcasted_iota(jnp.int32, (tt, tt), 0)
    col = lax.broadcasted_iota(jnp.int32, (tt, tt), 1)
    hrow = lax.broadcasted_iota(jnp.int32, (tt, 2 * _DN_HALO), 0)
    hcol = lax.broadcasted_iota(jnp.int32, (tt, 2 * _DN_HALO), 1)
    htok = jnp.where(hcol < _DN_HALO, hcol - _DN_HALO, tt + hcol - _DN_HALO)
    sel = {}
    for kk in range(DN_CONV_K):
        if kk != pad:
            sel[kk] = (jnp.where(col == row + (kk - pad), 1.0, 0.0).astype(BF16),
                       jnp.where(htok == hrow + (kk - pad), 1.0, 0.0).astype(BF16))
    outs = (q_ref, k_ref, v_ref)
    if use_rope:
        lane = lax.broadcasted_iota(jnp.int32, (_DN_RSUB, LANES), 1)
        first_half = (lane & (DN_DK // 4)) == 0
    for cg in range(3 * DN_QK_W // _DN_CGRP):
        gcols = slice(cg * _DN_CGRP, (cg + 1) * _DN_CGRP)
        taps = {}
        for kk in range(DN_CONV_K):
            if kk == pad:
                taps[kk] = main[:, gcols].astype(F32)
            else:
                taps[kk] = (jnp.dot(sel[kk][0], main[:, gcols], preferred_element_type=F32)
                            + jnp.dot(sel[kk][1], halo[:, gcols], preferred_element_type=F32))
        for sub in range(_DN_CGRP // LANES):
            cb = cg * (_DN_CGRP // LANES) + sub
            cols = slice(cb * LANES, (cb + 1) * LANES)
            lcols = slice(sub * LANES, (sub + 1) * LANES)
            for rs in range(tt // _DN_RSUB):
                rsl = slice(rs * _DN_RSUB, (rs + 1) * _DN_RSUB)
                acc = jnp.zeros((_DN_RSUB, LANES), F32)
                for kk in range(DN_CONV_K):
                    acc = acc + w_ref[kk:kk + 1, cols] * taps[kk][rsl, lcols]
                y = _silu(acc)
                if cb < 2 * DN_HEADS:
                    y = y * lax.rsqrt(jnp.sum(y * y, axis=-1, keepdims=True) + EPS)
                    if use_rope:
                        quarter = DN_DK // 4
                        partner = jnp.where(first_half, pltpu.roll(y, LANES - quarter, 1), pltpu.roll(y, quarter, 1))
                        y = y * cos_ref[rsl, :] + partner * sin_ref[rsl, :]
                    if cb < DN_HEADS:
                        y = y * (DN_DK ** -0.5)
                hh = cb % DN_HEADS
                outs[cb // DN_HEADS][0, rsl, hh * LANES:(hh + 1) * LANES] = y


def _rope_lane_tables(seq):
    pos = np.arange(seq)
    n_freq = DN_DK // 4
    inv = ROPE_BASE ** (-np.arange(n_freq) / n_freq)
    ang = np.stack([(pos // GRID_W)[:, None] * inv[None, :], (pos % GRID_W)[:, None] * inv[None, :]], axis=1)
    cos = np.cos(ang).astype(np.float32)
    sin = np.sin(ang).astype(np.float32)
    lane = np.arange(DN_DK)
    axis, half, freq = lane // (2 * n_freq), (lane // n_freq) % 2, lane % n_freq
    cos_l = cos[:, axis, freq]
    sin_l = sin[:, axis, freq] * np.where(half == 0, -1.0, 1.0).astype(np.float32)[None, :]
    return jnp.asarray(cos_l), jnp.asarray(sin_l)


def _dnprep(dqkv, conv_w, use_rope):
    bsz, seq, width = dqkv.shape
    tt = ROW_TILE
    nt = seq // tt
    prev, nxt = _halo_specs(tt, seq, _DN_HALO, width)
    in_specs = [pl.BlockSpec((1, tt, width), lambda b, t: (b, t, 0)), prev, nxt,
                pl.BlockSpec((DN_CONV_K, width), lambda b, t: (0, 0))]
    args = [dqkv, dqkv, dqkv, conv_w]
    if use_rope:
        tab = pl.BlockSpec((tt, DN_DK), lambda b, t: (t, 0))
        in_specs += [tab, tab]
        args += list(_rope_lane_tables(seq))
    out = jax.ShapeDtypeStruct((bsz, seq, DN_QK_W), F32)
    ospec = pl.BlockSpec((1, tt, DN_QK_W), lambda b, t: (b, t, 0))
    return pl.pallas_call(
        functools.partial(_dnprep_kernel, tt=tt, nt=nt, use_rope=use_rope),
        grid=(bsz, nt),
        in_specs=in_specs,
        out_specs=[ospec, ospec, ospec],
        out_shape=[out, out, out],
        compiler_params=_cparams(2),
        name="deltanet_prep",
    )(*args)


_LOG_CHUNK = int(math.log2(DN_CHUNK))
_N_GATES = 4 * DN_HEADS


def _bmm(a, b):
    return lax.dot_general(a.astype(BF16), b.astype(BF16), (((2,), (1,)), ((0,), (0,))),
                           preferred_element_type=F32)


def _bmm_nt(a, b):
    return lax.dot_general(a.astype(BF16), b.astype(BF16), (((2,), (2,)), ((0,), (0,))),
                           preferred_element_type=F32)


def _bmm_tn(a, b):
    return lax.dot_general(a.astype(BF16), b.astype(BF16), (((1,), (1,)), ((0,), (0,))),
                           preferred_element_type=F32)


def _pair_cols(col_a, col_b, width):
    lane = lax.broadcasted_iota(jnp.int32, (col_a.shape[0], width), 1)
    return jnp.where(lane < width // 2, col_a, col_b)


def _block_diag(y):
    lane = lax.broadcasted_iota(jnp.int32, y.shape, 2)
    first = lane < y.shape[2] // 2
    return jnp.concatenate([jnp.where(first, y, 0.0), jnp.where(first, 0.0, y)], axis=1)


def _chunk_cumsum(x, axis, reverse):
    n = x.shape[axis]
    pos = lax.broadcasted_iota(jnp.int32, x.shape, axis) & (DN_CHUNK - 1)
    step = 1
    while step < DN_CHUNK:
        if reverse:
            x = x + jnp.where(pos < DN_CHUNK - step, pltpu.roll(x, n - step, axis), 0.0)
        else:
            x = x + jnp.where(pos >= step, pltpu.roll(x, step, axis), 0.0)
        step *= 2
    return x


def _dn_tile_pair(fwd, bwd, par, of_ref, ob_ref, s_scr):
    alog_ref, dtb_ref, alogt_ref, dtbt_ref = par
    cs = DN_CHUNK
    pw = 2 * cs
    hp_n = DN_HEADS // 2
    tile = of_ref.shape[1]
    n_chunks = tile // cs
    npd = n_chunks * hp_n
    ii = lax.broadcasted_iota(jnp.int32, (cs, pw), 0)
    jj = lax.broadcasted_iota(jnp.int32, (cs, pw), 1) & (cs - 1)
    sx = jnp.stack([jnp.where(ii >= jj, ii ^ jj, -1), jnp.where(ii <= jj, ii ^ jj, -1)], axis=0)[:, None]
    incl = sx >= 0
    before = sx > 0

    gc_l, gct_l, beta_l, gc2_l, beta2_l, tot2_l, q_l, k_l, v_l = ([] for _ in range(9))
    for d, (q_ref, k_ref, v_ref, ab_ref, abt_ref) in enumerate((fwd, bwd)):
        ab = ab_ref[0]
        abt = abt_ref[0]
        g_col = -jnp.exp(alog_ref[...]) * _softplus(ab + dtb_ref[...])
        g_row = -jnp.exp(alogt_ref[...]) * _softplus(abt + dtbt_ref[...])
        beta_all = _sigmoid(ab)
        gc_rows = jnp.concatenate([_chunk_cumsum(g_row[:, i * LANES:(i + 1) * LANES], 1, d == 1)
                                   for i in range(tile // LANES)], axis=1)
        qx = q_ref[0]
        kx = k_ref[0]
        vx = v_ref[0]
        for c in range(n_chunks):
            rows = slice(c * cs, (c + 1) * cs)
            gc_c = _chunk_cumsum(g_col[rows], 0, d == 1)
            gct_c = gc_rows[:, rows]
            last = 0 if d == 1 else cs - 1
            tot_c = gc_c[last:last + 1, :]
            for hp in range(hp_n):
                ga = d * DN_HEADS + 2 * hp
                gb = ga + 1
                ba = ga + 2 * DN_HEADS
                bb = ba + 1
                gc_l.append(_pair_cols(gc_c[:, ga:ga + 1], gc_c[:, gb:gb + 1], pw))
                gct_l.append(jnp.concatenate([gct_c[ga:ga + 1, :], gct_c[gb:gb + 1, :]], axis=1))
                beta_l.append(_pair_cols(beta_all[rows, ba:ba + 1], beta_all[rows, bb:bb + 1], pw))
                gc2_l.append(_pair_cols(gc_c[:, ga:ga + 1], gc_c[:, gb:gb + 1], 2 * LANES))
                beta2_l.append(_pair_cols(beta_all[rows, ba:ba + 1], beta_all[rows, bb:bb + 1], 2 * LANES))
                tot2_l.append(_pair_cols(tot_c[:, ga:ga + 1], tot_c[:, gb:gb + 1], 2 * LANES))
                cols = slice(hp * 2 * LANES, (hp + 1) * 2 * LANES)
                q_l.append(qx[rows, cols])
                k_l.append(kx[rows, cols])
                v_l.append(vx[rows, cols])

    def stk(xs):
        return jnp.stack(xs, axis=0)

    gc, gc_t, beta = stk(gc_l), stk(gct_l), stk(beta_l)
    gc2, beta2, tot2 = stk(gc2_l), stk(beta2_l), stk(tot2_l)
    q, k, v = stk(q_l), stk(k_l), stk(v_l)

    def by_dir(x):
        return x.reshape((2, npd) + x.shape[1:])

    def flat(x):
        return x.reshape((2 * npd,) + x.shape[2:])

    k_bd = _block_diag(k)
    kk = by_dir(_bmm_nt(k, k_bd))
    qk = by_dir(_bmm_nt(q, k_bd))
    decay = jnp.where(incl, jnp.exp(jnp.where(incl, by_dir(gc - gc_t), 0.0)), 0.0)
    a_mat = jnp.where(before, (by_dir(beta) * kk) * decay, 0.0)
    a_qk = flat(qk * decay)

    e_mat = flat(-jnp.where(sx == 1, a_mat, 0.0))
    for lb in range(1, _LOG_CHUNK):
        l_b = flat(jnp.where(lax.shift_right_arithmetic(sx, lb) == 1, a_mat, 0.0))
        p_mat = l_b + _bmm(l_b, _block_diag(e_mat))
        e_mat = e_mat - p_mat - _bmm(e_mat, _block_diag(p_mat))

    eg2 = jnp.exp(gc2)
    vb = v * beta2
    kb = k * (beta2 * eg2)
    zero2 = jnp.zeros(vb.shape, F32)
    rhs = jnp.concatenate([vb[..., :LANES], kb[..., :LANES], vb[..., LANES:], kb[..., LANES:]], axis=-1)
    rhs_bd = jnp.concatenate([jnp.concatenate([vb[..., :LANES], kb[..., :LANES], zero2], axis=-1),
                              jnp.concatenate([zero2, vb[..., LANES:], kb[..., LANES:]], axis=-1)], axis=1)
    sol = rhs + _bmm(e_mat, rhs_bd)
    q_dec = q * eg2
    k_dec = k * jnp.exp(tot2 - gc2)
    c_dec2 = jnp.exp(tot2)

    def heads(x, chunks, off, stride):
        out = []
        for d in range(2):
            for hp in range(hp_n):
                p = d * npd + chunks[d] * hp_n + hp
                out += [x[p][:, off:off + LANES], x[p][:, off + stride:off + stride + LANES]]
        return jnp.stack(out, axis=0)

    s = s_scr[...].reshape(2 * DN_HEADS, DN_DK, DN_DV)
    zero_u = jnp.zeros((cs, LANES), F32)
    for step in range(n_chunks):
        chunks = (step, n_chunks - 1 - step)
        u_base = heads(sol, chunks, 0, 2 * LANES)
        w_dec = heads(sol, chunks, LANES, 2 * LANES)
        qd = heads(q_dec, chunks, 0, LANES)
        kd = heads(k_dec, chunks, 0, LANES)
        cd = heads(c_dec2, chunks, 0, LANES)[:, :, :1]
        ws_qs = _bmm(jnp.concatenate([w_dec, qd], axis=1), s)
        u = u_base - ws_qs[:, :cs]
        u_bd = jnp.stack([jnp.concatenate([jnp.concatenate([u[2 * i], zero_u], axis=-1),
                                           jnp.concatenate([zero_u, u[2 * i + 1]], axis=-1)], axis=0)
                          for i in range(DN_HEADS)], axis=0)
        aqk_sel = jnp.stack([a_qk[d * npd + chunks[d] * hp_n + hp] for d in range(2) for hp in range(hp_n)], axis=0)
        o_loc = _bmm(aqk_sel, u_bd)
        s = s * cd + _bmm_tn(kd, u)
        for d, o_ref in enumerate((of_ref, ob_ref)):
            c = chunks[d]
            for h in range(DN_HEADS):
                i = d * DN_HEADS + h
                o_ref[0, c * cs:(c + 1) * cs, h * LANES:(h + 1) * LANES] = (
                    ws_qs[i, cs:] + o_loc[i // 2][:, (h % 2) * LANES:(h % 2 + 1) * LANES]).astype(o_ref.dtype)
    s_scr[...] = s.reshape(2, DN_HEADS, DN_DK, DN_DV)


def _dnscan_kernel(*refs, nt, has_s0, want_state):
    it = iter(refs)
    fwd = [next(it) for _ in range(5)]
    bwd = [next(it) for _ in range(5)]
    par = [next(it) for _ in range(4)]
    s0_ref = next(it) if has_s0 else None
    of_ref = next(it)
    ob_ref = next(it)
    st_ref = next(it) if want_state else None
    s_scr = next(it)
    t = pl.program_id(1)

    @pl.when(t == 0)
    def _():
        if has_s0:
            s_scr[...] = s0_ref[0]
        else:
            s_scr[...] = jnp.zeros(s_scr.shape, F32)

    _dn_tile_pair(fwd, bwd, par, of_ref, ob_ref, s_scr)

    if want_state:
        @pl.when(t == nt - 1)
        def _():
            st_ref[0] = s_scr[...]


def _dnscan(q, k, v, ab, abt, a_log, dt_bias, s0, want_state):
    bsz, seq, _ = q.shape
    tile = min(DN_TILE, seq)
    nt = seq // tile

    def tile_specs(tmap):
        qs = pl.BlockSpec((1, tile, DN_V_W), lambda b, t: (b, tmap(t), 0))
        return [qs, qs, qs,
                pl.BlockSpec((1, tile, _N_GATES), lambda b, t: (b, tmap(t), 0)),
                pl.BlockSpec((1, _N_GATES, tile), lambda b, t: (b, 0, tmap(t)))]

    zeros8 = jnp.zeros((2 * DN_HEADS,), F32)
    alog16 = jnp.concatenate([a_log.reshape(-1), zeros8])
    dtb16 = jnp.concatenate([dt_bias.reshape(-1), zeros8])
    prow = pl.BlockSpec((1, _N_GATES), lambda b, t: (0, 0))
    pcol = pl.BlockSpec((_N_GATES, 1), lambda b, t: (0, 0))
    in_specs = tile_specs(lambda t: t) + tile_specs(lambda t: nt - 1 - t) + [prow, prow, pcol, pcol]
    args = [q, k, v, ab, abt, q, k, v, ab, abt,
            alog16.reshape(1, _N_GATES), dtb16.reshape(1, _N_GATES),
            alog16.reshape(_N_GATES, 1), dtb16.reshape(_N_GATES, 1)]
    st_block = (1, 2, DN_HEADS, DN_DK, DN_DV)
    if s0 is not None:
        in_specs.append(pl.BlockSpec(st_block, lambda b, t: (b, 0, 0, 0, 0)))
        args.append(s0)
    o_shape = jax.ShapeDtypeStruct((bsz, seq, DN_V_W), BF16)
    out_specs = [pl.BlockSpec((1, tile, DN_V_W), lambda b, t: (b, t, 0)),
                 pl.BlockSpec((1, tile, DN_V_W), lambda b, t: (b, nt - 1 - t, 0))]
    out_shape = [o_shape, o_shape]
    if want_state:
        out_specs.append(pl.BlockSpec(st_block, lambda b, t: (b, 0, 0, 0, 0)))
        out_shape.append(jax.ShapeDtypeStruct((bsz, 2, DN_HEADS, DN_DK, DN_DV), F32))
    return pl.pallas_call(
        functools.partial(_dnscan_kernel, nt=nt, has_s0=s0 is not None, want_state=want_state),
        grid=(bsz, nt),
        in_specs=in_specs,
        out_specs=out_specs,
        out_shape=out_shape,
        scratch_shapes=[pltpu.VMEM((2, DN_HEADS, DN_DK, DN_DV), F32)],
        compiler_params=_cparams(2),
        name="deltanet_scan",
    )(*args)


def _first_argmax(x, lane, valid):
    xm = jnp.where(valid, x, -jnp.inf)
    m = jnp.max(xm, axis=-1, keepdims=True)
    idx = jnp.min(jnp.where(valid & (xm == m), lane, float(LANES)), axis=-1, keepdims=True)
    return m, idx


def _route(h, rw_hi_ref, rw_lo_ref, rb_ref):
    h_hi = h.astype(BF16)
    h_lo = (h - h_hi.astype(F32)).astype(BF16)
    logits = (jnp.dot(h_hi, rw_hi_ref[...], preferred_element_type=F32)
              + jnp.dot(h_lo, rw_hi_ref[...], preferred_element_type=F32)
              + jnp.dot(h_hi, rw_lo_ref[...], preferred_element_type=F32)) + rb_ref[...]
    lane = lax.broadcasted_iota(jnp.int32, logits.shape, 1).astype(F32)
    is_grp = lane < N_GROUPS
    gmax, g_sel = _first_argmax(logits, lane, is_grp)
    pg_sel = 1.0 / jnp.sum(jnp.where(is_grp, jnp.exp(logits - gmax), 0.0), axis=-1, keepdims=True)
    e_lane = lane - N_GROUPS
    in_grp = (e_lane >= g_sel * EXPERTS_PER_GROUP) & (e_lane < (g_sel + 1) * EXPERTS_PER_GROUP)
    m1, i1 = _first_argmax(logits, lane, in_grp)
    m2, i2 = _first_argmax(logits, lane, in_grp & (lane != i1))
    e2 = jnp.exp(m2 - m1)
    w1 = pg_sel / (1.0 + e2)
    w2 = pg_sel * e2 / (1.0 + e2)
    return g_sel, jnp.where(lane == i1, w1, 0.0) + jnp.where(lane == i2, w2, 0.0)


def _mix_out(x_ref, a_ref, b_ref, of_ref, ob_ref, z_ref, ng_ref, ga_ref, w_ref):
    o = of_ref[...].astype(F32) + ob_ref[...].astype(F32)
    z = z_ref[...].astype(F32)
    acc = jnp.dot(a_ref[...], w_ref[0:CONV_W, :], preferred_element_type=F32)
    acc = acc + jnp.dot(b_ref[...], w_ref[CONV_W:CONV_W + NA_W, :], preferred_element_type=F32)
    for h in range(DN_HEADS):
        cols = slice(h * DN_DV, (h + 1) * DN_DV)
        oh = o[:, cols]
        y = oh * lax.rsqrt(jnp.mean(oh * oh, axis=-1, keepdims=True) + EPS) * ng_ref[...] * _silu(z[:, cols])
        r0 = CONV_W + NA_W + h * DN_DV
        acc = acc + jnp.dot(y.astype(BF16), w_ref[r0:r0 + DN_DV, :], preferred_element_type=F32)
    return x_ref[...] + ga_ref[0] * acc


def _moe_kernel(x_ref, a_ref, b_ref, of_ref, ob_ref, z_ref, ng_ref, ga1_ref, wout_ref,
                g_ref, sc_ref, sh_ref, ga_ref, rw_hi_ref, rw_lo_ref, rb_ref, tri_ref, w1_ref, w3_ref, w2_ref,
                gf_ref, o_ref, h_scr, ghi_scr, glo_scr, oh_scr, rank_scr, oht_scr, rankt_scr, cnt_scr,
                *, final_norm):
    grp = pl.program_id(1)
    tm = x_ref.shape[0]

    @pl.when(grp == 0)
    def _():
        x_new = _mix_out(x_ref, a_ref, b_ref, of_ref, ob_ref, z_ref, ng_ref, ga1_ref, wout_ref)
        o_ref[...] = x_new
        h = _mod_norm(x_new, g_ref[...], sc_ref[0], sh_ref[0])
        h_scr[...] = h.astype(BF16)
        g_sel, gates = _route(h, rw_hi_ref, rw_lo_ref, rb_ref)
        g_hi = gates.astype(BF16)
        ghi_scr[...] = g_hi
        glo_scr[...] = (gates - g_hi.astype(F32)).astype(BF16)
        lane = lax.broadcasted_iota(jnp.int32, (tm, LANES), 1).astype(F32)
        one_hot = jnp.where(lane == g_sel, 1.0, 0.0)
        oh_b = one_hot.astype(BF16)
        oh_scr[...] = one_hot
        rank_scr[...] = jnp.dot(tri_ref[...], oh_b, preferred_element_type=F32)
        pick = jnp.where(lax.broadcasted_iota(jnp.int32, (SUBLANES, LANES), 0)
                         == lax.broadcasted_iota(jnp.int32, (SUBLANES, LANES), 1), 1.0, 0.0).astype(BF16)
        oh_t = lax.dot_general(pick, oh_b, _NT, preferred_element_type=F32)
        oht_scr[...] = oh_t
        rankt_scr[...] = lax.dot_general(oh_t.astype(BF16), tri_ref[...], _NT, preferred_element_type=F32)
        for k in range(N_GROUPS):
            cnt_scr[k] = jnp.sum(one_hot[:, k:k + 1]).astype(jnp.int32)

    hb = h_scr[...]
    lane = lax.broadcasted_iota(jnp.int32, (tm, LANES), 1)
    in_col = jnp.sum(jnp.where(lane == grp, oh_scr[...], 0.0), axis=-1, keepdims=True)
    rank_col = jnp.sum(jnp.where(lane == grp, rank_scr[...], 0.0), axis=-1, keepdims=True)
    in_row = oht_scr[pl.ds(grp, 1), :]
    rank_row = rankt_scr[pl.ds(grp, 1), :]
    def chunk(first_rank, n_rows):
        slot_r = lax.broadcasted_iota(jnp.int32, (n_rows, tm), 0).astype(F32)
        slot_c = lax.broadcasted_iota(jnp.int32, (tm, n_rows), 1).astype(F32)
        glane = lax.broadcasted_iota(jnp.int32, (n_rows, LANES), 1)
        base = first_rank.astype(F32)
        take = jnp.where((in_row > 0.5) & (rank_row - base == slot_r), 1.0, 0.0).astype(BF16)
        put = jnp.where((in_col > 0.5) & (rank_col - base == slot_c), 1.0, 0.0).astype(BF16)
        xs = jnp.dot(take, hb, preferred_element_type=F32).astype(BF16)
        gates = (jnp.dot(take, ghi_scr[...], preferred_element_type=F32)
                 + jnp.dot(take, glo_scr[...], preferred_element_type=F32))
        y = None
        for half in range(EXPERTS_PER_GROUP // 2):
            hid = []
            for j in (2 * half, 2 * half + 1):
                gate_e = jnp.sum(jnp.where(glane == N_GROUPS + grp * EXPERTS_PER_GROUP + j, gates, 0.0),
                                 axis=-1, keepdims=True)
                up = _silu(jnp.dot(xs, w1_ref[0, j], preferred_element_type=F32)) * jnp.dot(
                    xs, w3_ref[0, j], preferred_element_type=F32)
                hid.append((up * gate_e).astype(BF16))
            rows = slice(2 * half * D_EXPERT, 2 * (half + 1) * D_EXPERT)
            part = jnp.dot(jnp.concatenate(hid, axis=-1), w2_ref[0, 0, rows, :], preferred_element_type=F32)
            y = part if y is None else y + part
        o_ref[...] += ga_ref[0] * jnp.dot(put, y.astype(BF16), preferred_element_type=F32)

    count = cnt_scr[grp]
    rest = lax.rem(count, MOE_ROWS)
    n_full = count // MOE_ROWS + (rest > MOE_ROWS // 2).astype(jnp.int32)

    def full_chunk(c, carry):
        chunk(c * MOE_ROWS, MOE_ROWS)
        return carry

    lax.fori_loop(0, n_full, full_chunk, 0)

    @pl.when((rest > 0) & (rest <= MOE_ROWS // 2))
    def _():
        chunk(n_full * MOE_ROWS, MOE_ROWS // 2)

    if final_norm:
        @pl.when(grp == N_GROUPS - 1)
        def _():
            y = o_ref[...]
            o_ref[...] = y * lax.rsqrt(jnp.mean(y * y, axis=-1, keepdims=True) + EPS) * gf_ref[...]


def _mix_moe(x, out_a, out_b, o_f, o_b, z, norm_g, gate1, w_out_b,
             g, scale, shift, gate, rw_hi, rw_lo, rb, w1_b, w3_b, w2_b, g_final, layer, final_norm):
    bsz, seq, d = x.shape
    n = bsz * seq
    x2 = x.reshape(n, d)
    flat = [a.reshape(n, a.shape[-1]) for a in (out_a, out_b, o_f, o_b, z)]
    if scale.shape[0] == 1:
        tm = MOE_TILE
        mm = lambda i, e: (0, 0, 0)
    else:
        tm = min(MOE_TILE, seq)
        per_seq = seq // tm
        mm = lambda i, e: (i // per_seq, 0, 0)
    tri = jnp.asarray(np.tril(np.ones((tm, tm), np.float32), -1), BF16)
    vec = pl.BlockSpec((1, d), lambda i, e: (0, 0))
    mod = pl.BlockSpec((1, 1, d), mm)
    rspec = pl.BlockSpec((d, LANES), lambda i, e: (0, 0))
    gw = EXPERTS_PER_GROUP * D_EXPERT
    out = pl.pallas_call(
        functools.partial(_moe_kernel, final_norm=final_norm),
        grid=(bsz * seq // tm, N_GROUPS),
        in_specs=[pl.BlockSpec((tm, d), lambda i, e: (i, 0))]
                 + [pl.BlockSpec((tm, a.shape[-1]), lambda i, e: (i, 0)) for a in flat]
                 + [pl.BlockSpec((1, DN_DV), lambda i, e: (0, 0)), mod,
                    pl.BlockSpec((None, MIX_W, d), lambda i, e: (layer, 0, 0)),
                    vec, mod, mod, mod,
                  rspec, rspec, pl.BlockSpec((1, LANES), lambda i, e: (0, 0)),
                  pl.BlockSpec((tm, tm), lambda i, e: (0, 0)),
                  pl.BlockSpec((None, 1, EXPERTS_PER_GROUP, d, D_EXPERT), lambda i, e: (layer, e, 0, 0, 0)),
                  pl.BlockSpec((None, 1, EXPERTS_PER_GROUP, d, D_EXPERT), lambda i, e: (layer, e, 0, 0, 0)),
                  pl.BlockSpec((1, 1, gw, d), lambda i, e: (layer, e, 0, 0)),
                  vec],
        out_specs=pl.BlockSpec((tm, d), lambda i, e: (i, 0)),
        out_shape=jax.ShapeDtypeStruct((bsz * seq, d), F32),
        scratch_shapes=[pltpu.VMEM((tm, d), BF16), pltpu.VMEM((tm, LANES), BF16), pltpu.VMEM((tm, LANES), BF16),
                        pltpu.VMEM((tm, LANES), F32), pltpu.VMEM((tm, LANES), F32),
                        pltpu.VMEM((SUBLANES, tm), F32), pltpu.VMEM((SUBLANES, tm), F32),
                        pltpu.SMEM((N_GROUPS,), jnp.int32)],
        compiler_params=pltpu.CompilerParams(dimension_semantics=("arbitrary", "arbitrary"),
                                             vmem_limit_bytes=MOE_VMEM_LIMIT),
        name="moe",
    )(x2, *flat, norm_g.reshape(1, -1), gate1, w_out_b,
      g, scale, shift, gate, rw_hi, rw_lo, rb, tri, w1_b, w3_b, w2_b, g_final.reshape(1, -1))
    return out.reshape(bsz, seq, d)


def _split_hi_lo(w):
    hi = w.astype(BF16)
    return hi, (w - hi.astype(F32)).astype(BF16)


def kernel(x_prompt, x_sample, cache_na_k, cache_na_v, state_delta, c, c_ctx, w_mod, b_mod, g_norm1, g_norm2,
           w_in, conv_a_w, conv_a_b, ln_a_g, ln_a_b, na_rel_bias, dn_conv_w, dn_a_log, dn_dt_bias, dn_norm_g,
           w_out, router_wg, router_bg, router_we, router_be, w1, w3, w2, g_final):
    n_dec = x_sample.shape[0]
    d = D_MODEL
    cond8 = jnp.concatenate([c_ctx[None, :], c, jnp.zeros((8 - 1 - n_dec, d), F32)], axis=0)
    mods = _adaln(cond8, w_mod, b_mod).reshape(DEPTH, 8, 6, d)

    w_in_b = w_in.astype(BF16)
    w_abt_b = jnp.swapaxes(w_in[:, :, OFF_DN_AB:], 1, 2).astype(BF16)
    w_out_b = w_out.astype(BF16)
    w1_b = w1.astype(BF16).reshape(DEPTH, N_GROUPS, EXPERTS_PER_GROUP, d, D_EXPERT)
    w3_b = w3.astype(BF16).reshape(DEPTH, N_GROUPS, EXPERTS_PER_GROUP, d, D_EXPERT)
    w2_b = w2.astype(BF16).reshape(DEPTH, N_GROUPS, EXPERTS_PER_GROUP * D_EXPERT, d)
    pad = LANES - N_GROUPS - N_EXPERTS
    rw = jnp.concatenate([router_wg, router_we, jnp.zeros((DEPTH, d, pad), F32)], axis=-1)
    rw_hi, rw_lo = _split_hi_lo(rw)
    rb = jnp.concatenate([router_bg, router_be, jnp.zeros((DEPTH, pad), F32)], axis=-1)

    past = cache_na_k.shape[2]
    cache_k = cache_na_k.reshape(n_dec, DEPTH, past, NA_W)
    cache_v = cache_na_v.reshape(n_dec, DEPTH, past, NA_W)
    rows = x_sample.shape[1] // GRID_W

    def layer(x, l, is_ctx):
        m = mods[l, 0:1] if is_ctx else mods[l, 1:1 + n_dec]
        sh1, sc1, ga1, sh2, sc2, ga2 = (m[:, i:i + 1, :] for i in range(6))
        glu, q_na, k_na, v_na, dqkv, dz, ab, abt = _inproj(x, g_norm1[l:l + 1], sc1, sh1, w_in_b, w_abt_b, l)
        out_a = _cconv(glu, conv_a_w[l], conv_a_b[l], ln_a_g[l], ln_a_b[l])
        if is_ctx:
            out_b = _ctx_attn(q_na, k_na, v_na)
        else:
            out_b = _na_attn(q_na, k_na, v_na, cache_k, cache_v, l, _na_bias_tables(na_rel_bias[l], rows))
        dq, dk, dv = _dnprep(dqkv, dn_conv_w[l], use_rope=not is_ctx)
        scan = _dnscan(dq, dk, dv, ab, abt, dn_a_log[l], dn_dt_bias[l],
                       None if is_ctx else state_delta[:, l], want_state=is_ctx)
        o_f, o_b = scan[0], scan[1]
        x = _mix_moe(x, out_a, out_b, o_f, o_b, dz, dn_norm_g[l], ga1, w_out_b,
                     g_norm2[l:l + 1], sc2, sh2, ga2, rw_hi[l], rw_lo[l], rb[l:l + 1], w1_b, w3_b, w2_b,
                     g_final, layer=l, final_norm=(l == DEPTH - 1))
        return x, k_na, v_na, (scan[2] if is_ctx else None)

    xc = x_prompt
    new_k, new_v, new_s = [], [], []
    for l in range(DEPTH):
        xc, k_na, v_na, s_ctx = layer(xc, l, True)
        new_k.append(k_na.reshape(k_na.shape[0], k_na.shape[1], NA_HEADS, NA_HEAD_DIM))
        new_v.append(v_na.reshape(v_na.shape[0], v_na.shape[1], NA_HEADS, NA_HEAD_DIM))
        new_s.append(s_ctx)
    xs = x_sample
    for l in range(DEPTH):
        xs, _, _, _ = layer(xs, l, False)
    return (xc, xs, jnp.stack(new_k, axis=1), jnp.stack(new_v, axis=1), jnp.stack(new_s, axis=1))
```

```python
import functools
import math

import numpy as np
import jax
import jax.numpy as jnp
from jax import lax
from jax.experimental import pallas as pl
from jax.experimental.pallas import tpu as pltpu

F32 = jnp.float32
BF16 = jnp.bfloat16

D_MODEL = 1024
DEPTH = 2
GRID_W = 64
CONV_W = 256
CONV_K = 31
NA_HEADS = 4
NA_HEAD_DIM = 64
NA_W = NA_HEADS * NA_HEAD_DIM
NA_ROWS = 8
NA_COLS = 16
DN_HEADS = 4
DN_DK = 128
DN_DV = 128
DN_QK_W = DN_HEADS * DN_DK
DN_V_W = DN_HEADS * DN_DV
DN_CONV_K = 5
DN_CHUNK = 64
ROPE_BASE = 10000.0
MIX_W = CONV_W + NA_W + DN_V_W
OFF_NA = 2 * CONV_W
OFF_DN_QKV = OFF_NA + 3 * NA_W
OFF_DN_Z = OFF_DN_QKV + 2 * DN_QK_W + DN_V_W
OFF_DN_AB = OFF_DN_Z + DN_V_W
PROJ_W = OFF_DN_AB + 4 * DN_HEADS
N_GROUPS = 4
EXPERTS_PER_GROUP = 4
N_EXPERTS = N_GROUPS * EXPERTS_PER_GROUP
D_EXPERT = 256
EPS = 1e-6
NEG_INF = -1e30

LANES = 128
SUBLANES = 8
ROW_TILE = 256
CONV_TILE = 512
PROJ_TILE = 512
DN_TILE = 512
NA_QROWS = 4
NA_WIN_ROWS = 12
MOE_TILE = 1024
MOE_ROWS = 256
VMEM_LIMIT = 48 * 1024 * 1024
MOE_VMEM_LIMIT = 56 * 1024 * 1024

_NT = (((1,), (1,)), ((), ()))


def _cparams(n_axes):
    return pltpu.CompilerParams(dimension_semantics=("arbitrary",) * n_axes,
                                vmem_limit_bytes=VMEM_LIMIT)


def _sigmoid(x):
    return 1.0 / (1.0 + jnp.exp(-x))


def _silu(x):
    return x * _sigmoid(x)


def _softplus(x):
    return jnp.maximum(x, 0.0) + jnp.log(1.0 + jnp.exp(-jnp.abs(x)))


def _dot(a, b):
    return jnp.dot(a.astype(BF16), b.astype(BF16), preferred_element_type=F32)


def _adaln_kernel(c_ref, w_ref, b_ref, o_ref):
    c = c_ref[...]
    o_ref[0] = _dot(_silu(c), w_ref[0]) + b_ref[0]


def _adaln(cond8, w_mod, b_mod):
    n_l, d, n6 = w_mod.shape
    tn = 1536
    return pl.pallas_call(
        _adaln_kernel,
        grid=(n_l, n6 // tn),
        in_specs=[pl.BlockSpec((8, d), lambda l, j: (0, 0)),
                  pl.BlockSpec((1, d, tn), lambda l, j: (l, 0, j)),
                  pl.BlockSpec((1, 1, tn), lambda l, j: (l, 0, j))],
        out_specs=pl.BlockSpec((1, 8, tn), lambda l, j: (l, 0, j)),
        out_shape=jax.ShapeDtypeStruct((n_l, 8, n6), F32),
        compiler_params=_cparams(2),
        name="adaln",
    )(cond8, w_mod, b_mod.reshape(n_l, 1, n6))


def _mod_norm(x, g, scale, shift):
    ms = jnp.mean(x * x, axis=-1, keepdims=True)
    return (x * lax.rsqrt(ms + EPS) * g) * (1.0 + scale) + shift


def _inproj_kernel(x_ref, g_ref, sc_ref, sh_ref, w_ref, wabt_ref,
                   glu_ref, q_ref, k_ref, v_ref, dqkv_ref, z_ref, ab_ref, abt_ref):
    hb = _mod_norm(x_ref[0], g_ref[...], sc_ref[0], sh_ref[0]).astype(BF16)

    def proj(a, b):
        return jnp.dot(hb, w_ref[:, a:b], preferred_element_type=F32)

    glu_ref[0] = proj(0, OFF_NA)
    q_ref[0] = proj(OFF_NA, OFF_NA + NA_W).astype(BF16)
    k_ref[0] = proj(OFF_NA + NA_W, OFF_NA + 2 * NA_W)
    v_ref[0] = proj(OFF_NA + 2 * NA_W, OFF_DN_QKV)
    dqkv_ref[0] = proj(OFF_DN_QKV, OFF_DN_Z).astype(BF16)
    z_ref[0] = proj(OFF_DN_Z, OFF_DN_AB).astype(BF16)
    abt = lax.dot_general(wabt_ref[...], hb, _NT, preferred_element_type=F32)
    abt_ref[0] = abt
    ab_ref[0] = abt.T


def _bmap(bm):
    if bm == 1:
        return lambda b, t: (0, 0, 0)
    return lambda b, t: (b, 0, 0)


def _merge_short_sequences(x, tm, shared_mod):
    bsz, seq, d = x.shape
    if shared_mod and seq < tm:
        return x.reshape(bsz * seq // tm, tm, d)
    return x


def _inproj(x, g, scale, shift, w_b, wabt_b, layer):
    bsz0, seq0, _ = x.shape
    tm = PROJ_TILE
    x = _merge_short_sequences(x, tm, scale.shape[0] == 1)
    outs = _inproj_call(x, g, scale, shift, w_b, wabt_b, layer, tm)
    if x.shape[0] == bsz0:
        return outs
    per = tm // seq0
    row_outs = [o.reshape(bsz0, seq0, o.shape[-1]) for o in outs[:-1]]
    abt = outs[-1].reshape(x.shape[0], 4 * DN_HEADS, per, seq0)
    abt = jnp.swapaxes(abt, 1, 2).reshape(bsz0, 4 * DN_HEADS, seq0)
    return row_outs + [abt]


def _inproj_call(x, g, scale, shift, w_b, wabt_b, layer, tm):
    bsz, seq, d = x.shape
    widths = (OFF_NA, NA_W, NA_W, NA_W, OFF_DN_Z - OFF_DN_QKV, DN_V_W, 4 * DN_HEADS)
    dtypes = (F32, BF16, F32, F32, BF16, BF16, F32)
    out_shape = [jax.ShapeDtypeStruct((bsz, seq, w), dt) for w, dt in zip(widths, dtypes)]
    out_shape.append(jax.ShapeDtypeStruct((bsz, 4 * DN_HEADS, seq), F32))
    out_specs = [pl.BlockSpec((1, tm, w), lambda b, t: (b, t, 0)) for w in widths]
    out_specs.append(pl.BlockSpec((1, 4 * DN_HEADS, tm), lambda b, t: (b, 0, t)))
    mm = _bmap(scale.shape[0])
    return pl.pallas_call(
        _inproj_kernel,
        grid=(bsz, seq // tm),
        in_specs=[pl.BlockSpec((1, tm, d), lambda b, t: (b, t, 0)),
                  pl.BlockSpec((1, d), lambda b, t: (0, 0)),
                  pl.BlockSpec((1, 1, d), mm),
                  pl.BlockSpec((1, 1, d), mm),
                  pl.BlockSpec((None, d, OFF_DN_AB), lambda b, t: (layer, 0, 0)),
                  pl.BlockSpec((None, 4 * DN_HEADS, d), lambda b, t: (layer, 0, 0))],
        out_specs=out_specs,
        out_shape=out_shape,
        compiler_params=_cparams(2),
        name="inproj",
    )(x, g, scale, shift, w_b, wabt_b)


_CC_HALO = 16
_CC_SUB = 64


def _cconv_kernel(main_ref, prev_ref, next_ref, w_ref, b_ref, g_ref, beta_ref, o_ref, scr, *, tt, nt):
    t = pl.program_id(1)

    def glu(a):
        return a[:, :CONV_W] * _sigmoid(a[:, CONV_W:])

    scr[0, 0:_CC_HALO, :] = jnp.where(t > 0, glu(prev_ref[0]), 0.0)
    scr[0, _CC_HALO:_CC_HALO + tt, :] = glu(main_ref[0])
    scr[0, _CC_HALO + tt:2 * _CC_HALO + tt, :] = jnp.where(t < nt - 1, glu(next_ref[0]), 0.0)
    n_shift = tt + 2 * _CC_HALO - SUBLANES
    for r in range(1, SUBLANES):
        scr[r, 0:n_shift, :] = scr[0, r:r + n_shift, :]
    pad = (CONV_K - 1) // 2
    for s in range(tt // _CC_SUB):
        base = s * _CC_SUB + _CC_HALO - pad
        acc = jnp.zeros((_CC_SUB, CONV_W), F32) + b_ref[...]
        for k in range(CONV_K):
            r = (base + k) % SUBLANES
            a0 = base + k - r
            acc = acc + w_ref[k:k + 1, :] * scr[r, a0:a0 + _CC_SUB, :]
        mu = jnp.mean(acc, axis=-1, keepdims=True)
        xc = acc - mu
        var = jnp.mean(xc * xc, axis=-1, keepdims=True)
        y = xc * lax.rsqrt(var + EPS) * g_ref[...] + beta_ref[...]
        o_ref[0, s * _CC_SUB:(s + 1) * _CC_SUB, :] = _silu(y).astype(BF16)


def _halo_specs(tt, seq, halo, width):
    per = tt // halo
    last = seq // halo - 1
    prev = pl.BlockSpec((1, halo, width), lambda b, t: (b, jnp.maximum(t * per - 1, 0), 0))
    nxt = pl.BlockSpec((1, halo, width), lambda b, t: (b, jnp.minimum((t + 1) * per, last), 0))
    return prev, nxt


def _cconv(glu, w, b, g, beta):
    bsz, seq, _ = glu.shape
    tt = min(CONV_TILE, seq)
    nt = seq // tt
    prev, nxt = _halo_specs(tt, seq, _CC_HALO, 2 * CONV_W)
    vec = pl.BlockSpec((1, CONV_W), lambda b_, t: (0, 0))
    return pl.pallas_call(
        functools.partial(_cconv_kernel, tt=tt, nt=nt),
        grid=(bsz, nt),
        in_specs=[pl.BlockSpec((1, tt, 2 * CONV_W), lambda b_, t: (b_, t, 0)), prev, nxt,
                  pl.BlockSpec((CONV_K, CONV_W), lambda b_, t: (0, 0)), vec, vec, vec],
        out_specs=pl.BlockSpec((1, tt, CONV_W), lambda b_, t: (b_, t, 0)),
        out_shape=jax.ShapeDtypeStruct((bsz, seq, CONV_W), BF16),
        scratch_shapes=[pltpu.VMEM((SUBLANES, tt + 2 * _CC_HALO, CONV_W), F32)],
        compiler_params=_cparams(2),
        name="conformer_conv",
    )(glu, glu, glu, w, b.reshape(1, -1), g.reshape(1, -1), beta.reshape(1, -1))


def _head_lane_id(shape):
    return lax.shift_right_logical(lax.broadcasted_iota(jnp.int32, shape, 1), 6)


def _ctx_attn_kernel(q_ref, k_ref, v_ref, o_ref):
    q = q_ref[0].astype(F32) * (NA_HEAD_DIM ** -0.5)
    k = k_ref[0].astype(BF16)
    v = v_ref[0].astype(BF16)
    head = _head_lane_id(q.shape)
    out = jnp.zeros(q.shape, F32)
    for h in range(NA_HEADS):
        s = lax.dot_general(jnp.where(head == h, q, 0.0).astype(BF16), k, _NT, preferred_element_type=F32)
        m = jnp.max(s, axis=-1, keepdims=True)
        p = jnp.exp(s - m)
        den = jnp.sum(p, axis=-1, keepdims=True)
        oh = jnp.dot(p.astype(BF16), v, preferred_element_type=F32)
        out = jnp.where(head == h, oh / den, out)
    o_ref[0] = out.astype(BF16)


def _ctx_attn(q, k, v):
    bsz, seq, w = q.shape
    spec = pl.BlockSpec((1, seq, w), lambda b: (b, 0, 0))
    return pl.pallas_call(
        _ctx_attn_kernel,
        grid=(bsz,),
        in_specs=[spec, spec, spec],
        out_specs=spec,
        out_shape=jax.ShapeDtypeStruct((bsz, seq, w), BF16),
        compiler_params=_cparams(1),
        name="context_attention",
    )(q, k, v)


def _na_window_start(blk, rows):
    return jnp.clip(blk * NA_QROWS - NA_ROWS // 2, 0, rows - NA_WIN_ROWS)


def _na_kernel(q_ref, k_ref, v_ref, kc_ref, vc_ref, eb_ref, o_ref, *, rows):
    blk = pl.program_id(1)
    ws = pl.multiple_of(_na_window_start(blk, rows) * GRID_W, GRID_W)
    nwin = NA_WIN_ROWS * GRID_W
    q = q_ref[0].astype(F32) * (NA_HEAD_DIM ** -0.5)
    kl = k_ref[0, pl.ds(ws, nwin), :].astype(BF16)
    vl = v_ref[0, pl.ds(ws, nwin), :].astype(BF16)
    kc = kc_ref[...].astype(BF16)
    vc = vc_ref[...].astype(BF16)
    head = _head_lane_id(q.shape)
    out = jnp.zeros(q.shape, F32)
    for h in range(NA_HEADS):
        qh = jnp.where(head == h, q, 0.0).astype(BF16)
        sl = lax.dot_general(qh, kl, _NT, preferred_element_type=F32) + eb_ref[0, h]
        sc = lax.dot_general(qh, kc, _NT, preferred_element_type=F32)
        m = jnp.maximum(jnp.max(sl, axis=-1, keepdims=True), jnp.max(sc, axis=-1, keepdims=True))
        p_l = jnp.exp(sl - m)
        p_c = jnp.exp(sc - m)
        den = jnp.sum(p_l, axis=-1, keepdims=True) + jnp.sum(p_c, axis=-1, keepdims=True)
        oh = (jnp.dot(p_l.astype(BF16), vl, preferred_element_type=F32)
              + jnp.dot(p_c.astype(BF16), vc, preferred_element_type=F32))
        out = jnp.where(head == h, oh / den, out)
    o_ref[0] = out.astype(BF16)


def _na_bias_tables(rel_bias, rows):
    n_heads = rel_bias.shape[0]
    c = np.arange(GRID_W)
    cs = np.clip(c - NA_COLS // 2, 0, GRID_W - NA_COLS)
    col_valid = (c[None, :] >= cs[:, None]) & (c[None, :] < cs[:, None] + NA_COLS)
    padw = GRID_W - NA_COLS
    padded = jnp.pad(rel_bias, ((0, 0), (0, 0), (padw, padw)))
    toeplitz = jnp.stack([padded[:, :, GRID_W - 1 - ci:2 * GRID_W - 1 - ci] for ci in range(GRID_W)], axis=2)
    toeplitz = jnp.where(jnp.asarray(col_valid)[None, None], toeplitz, NEG_INF)
    masked = jnp.full((n_heads, GRID_W, GRID_W), NEG_INF, F32)
    nblk = rows // NA_QROWS
    tables = []
    for blk in (0, 1, nblk - 1):
        ws = int(np.clip(blk * NA_QROWS - NA_ROWS // 2, 0, rows - NA_WIN_ROWS))
        q_rows = []
        for a in range(NA_QROWS):
            r = blk * NA_QROWS + a
            rs = int(np.clip(r - NA_ROWS // 2, 0, rows - NA_ROWS))
            blocks = []
            for j in range(NA_WIN_ROWS):
                kr = ws + j
                blocks.append(toeplitz[:, kr - r + NA_ROWS - 1] if rs <= kr < rs + NA_ROWS else masked)
            q_rows.append(jnp.concatenate(blocks, axis=-1))
        tables.append(jnp.concatenate(q_rows, axis=1))
    return jnp.stack(tables, axis=0)


def _na_attn(q, k, v, cache_k, cache_v, layer, eb):
    bsz, seq, w = q.shape
    rows = seq // GRID_W
    nblk = rows // NA_QROWS
    tq = NA_QROWS * GRID_W
    past = cache_k.shape[2]
    full = pl.BlockSpec((1, seq, w), lambda b, r: (b, 0, 0))
    ctx = pl.BlockSpec((None, None, past, w), lambda b, r: (b, layer, 0, 0))

    def eb_map(b, r):
        return (jnp.where(r == 0, 0, jnp.where(r == nblk - 1, 2, 1)), 0, 0, 0)

    return pl.pallas_call(
        functools.partial(_na_kernel, rows=rows),
        grid=(bsz, nblk),
        in_specs=[pl.BlockSpec((1, tq, w), lambda b, r: (b, r, 0)), full, full, ctx, ctx,
                  pl.BlockSpec((1, NA_HEADS, tq, NA_WIN_ROWS * GRID_W), eb_map)],
        out_specs=pl.BlockSpec((1, tq, w), lambda b, r: (b, r, 0)),
        out_shape=jax.ShapeDtypeStruct((bsz, seq, w), BF16),
        compiler_params=_cparams(2),
        name="neighbourhood_attention",
    )(q, k, v, cache_k, cache_v, eb)


_DN_HALO = 16
_DN_RSUB = 128
_DN_CGRP = 2 * LANES


def _dnprep_kernel(*refs, tt, nt, use_rope):
    if use_rope:
        main_ref, prev_ref, next_ref, w_ref, cos_ref, sin_ref, q_ref, k_ref, v_ref = refs
    else:
        main_ref, prev_ref, next_ref, w_ref, q_ref, k_ref, v_ref = refs
    t = pl.program_id(1)
    main = main_ref[0]
    halo = jnp.concatenate([jnp.where(t > 0, prev_ref[0], jnp.zeros_like(prev_ref[0])),
                            jnp.where(t < nt - 1, next_ref[0], jnp.zeros_like(next_ref[0]))], axis=0)
    pad = (DN_CONV_K - 1) // 2
    row = lax.broadcasted_iota(jnp.int32, (tt, tt), 0)
    col = lax.broadcasted_iota(jnp.int32, (tt, tt), 1)
    hrow = lax.broadcasted_iota(jnp.int32, (tt, 2 * _DN_HALO), 0)
    hcol = lax.broadcasted_iota(jnp.int32, (tt, 2 * _DN_HALO), 1)
    htok = jnp.where(hcol < _DN_HALO, hcol - _DN_HALO, tt + hcol - _DN_HALO)
    sel = {}
    for kk in range(DN_CONV_K):
        if kk != pad:
            sel[kk] = (jnp.where(col == row + (kk - pad), 1.0, 0.0).astype(BF16),
                       jnp.where(htok == hrow + (kk - pad), 1.0, 0.0).astype(BF16))
    outs = (q_ref, k_ref, v_ref)
    if use_rope:
        lane = lax.broadcasted_iota(jnp.int32, (_DN_RSUB, LANES), 1)
        first_half = (lane & (DN_DK // 4)) == 0
    for cg in range(3 * DN_QK_W // _DN_CGRP):
        gcols = slice(cg * _DN_CGRP, (cg + 1) * _DN_CGRP)
        taps = {}
        for kk in range(DN_CONV_K):
            if kk == pad:
                taps[kk] = main[:, gcols].astype(F32)
            else:
                taps[kk] = (jnp.dot(sel[kk][0], main[:, gcols], preferred_element_type=F32)
                            + jnp.dot(sel[kk][1], halo[:, gcols], preferred_element_type=F32))
        for sub in range(_DN_CGRP // LANES):
            cb = cg * (_DN_CGRP // LANES) + sub
            cols = slice(cb * LANES, (cb + 1) * LANES)
            lcols = slice(sub * LANES, (sub + 1) * LANES)
            for rs in range(tt // _DN_RSUB):
                rsl = slice(rs * _DN_RSUB, (rs + 1) * _DN_RSUB)
                acc = jnp.zeros((_DN_RSUB, LANES), F32)
                for kk in range(DN_CONV_K):
                    acc = acc + w_ref[kk:kk + 1, cols] * taps[kk][rsl, lcols]
                y = _silu(acc)
                if cb < 2 * DN_HEADS:
                    y = y * lax.rsqrt(jnp.sum(y * y, axis=-1, keepdims=True) + EPS)
                    if use_rope:
                        quarter = DN_DK // 4
                        partner = jnp.where(first_half, pltpu.roll(y, LANES - quarter, 1), pltpu.roll(y, quarter, 1))
                        y = y * cos_ref[rsl, :] + partner * sin_ref[rsl, :]
                    if cb < DN_HEADS:
                        y = y * (DN_DK ** -0.5)
                hh = cb % DN_HEADS
                outs[cb // DN_HEADS][0, rsl, hh * LANES:(hh + 1) * LANES] = y


def _rope_lane_tables(seq):
    pos = np.arange(seq)
    n_freq = DN_DK // 4
    inv = ROPE_BASE ** (-np.arange(n_freq) / n_freq)
    ang = np.stack([(pos // GRID_W)[:, None] * inv[None, :], (pos % GRID_W)[:, None] * inv[None, :]], axis=1)
    cos = np.cos(ang).astype(np.float32)
    sin = np.sin(ang).astype(np.float32)
    lane = np.arange(DN_DK)
    axis, half, freq = lane // (2 * n_freq), (lane // n_freq) % 2, lane % n_freq
    cos_l = cos[:, axis, freq]
    sin_l = sin[:, axis, freq] * np.where(half == 0, -1.0, 1.0).astype(np.float32)[None, :]
    return jnp.asarray(cos_l), jnp.asarray(sin_l)


def _dnprep(dqkv, conv_w, use_rope):
    bsz, seq, width = dqkv.shape
    tt = ROW_TILE
    nt = seq // tt
    prev, nxt = _halo_specs(tt, seq, _DN_HALO, width)
    in_specs = [pl.BlockSpec((1, tt, width), lambda b, t: (b, t, 0)), prev, nxt,
                pl.BlockSpec((DN_CONV_K, width), lambda b, t: (0, 0))]
    args = [dqkv, dqkv, dqkv, conv_w]
    if use_rope:
        tab = pl.BlockSpec((tt, DN_DK), lambda b, t: (t, 0))
        in_specs += [tab, tab]
        args += list(_rope_lane_tables(seq))
    out = jax.ShapeDtypeStruct((bsz, seq, DN_QK_W), F32)
    ospec = pl.BlockSpec((1, tt, DN_QK_W), lambda b, t: (b, t, 0))
    return pl.pallas_call(
        functools.partial(_dnprep_kernel, tt=tt, nt=nt, use_rope=use_rope),
        grid=(bsz, nt),
        in_specs=in_specs,
        out_specs=[ospec, ospec, ospec],
        out_shape=[out, out, out],
        compiler_params=_cparams(2),
        name="deltanet_prep",
    )(*args)


_LOG_CHUNK = int(math.log2(DN_CHUNK))
_N_GATES = 4 * DN_HEADS


def _bmm(a, b):
    return lax.dot_general(a.astype(BF16), b.astype(BF16), (((2,), (1,)), ((0,), (0,))),
                           preferred_element_type=F32)


def _bmm_nt(a, b):
    return lax.dot_general(a.astype(BF16), b.astype(BF16), (((2,), (2,)), ((0,), (0,))),
                           preferred_element_type=F32)


def _bmm_tn(a, b):
    return lax.dot_general(a.astype(BF16), b.astype(BF16), (((1,), (1,)), ((0,), (0,))),
                           preferred_element_type=F32)


def _pair_cols(col_a, col_b, width):
    lane = lax.broadcasted_iota(jnp.int32, (col_a.shape[0], width), 1)
    return jnp.where(lane < width // 2, col_a, col_b)


def _block_diag(y):
    lane = lax.broadcasted_iota(jnp.int32, y.shape, 2)
    first = lane < y.shape[2] // 2
    return jnp.concatenate([jnp.where(first, y, 0.0), jnp.where(first, 0.0, y)], axis=1)


def _chunk_cumsum(x, axis, reverse):
    n = x.shape[axis]
    pos = lax.broadcasted_iota(jnp.int32, x.shape, axis) & (DN_CHUNK - 1)
    step = 1
    while step < DN_CHUNK:
        if reverse:
            x = x + jnp.where(pos < DN_CHUNK - step, pltpu.roll(x, n - step, axis), 0.0)
        else:
            x = x + jnp.where(pos >= step, pltpu.roll(x, step, axis), 0.0)
        step *= 2
    return x


def _dn_tile_pair(fwd, bwd, par, of_ref, ob_ref, s_scr):
    alog_ref, dtb_ref, alogt_ref, dtbt_ref = par
    cs = DN_CHUNK
    pw = 2 * cs
    hp_n = DN_HEADS // 2
    tile = of_ref.shape[1]
    n_chunks = tile // cs
    npd = n_chunks * hp_n
    ii = lax.broadcasted_iota(jnp.int32, (cs, pw), 0)
    jj = lax.broadcasted_iota(jnp.int32, (cs, pw), 1) & (cs - 1)
    sx = jnp.stack([jnp.where(ii >= jj, ii ^ jj, -1), jnp.where(ii <= jj, ii ^ jj, -1)], axis=0)[:, None]
    incl = sx >= 0
    before = sx > 0

    gc_l, gct_l, beta_l, gc2_l, beta2_l, tot2_l, q_l, k_l, v_l = ([] for _ in range(9))
    for d, (q_ref, k_ref, v_ref, ab_ref, abt_ref) in enumerate((fwd, bwd)):
        ab = ab_ref[0]
        abt = abt_ref[0]
        g_col = -jnp.exp(alog_ref[...]) * _softplus(ab + dtb_ref[...])
        g_row = -jnp.exp(alogt_ref[...]) * _softplus(abt + dtbt_ref[...])
        beta_all = _sigmoid(ab)
        gc_rows = jnp.concatenate([_chunk_cumsum(g_row[:, i * LANES:(i + 1) * LANES], 1, d == 1)
                                   for i in range(tile // LANES)], axis=1)
        qx = q_ref[0]
        kx = k_ref[0]
        vx = v_ref[0]
        for c in range(n_chunks):
            rows = slice(c * cs, (c + 1) * cs)
            gc_c = _chunk_cumsum(g_col[rows], 0, d == 1)
            gct_c = gc_rows[:, rows]
            last = 0 if d == 1 else cs - 1
            tot_c = gc_c[last:last + 1, :]
            for hp in range(hp_n):
                ga = d * DN_HEADS + 2 * hp
                gb = ga + 1
                ba = ga + 2 * DN_HEADS
                bb = ba + 1
                gc_l.append(_pair_cols(gc_c[:, ga:ga + 1], gc_c[:, gb:gb + 1], pw))
                gct_l.append(jnp.concatenate([gct_c[ga:ga + 1, :], gct_c[gb:gb + 1, :]], axis=1))
                beta_l.append(_pair_cols(beta_all[rows, ba:ba + 1], beta_all[rows, bb:bb + 1], pw))
                gc2_l.append(_pair_cols(gc_c[:, ga:ga + 1], gc_c[:, gb:gb + 1], 2 * LANES))
                beta2_l.append(_pair_cols(beta_all[rows, ba:ba + 1], beta_all[rows, bb:bb + 1], 2 * LANES))
                tot2_l.append(_pair_cols(tot_c[:, ga:ga + 1], tot_c[:, gb:gb + 1], 2 * LANES))
                cols = slice(hp * 2 * LANES, (hp + 1) * 2 * LANES)
                q_l.append(qx[rows, cols])
                k_l.append(kx[rows, cols])
                v_l.append(vx[rows, cols])

    def stk(xs):
        return jnp.stack(xs, axis=0)

    gc, gc_t, beta = stk(gc_l), stk(gct_l), stk(beta_l)
    gc2, beta2, tot2 = stk(gc2_l), stk(beta2_l), stk(tot2_l)
    q, k, v = stk(q_l), stk(k_l), stk(v_l)

    def by_dir(x):
        return x.reshape((2, npd) + x.shape[1:])

    def flat(x):
        return x.reshape((2 * npd,) + x.shape[2:])

    k_bd = _block_diag(k)
    kk = by_dir(_bmm_nt(k, k_bd))
    qk = by_dir(_bmm_nt(q, k_bd))
    decay = jnp.where(incl, jnp.exp(jnp.where(incl, by_dir(gc - gc_t), 0.0)), 0.0)
    a_mat = jnp.where(before, (by_dir(beta) * kk) * decay, 0.0)
    a_qk = flat(qk * decay)

    e_mat = flat(-jnp.where(sx == 1, a_mat, 0.0))
    for lb in range(1, _LOG_CHUNK):
        l_b = flat(jnp.where(lax.shift_right_arithmetic(sx, lb) == 1, a_mat, 0.0))
        p_mat = l_b + _bmm(l_b, _block_diag(e_mat))
        e_mat = e_mat - p_mat - _bmm(e_mat, _block_diag(p_mat))

    eg2 = jnp.exp(gc2)
    vb = v * beta2
    kb = k * (beta2 * eg2)
    zero2 = jnp.zeros(vb.shape, F32)
    rhs = jnp.concatenate([vb[..., :LANES], kb[..., :LANES], vb[..., LANES:], kb[..., LANES:]], axis=-1)
    rhs_bd = jnp.concatenate([jnp.concatenate([vb[..., :LANES], kb[..., :LANES], zero2], axis=-1),
                              jnp.concatenate([zero2, vb[..., LANES:], kb[..., LANES:]], axis=-1)], axis=1)
    sol = rhs + _bmm(e_mat, rhs_bd)
    q_dec = q * eg2
    k_dec = k * jnp.exp(tot2 - gc2)
    c_dec2 = jnp.exp(tot2)

    def heads(x, chunks, off, stride):
        out = []
        for d in range(2):
            for hp in range(hp_n):
                p = d * npd + chunks[d] * hp_n + hp
                out += [x[p][:, off:off + LANES], x[p][:, off + stride:off + stride + LANES]]
        return jnp.stack(out, axis=0)

    s = s_scr[...].reshape(2 * DN_HEADS, DN_DK, DN_DV)
    zero_u = jnp.zeros((cs, LANES), F32)
    for step in range(n_chunks):
        chunks = (step, n_chunks - 1 - step)
        u_base = heads(sol, chunks, 0, 2 * LANES)
        w_dec = heads(sol, chunks, LANES, 2 * LANES)
        qd = heads(q_dec, chunks, 0, LANES)
        kd = heads(k_dec, chunks, 0, LANES)
        cd = heads(c_dec2, chunks, 0, LANES)[:, :, :1]
        ws_qs = _bmm(jnp.concatenate([w_dec, qd], axis=1), s)
        u = u_base - ws_qs[:, :cs]
        u_bd = jnp.stack([jnp.concatenate([jnp.concatenate([u[2 * i], zero_u], axis=-1),
                                           jnp.concatenate([zero_u, u[2 * i + 1]], axis=-1)], axis=0)
                          for i in range(DN_HEADS)], axis=0)
        aqk_sel = jnp.stack([a_qk[d * npd + chunks[d] * hp_n + hp] for d in range(2) for hp in range(hp_n)], axis=0)
        o_loc = _bmm(aqk_sel, u_bd)
        s = s * cd + _bmm_tn(kd, u)
        for d, o_ref in enumerate((of_ref, ob_ref)):
            c = chunks[d]
            for h in range(DN_HEADS):
                i = d * DN_HEADS + h
                o_ref[0, c * cs:(c + 1) * cs, h * LANES:(h + 1) * LANES] = (
                    ws_qs[i, cs:] + o_loc[i // 2][:, (h % 2) * LANES:(h % 2 + 1) * LANES]).astype(o_ref.dtype)
    s_scr[...] = s.reshape(2, DN_HEADS, DN_DK, DN_DV)


def _dnscan_kernel(*refs, nt, has_s0, want_state):
    it = iter(refs)
    fwd = [next(it) for _ in range(5)]
    bwd = [next(it) for _ in range(5)]
    par = [next(it) for _ in range(4)]
    s0_ref = next(it) if has_s0 else None
    of_ref = next(it)
    ob_ref = next(it)
    st_ref = next(it) if want_state else None
    s_scr = next(it)
    t = pl.program_id(1)

    @pl.when(t == 0)
    def _():
        if has_s0:
            s_scr[...] = s0_ref[0]
        else:
            s_scr[...] = jnp.zeros(s_scr.shape, F32)

    _dn_tile_pair(fwd, bwd, par, of_ref, ob_ref, s_scr)

    if want_state:
        @pl.when(t == nt - 1)
        def _():
            st_ref[0] = s_scr[...]


def _dnscan(q, k, v, ab, abt, a_log, dt_bias, s0, want_state):
    bsz, seq, _ = q.shape
    tile = min(DN_TILE, seq)
    nt = seq // tile

    def tile_specs(tmap):
        qs = pl.BlockSpec((1, tile, DN_V_W), lambda b, t: (b, tmap(t), 0))
        return [qs, qs, qs,
                pl.BlockSpec((1, tile, _N_GATES), lambda b, t: (b, tmap(t), 0)),
                pl.BlockSpec((1, _N_GATES, tile), lambda b, t: (b, 0, tmap(t)))]

    zeros8 = jnp.zeros((2 * DN_HEADS,), F32)
    alog16 = jnp.concatenate([a_log.reshape(-1), zeros8])
    dtb16 = jnp.concatenate([dt_bias.reshape(-1), zeros8])
    prow = pl.BlockSpec((1, _N_GATES), lambda b, t: (0, 0))
    pcol = pl.BlockSpec((_N_GATES, 1), lambda b, t: (0, 0))
    in_specs = tile_specs(lambda t: t) + tile_specs(lambda t: nt - 1 - t) + [prow, prow, pcol, pcol]
    args = [q, k, v, ab, abt, q, k, v, ab, abt,
            alog16.reshape(1, _N_GATES), dtb16.reshape(1, _N_GATES),
            alog16.reshape(_N_GATES, 1), dtb16.reshape(_N_GATES, 1)]
    st_block = (1, 2, DN_HEADS, DN_DK, DN_DV)
    if s0 is not None:
        in_specs.append(pl.BlockSpec(st_block, lambda b, t: (b, 0, 0, 0, 0)))
        args.append(s0)
    o_shape = jax.ShapeDtypeStruct((bsz, seq, DN_V_W), BF16)
    out_specs = [pl.BlockSpec((1, tile, DN_V_W), lambda b, t: (b, t, 0)),
                 pl.BlockSpec((1, tile, DN_V_W), lambda b, t: (b, nt - 1 - t, 0))]
    out_shape = [o_shape, o_shape]
    if want_state:
        out_specs.append(pl.BlockSpec(st_block, lambda b, t: (b, 0, 0, 0, 0)))
        out_shape.append(jax.ShapeDtypeStruct((bsz, 2, DN_HEADS, DN_DK, DN_DV), F32))
    return pl.pallas_call(
        functools.partial(_dnscan_kernel, nt=nt, has_s0=s0 is not None, want_state=want_state),
        grid=(bsz, nt),
        in_specs=in_specs,
        out_specs=out_specs,
        out_shape=out_shape,
        scratch_shapes=[pltpu.VMEM((2, DN_HEADS, DN_DK, DN_DV), F32)],
        compiler_params=_cparams(2),
        name="deltanet_scan",
    )(*args)


def _first_argmax(x, lane, valid):
    xm = jnp.where(valid, x, -jnp.inf)
    m = jnp.max(xm, axis=-1, keepdims=True)
    idx = jnp.min(jnp.where(valid & (xm == m), lane, float(LANES)), axis=-1, keepdims=True)
    return m, idx


def _route(h, rw_hi_ref, rw_lo_ref, rb_ref):
    h_hi = h.astype(BF16)
    h_lo = (h - h_hi.astype(F32)).astype(BF16)
    both = jnp.dot(h_hi, jnp.concatenate([rw_hi_ref[...], rw_lo_ref[...]], axis=1), preferred_element_type=F32)
    logits = (both[:, :LANES] + both[:, LANES:]
              + jnp.dot(h_lo, rw_hi_ref[...], preferred_element_type=F32)) + rb_ref[...]
    lane = lax.broadcasted_iota(jnp.int32, logits.shape, 1).astype(F32)
    is_grp = lane < N_GROUPS
    gmax, g_sel = _first_argmax(logits, lane, is_grp)
    pg_sel = 1.0 / jnp.sum(jnp.where(is_grp, jnp.exp(logits - gmax), 0.0), axis=-1, keepdims=True)
    e_lane = lane - N_GROUPS
    in_grp = (e_lane >= g_sel * EXPERTS_PER_GROUP) & (e_lane < (g_sel + 1) * EXPERTS_PER_GROUP)
    m1, i1 = _first_argmax(logits, lane, in_grp)
    m2, i2 = _first_argmax(logits, lane, in_grp & (lane != i1))
    e2 = jnp.exp(m2 - m1)
    w1 = pg_sel / (1.0 + e2)
    w2 = pg_sel * e2 / (1.0 + e2)
    return g_sel, jnp.where(lane == i1, w1, 0.0) + jnp.where(lane == i2, w2, 0.0)


def _mix_out(x_ref, a_ref, b_ref, of_ref, ob_ref, z_ref, ng_ref, ga_ref, w_ref):
    o = of_ref[...].astype(F32) + ob_ref[...].astype(F32)
    z = z_ref[...].astype(F32)
    acc = jnp.dot(a_ref[...], w_ref[0:CONV_W, :], preferred_element_type=F32)
    acc = acc + jnp.dot(b_ref[...], w_ref[CONV_W:CONV_W + NA_W, :], preferred_element_type=F32)
    for h in range(DN_HEADS):
        cols = slice(h * DN_DV, (h + 1) * DN_DV)
        oh = o[:, cols]
        y = oh * lax.rsqrt(jnp.mean(oh * oh, axis=-1, keepdims=True) + EPS) * ng_ref[...] * _silu(z[:, cols])
        r0 = CONV_W + NA_W + h * DN_DV
        acc = acc + jnp.dot(y.astype(BF16), w_ref[r0:r0 + DN_DV, :], preferred_element_type=F32)
    return x_ref[...] + ga_ref[0] * acc


def _moe_kernel(x_ref, a_ref, b_ref, of_ref, ob_ref, z_ref, ng_ref, ga1_ref, wout_ref,
                g_ref, sc_ref, sh_ref, ga_ref, rw_hi_ref, rw_lo_ref, rb_ref, tri_ref, w1_ref, w3_ref, w2_ref,
                gf_ref, o_ref, h_scr, ghi_scr, glo_scr, oh_scr, rank_scr, oht_scr, rankt_scr, cnt_scr,
                *, final_norm):
    grp = pl.program_id(1)
    tm = x_ref.shape[0]

    @pl.when(grp == 0)
    def _():
        x_new = _mix_out(x_ref, a_ref, b_ref, of_ref, ob_ref, z_ref, ng_ref, ga1_ref, wout_ref)
        o_ref[...] = x_new
        h = _mod_norm(x_new, g_ref[...], sc_ref[0], sh_ref[0])
        h_scr[...] = h.astype(BF16)
        g_sel, gates = _route(h, rw_hi_ref, rw_lo_ref, rb_ref)
        g_hi = gates.astype(BF16)
        ghi_scr[...] = g_hi
        glo_scr[...] = (gates - g_hi.astype(F32)).astype(BF16)
        lane = lax.broadcasted_iota(jnp.int32, (tm, LANES), 1).astype(F32)
        one_hot = jnp.where(lane == g_sel, 1.0, 0.0)
        oh_b = one_hot.astype(BF16)
        oh_scr[...] = one_hot
        rank_scr[...] = jnp.dot(tri_ref[...], oh_b, preferred_element_type=F32)
        pick = jnp.where(lax.broadcasted_iota(jnp.int32, (SUBLANES, LANES), 0)
                         == lax.broadcasted_iota(jnp.int32, (SUBLANES, LANES), 1), 1.0, 0.0).astype(BF16)
        oh_t = lax.dot_general(pick, oh_b, _NT, preferred_element_type=F32)
        oht_scr[...] = oh_t
        rankt_scr[...] = lax.dot_general(oh_t.astype(BF16), tri_ref[...], _NT, preferred_element_type=F32)
        for k in range(N_GROUPS):
            cnt_scr[k] = jnp.sum(one_hot[:, k:k + 1]).astype(jnp.int32)

    hb = h_scr[...]
    lane = lax.broadcasted_iota(jnp.int32, (tm, LANES), 1)
    in_col = jnp.sum(jnp.where(lane == grp, oh_scr[...], 0.0), axis=-1, keepdims=True)
    rank_col = jnp.sum(jnp.where(lane == grp, rank_scr[...], 0.0), axis=-1, keepdims=True)
    in_row = oht_scr[pl.ds(grp, 1), :]
    rank_row = rankt_scr[pl.ds(grp, 1), :]
    def chunk(first_rank, n_rows):
        slot_r = lax.broadcasted_iota(jnp.int32, (n_rows, tm), 0).astype(F32)
        slot_c = lax.broadcasted_iota(jnp.int32, (tm, n_rows), 1).astype(F32)
        glane = lax.broadcasted_iota(jnp.int32, (n_rows, LANES), 1)
        base = first_rank.astype(F32)
        take = jnp.where((in_row > 0.5) & (rank_row - base == slot_r), 1.0, 0.0).astype(BF16)
        put = jnp.where((in_col > 0.5) & (rank_col - base == slot_c), 1.0, 0.0).astype(BF16)
        xs = jnp.dot(take, hb, preferred_element_type=F32).astype(BF16)
        gates = (jnp.dot(take, ghi_scr[...], preferred_element_type=F32)
                 + jnp.dot(take, glo_scr[...], preferred_element_type=F32))
        y = None
        for half in range(EXPERTS_PER_GROUP // 2):
            hid = []
            for j in (2 * half, 2 * half + 1):
                gate_e = jnp.sum(jnp.where(glane == N_GROUPS + grp * EXPERTS_PER_GROUP + j, gates, 0.0),
                                 axis=-1, keepdims=True)
                up = _silu(jnp.dot(xs, w1_ref[0, j], preferred_element_type=F32)) * jnp.dot(
                    xs, w3_ref[0, j], preferred_element_type=F32)
                hid.append((up * gate_e).astype(BF16))
            rows = slice(2 * half * D_EXPERT, 2 * (half + 1) * D_EXPERT)
            part = jnp.dot(jnp.concatenate(hid, axis=-1), w2_ref[0, 0, rows, :], preferred_element_type=F32)
            y = part if y is None else y + part
        o_ref[...] += ga_ref[0] * jnp.dot(put, y.astype(BF16), preferred_element_type=F32)

    count = cnt_scr[grp]
    rest = lax.rem(count, MOE_ROWS)
    n_full = count // MOE_ROWS + (rest > MOE_ROWS // 2).astype(jnp.int32)

    def full_chunk(c, carry):
        chunk(c * MOE_ROWS, MOE_ROWS)
        return carry

    lax.fori_loop(0, n_full, full_chunk, 0)

    @pl.when((rest > 0) & (rest <= MOE_ROWS // 2))
    def _():
        chunk(n_full * MOE_ROWS, MOE_ROWS // 2)

    if final_norm:
        @pl.when(grp == N_GROUPS - 1)
        def _():
            y = o_ref[...]
            o_ref[...] = y * lax.rsqrt(jnp.mean(y * y, axis=-1, keepdims=True) + EPS) * gf_ref[...]


def _mix_moe(x, out_a, out_b, o_f, o_b, z, norm_g, gate1, w_out_b,
             g, scale, shift, gate, rw_hi, rw_lo, rb, w1_b, w3_b, w2_b, g_final, layer, final_norm):
    bsz, seq, d = x.shape
    n = bsz * seq
    x2 = x.reshape(n, d)
    flat = [a.reshape(n, a.shape[-1]) for a in (out_a, out_b, o_f, o_b, z)]
    if scale.shape[0] == 1:
        tm = MOE_TILE
        mm = lambda i, e: (0, 0, 0)
    else:
        tm = min(MOE_TILE, seq)
        per_seq = seq // tm
        mm = lambda i, e: (i // per_seq, 0, 0)
    tri = jnp.asarray(np.tril(np.ones((tm, tm), np.float32), -1), BF16)
    vec = pl.BlockSpec((1, d), lambda i, e: (0, 0))
    mod = pl.BlockSpec((1, 1, d), mm)
    rspec = pl.BlockSpec((d, LANES), lambda i, e: (0, 0))
    gw = EXPERTS_PER_GROUP * D_EXPERT
    out = pl.pallas_call(
        functools.partial(_moe_kernel, final_norm=final_norm),
        grid=(bsz * seq // tm, N_GROUPS),
        in_specs=[pl.BlockSpec((tm, d), lambda i, e: (i, 0))]
                 + [pl.BlockSpec((tm, a.shape[-1]), lambda i, e: (i, 0)) for a in flat]
                 + [pl.BlockSpec((1, DN_DV), lambda i, e: (0, 0)), mod,
                    pl.BlockSpec((None, MIX_W, d), lambda i, e: (layer, 0, 0)),
                    vec, mod, mod, mod,
                  rspec, rspec, pl.BlockSpec((1, LANES), lambda i, e: (0, 0)),
                  pl.BlockSpec((tm, tm), lambda i, e: (0, 0)),
                  pl.BlockSpec((None, 1, EXPERTS_PER_GROUP, d, D_EXPERT), lambda i, e: (layer, e, 0, 0, 0)),
                  pl.BlockSpec((None, 1, EXPERTS_PER_GROUP, d, D_EXPERT), lambda i, e: (layer, e, 0, 0, 0)),
                  pl.BlockSpec((1, 1, gw, d), lambda i, e: (layer, e, 0, 0)),
                  vec],
        out_specs=pl.BlockSpec((tm, d), lambda i, e: (i, 0)),
        out_shape=jax.ShapeDtypeStruct((bsz * seq, d), F32),
        scratch_shapes=[pltpu.VMEM((tm, d), BF16), pltpu.VMEM((tm, LANES), BF16), pltpu.VMEM((tm, LANES), BF16),
                        pltpu.VMEM((tm, LANES), F32), pltpu.VMEM((tm, LANES), F32),
                        pltpu.VMEM((SUBLANES, tm), F32), pltpu.VMEM((SUBLANES, tm), F32),
                        pltpu.SMEM((N_GROUPS,), jnp.int32)],
        compiler_params=pltpu.CompilerParams(dimension_semantics=("arbitrary", "arbitrary"),
                                             vmem_limit_bytes=MOE_VMEM_LIMIT),
        name="moe",
    )(x2, *flat, norm_g.reshape(1, -1), gate1, w_out_b,
      g, scale, shift, gate, rw_hi, rw_lo, rb, tri, w1_b, w3_b, w2_b, g_final.reshape(1, -1))
    return out.reshape(bsz, seq, d)


def _split_hi_lo(w):
    hi = w.astype(BF16)
    return hi, (w - hi.astype(F32)).astype(BF16)


def kernel(x_prompt, x_sample, cache_na_k, cache_na_v, state_delta, c, c_ctx, w_mod, b_mod, g_norm1, g_norm2,
           w_in, conv_a_w, conv_a_b, ln_a_g, ln_a_b, na_rel_bias, dn_conv_w, dn_a_log, dn_dt_bias, dn_norm_g,
           w_out, router_wg, router_bg, router_we, router_be, w1, w3, w2, g_final):
    n_dec = x_sample.shape[0]
    d = D_MODEL
    cond8 = jnp.concatenate([c_ctx[None, :], c, jnp.zeros((8 - 1 - n_dec, d), F32)], axis=0)
    mods = _adaln(cond8, w_mod, b_mod).reshape(DEPTH, 8, 6, d)

    w_in_b = w_in[:, :, :OFF_DN_AB].astype(BF16)
    w_abt_b = jnp.swapaxes(w_in[:, :, OFF_DN_AB:], 1, 2).astype(BF16)
    w_out_b = w_out.astype(BF16)
    w1_b = w1.astype(BF16).reshape(DEPTH, N_GROUPS, EXPERTS_PER_GROUP, d, D_EXPERT)
    w3_b = w3.astype(BF16).reshape(DEPTH, N_GROUPS, EXPERTS_PER_GROUP, d, D_EXPERT)
    w2_b = w2.astype(BF16).reshape(DEPTH, N_GROUPS, EXPERTS_PER_GROUP * D_EXPERT, d)
    pad = LANES - N_GROUPS - N_EXPERTS
    rw = jnp.concatenate([router_wg, router_we, jnp.zeros((DEPTH, d, pad), F32)], axis=-1)
    rw_hi, rw_lo = _split_hi_lo(rw)
    rb = jnp.concatenate([router_bg, router_be, jnp.zeros((DEPTH, pad), F32)], axis=-1)

    past = cache_na_k.shape[2]
    cache_k = cache_na_k.reshape(n_dec, DEPTH, past, NA_W)
    cache_v = cache_na_v.reshape(n_dec, DEPTH, past, NA_W)
    rows = x_sample.shape[1] // GRID_W

    def layer(x, l, is_ctx):
        m = mods[l, 0:1] if is_ctx else mods[l, 1:1 + n_dec]
        sh1, sc1, ga1, sh2, sc2, ga2 = (m[:, i:i + 1, :] for i in range(6))
        glu, q_na, k_na, v_na, dqkv, dz, ab, abt = _inproj(x, g_norm1[l:l + 1], sc1, sh1, w_in_b, w_abt_b, l)
        out_a = _cconv(glu, conv_a_w[l], conv_a_b[l], ln_a_g[l], ln_a_b[l])
        if is_ctx:
            out_b = _ctx_attn(q_na, k_na, v_na)
        else:
            out_b = _na_attn(q_na, k_na, v_na, cache_k, cache_v, l, _na_bias_tables(na_rel_bias[l], rows))
        dq, dk, dv = _dnprep(dqkv, dn_conv_w[l], use_rope=not is_ctx)
        scan = _dnscan(dq, dk, dv, ab, abt, dn_a_log[l], dn_dt_bias[l],
                       None if is_ctx else state_delta[:, l], want_state=is_ctx)
        o_f, o_b = scan[0], scan[1]
        x = _mix_moe(x, out_a, out_b, o_f, o_b, dz, dn_norm_g[l], ga1, w_out_b,
                     g_norm2[l:l + 1], sc2, sh2, ga2, rw_hi[l], rw_lo[l], rb[l:l + 1], w1_b, w3_b, w2_b,
                     g_final, layer=l, final_norm=(l == DEPTH - 1))
        return x, k_na, v_na, (scan[2] if is_ctx else None)

    xc = x_prompt
    new_k, new_v, new_s = [], [], []
    for l in range(DEPTH):
        xc, k_na, v_na, s_ctx = layer(xc, l, True)
        new_k.append(k_na.reshape(k_na.shape[0], k_na.shape[1], NA_HEADS, NA_HEAD_DIM))
        new_v.append(v_na.reshape(v_na.shape[0], v_na.shape[1], NA_HEADS, NA_HEAD_DIM))
        new_s.append(s_ctx)
    xs = x_sample
    for l in range(DEPTH):
        xs, _, _, _ = layer(xs, l, False)
    return (xc, xs, jnp.stack(new_k, axis=1), jnp.stack(new_v, axis=1), jnp.stack(new_s, axis=1))
```

```python
import functools
import math

import numpy as np
import jax
import jax.numpy as jnp
from jax import lax
from jax.experimental import pallas as pl
from jax.experimental.pallas import tpu as pltpu

F32 = jnp.float32
BF16 = jnp.bfloat16

D_MODEL = 1024
DEPTH = 2
GRID_W = 64
CONV_W = 256
CONV_K = 31
NA_HEADS = 4
NA_HEAD_DIM = 64
NA_W = NA_HEADS * NA_HEAD_DIM
NA_ROWS = 8
NA_COLS = 16
DN_HEADS = 4
DN_DK = 128
DN_DV = 128
DN_QK_W = DN_HEADS * DN_DK
DN_V_W = DN_HEADS * DN_DV
DN_CONV_K = 5
DN_CHUNK = 64
ROPE_BASE = 10000.0
MIX_W = CONV_W + NA_W + DN_V_W
OFF_NA = 2 * CONV_W
OFF_DN_QKV = OFF_NA + 3 * NA_W
OFF_DN_Z = OFF_DN_QKV + 2 * DN_QK_W + DN_V_W
OFF_DN_AB = OFF_DN_Z + DN_V_W
PROJ_W = OFF_DN_AB + 4 * DN_HEADS
N_GROUPS = 4
EXPERTS_PER_GROUP = 4
N_EXPERTS = N_GROUPS * EXPERTS_PER_GROUP
D_EXPERT = 256
EPS = 1e-6
NEG_INF = -1e30

LANES = 128
SUBLANES = 8
ROW_TILE = 256
CONV_TILE = 512
PROJ_TILE = 512
DN_TILE = 512
NA_QROWS = 4
NA_WIN_ROWS = 12
MOE_TILE = 1024
MOE_ROWS = 256
VMEM_LIMIT = 48 * 1024 * 1024
MOE_VMEM_LIMIT = 56 * 1024 * 1024

_NT = (((1,), (1,)), ((), ()))


def _cparams(n_axes):
    return pltpu.CompilerParams(dimension_semantics=("arbitrary",) * n_axes,
                                vmem_limit_bytes=VMEM_LIMIT)


def _sigmoid(x):
    return 1.0 / (1.0 + jnp.exp(-x))


def _silu(x):
    return x * _sigmoid(x)


def _softplus(x):
    return jnp.maximum(x, 0.0) + jnp.log(1.0 + jnp.exp(-jnp.abs(x)))


def _dot(a, b):
    return jnp.dot(a.astype(BF16), b.astype(BF16), preferred_element_type=F32)


def _adaln_kernel(c_ref, w_ref, b_ref, o_ref):
    c = c_ref[...]
    o_ref[0] = _dot(_silu(c), w_ref[0]) + b_ref[0]


def _adaln(cond8, w_mod, b_mod):
    n_l, d, n6 = w_mod.shape
    tn = 1536
    return pl.pallas_call(
        _adaln_kernel,
        grid=(n_l, n6 // tn),
        in_specs=[pl.BlockSpec((8, d), lambda l, j: (0, 0)),
                  pl.BlockSpec((1, d, tn), lambda l, j: (l, 0, j)),
                  pl.BlockSpec((1, 1, tn), lambda l, j: (l, 0, j))],
        out_specs=pl.BlockSpec((1, 8, tn), lambda l, j: (l, 0, j)),
        out_shape=jax.ShapeDtypeStruct((n_l, 8, n6), F32),
        compiler_params=_cparams(2),
        name="adaln",
    )(cond8, w_mod, b_mod.reshape(n_l, 1, n6))


def _mod_norm(x, g, scale, shift):
    ms = jnp.mean(x * x, axis=-1, keepdims=True)
    return (x * lax.rsqrt(ms + EPS) * g) * (1.0 + scale) + shift


def _inproj_kernel(x_ref, g_ref, sc_ref, sh_ref, w_ref, wabt_ref,
                   glu_ref, q_ref, k_ref, v_ref, dqkv_ref, z_ref, ab_ref, abt_ref):
    hb = _mod_norm(x_ref[0], g_ref[...], sc_ref[0], sh_ref[0]).astype(BF16)

    def proj(a, b):
        return jnp.dot(hb, w_ref[:, a:b], preferred_element_type=F32)

    glu_ref[0] = proj(0, OFF_NA)
    q_ref[0] = proj(OFF_NA, OFF_NA + NA_W).astype(BF16)
    k_ref[0] = proj(OFF_NA + NA_W, OFF_NA + 2 * NA_W)
    v_ref[0] = proj(OFF_NA + 2 * NA_W, OFF_DN_QKV)
    dqkv_ref[0] = proj(OFF_DN_QKV, OFF_DN_Z).astype(BF16)
    z_ref[0] = proj(OFF_DN_Z, OFF_DN_AB).astype(BF16)
    abt = lax.dot_general(wabt_ref[...], hb, _NT, preferred_element_type=F32)
    abt_ref[0] = abt
    ab_ref[0] = abt.T


def _bmap(bm):
    if bm == 1:
        return lambda b, t: (0, 0, 0)
    return lambda b, t: (b, 0, 0)


def _merge_short_sequences(x, tm, shared_mod):
    bsz, seq, d = x.shape
    if shared_mod and seq < tm:
        return x.reshape(bsz * seq // tm, tm, d)
    return x


def _inproj(x, g, scale, shift, w_b, wabt_b, layer):
    bsz0, seq0, _ = x.shape
    tm = PROJ_TILE
    x = _merge_short_sequences(x, tm, scale.shape[0] == 1)
    outs = _inproj_call(x, g, scale, shift, w_b, wabt_b, layer, tm)
    if x.shape[0] == bsz0:
        return outs
    per = tm // seq0
    row_outs = [o.reshape(bsz0, seq0, o.shape[-1]) for o in outs[:-1]]
    abt = outs[-1].reshape(x.shape[0], 4 * DN_HEADS, per, seq0)
    abt = jnp.swapaxes(abt, 1, 2).reshape(bsz0, 4 * DN_HEADS, seq0)
    return row_outs + [abt]


def _inproj_call(x, g, scale, shift, w_b, wabt_b, layer, tm):
    bsz, seq, d = x.shape
    widths = (OFF_NA, NA_W, NA_W, NA_W, OFF_DN_Z - OFF_DN_QKV, DN_V_W, 4 * DN_HEADS)
    dtypes = (F32, BF16, F32, F32, BF16, BF16, F32)
    out_shape = [jax.ShapeDtypeStruct((bsz, seq, w), dt) for w, dt in zip(widths, dtypes)]
    out_shape.append(jax.ShapeDtypeStruct((bsz, 4 * DN_HEADS, seq), F32))
    out_specs = [pl.BlockSpec((1, tm, w), lambda b, t: (b, t, 0)) for w in widths]
    out_specs.append(pl.BlockSpec((1, 4 * DN_HEADS, tm), lambda b, t: (b, 0, t)))
    mm = _bmap(scale.shape[0])
    return pl.pallas_call(
        _inproj_kernel,
        grid=(bsz, seq // tm),
        in_specs=[pl.BlockSpec((1, tm, d), lambda b, t: (b, t, 0)),
                  pl.BlockSpec((1, d), lambda b, t: (0, 0)),
                  pl.BlockSpec((1, 1, d), mm),
                  pl.BlockSpec((1, 1, d), mm),
                  pl.BlockSpec((None, d, OFF_DN_AB), lambda b, t: (layer, 0, 0)),
                  pl.BlockSpec((None, 4 * DN_HEADS, d), lambda b, t: (layer, 0, 0))],
        out_specs=out_specs,
        out_shape=out_shape,
        compiler_params=_cparams(2),
        name="inproj",
    )(x, g, scale, shift, w_b, wabt_b)


_CC_HALO = 16
_CC_SUB = 64


def _cconv_kernel(main_ref, prev_ref, next_ref, w_ref, b_ref, g_ref, beta_ref, o_ref, scr, *, tt, nt):
    t = pl.program_id(1)

    def glu(a):
        return a[:, :CONV_W] * _sigmoid(a[:, CONV_W:])

    scr[0, 0:_CC_HALO, :] = jnp.where(t > 0, glu(prev_ref[0]), 0.0)
    scr[0, _CC_HALO:_CC_HALO + tt, :] = glu(main_ref[0])
    scr[0, _CC_HALO + tt:2 * _CC_HALO + tt, :] = jnp.where(t < nt - 1, glu(next_ref[0]), 0.0)
    n_shift = tt + 2 * _CC_HALO - SUBLANES
    for r in range(1, SUBLANES):
        scr[r, 0:n_shift, :] = scr[0, r:r + n_shift, :]
    pad = (CONV_K - 1) // 2
    for s in range(tt // _CC_SUB):
        base = s * _CC_SUB + _CC_HALO - pad
        acc = jnp.zeros((_CC_SUB, CONV_W), F32) + b_ref[...]
        for k in range(CONV_K):
            r = (base + k) % SUBLANES
            a0 = base + k - r
            acc = acc + w_ref[k:k + 1, :] * scr[r, a0:a0 + _CC_SUB, :]
        mu = jnp.mean(acc, axis=-1, keepdims=True)
        xc = acc - mu
        var = jnp.mean(xc * xc, axis=-1, keepdims=True)
        y = xc * lax.rsqrt(var + EPS) * g_ref[...] + beta_ref[...]
        o_ref[0, s * _CC_SUB:(s + 1) * _CC_SUB, :] = _silu(y).astype(BF16)


def _halo_specs(tt, seq, halo, width):
    per = tt // halo
    last = seq // halo - 1
    prev = pl.BlockSpec((1, halo, width), lambda b, t: (b, jnp.maximum(t * per - 1, 0), 0))
    nxt = pl.BlockSpec((1, halo, width), lambda b, t: (b, jnp.minimum((t + 1) * per, last), 0))
    return prev, nxt


def _cconv(glu, w, b, g, beta):
    bsz, seq, _ = glu.shape
    tt = min(CONV_TILE, seq)
    nt = seq // tt
    prev, nxt = _halo_specs(tt, seq, _CC_HALO, 2 * CONV_W)
    vec = pl.BlockSpec((1, CONV_W), lambda b_, t: (0, 0))
    return pl.pallas_call(
        functools.partial(_cconv_kernel, tt=tt, nt=nt),
        grid=(bsz, nt),
        in_specs=[pl.BlockSpec((1, tt, 2 * CONV_W), lambda b_, t: (b_, t, 0)), prev, nxt,
                  pl.BlockSpec((CONV_K, CONV_W), lambda b_, t: (0, 0)), vec, vec, vec],
        out_specs=pl.BlockSpec((1, tt, CONV_W), lambda b_, t: (b_, t, 0)),
        out_shape=jax.ShapeDtypeStruct((bsz, seq, CONV_W), BF16),
        scratch_shapes=[pltpu.VMEM((SUBLANES, tt + 2 * _CC_HALO, CONV_W), F32)],
        compiler_params=_cparams(2),
        name="conformer_conv",
    )(glu, glu, glu, w, b.reshape(1, -1), g.reshape(1, -1), beta.reshape(1, -1))


def _head_lane_id(shape):
    return lax.shift_right_logical(lax.broadcasted_iota(jnp.int32, shape, 1), 6)


def _ctx_attn_kernel(q_ref, k_ref, v_ref, o_ref):
    q = q_ref[0].astype(F32) * (NA_HEAD_DIM ** -0.5)
    k = k_ref[0].astype(BF16)
    v = v_ref[0].astype(BF16)
    head = _head_lane_id(q.shape)
    out = jnp.zeros(q.shape, F32)
    for h in range(NA_HEADS):
        s = lax.dot_general(jnp.where(head == h, q, 0.0).astype(BF16), k, _NT, preferred_element_type=F32)
        m = jnp.max(s, axis=-1, keepdims=True)
        p = jnp.exp(s - m)
        den = jnp.sum(p, axis=-1, keepdims=True)
        oh = jnp.dot(p.astype(BF16), v, preferred_element_type=F32)
        out = jnp.where(head == h, oh / den, out)
    o_ref[0] = out.astype(BF16)


def _ctx_attn(q, k, v):
    bsz, seq, w = q.shape
    spec = pl.BlockSpec((1, seq, w), lambda b: (b, 0, 0))
    return pl.pallas_call(
        _ctx_attn_kernel,
        grid=(bsz,),
        in_specs=[spec, spec, spec],
        out_specs=spec,
        out_shape=jax.ShapeDtypeStruct((bsz, seq, w), BF16),
        compiler_params=_cparams(1),
        name="context_attention",
    )(q, k, v)


def _na_window_start(blk, rows):
    return jnp.clip(blk * NA_QROWS - NA_ROWS // 2, 0, rows - NA_WIN_ROWS)


def _na_kernel(q_ref, k_ref, v_ref, kc_ref, vc_ref, eb_ref, o_ref, *, rows):
    blk = pl.program_id(1)
    ws = pl.multiple_of(_na_window_start(blk, rows) * GRID_W, GRID_W)
    nwin = NA_WIN_ROWS * GRID_W
    q = q_ref[0].astype(F32) * (NA_HEAD_DIM ** -0.5)
    kl = k_ref[0, pl.ds(ws, nwin), :].astype(BF16)
    vl = v_ref[0, pl.ds(ws, nwin), :].astype(BF16)
    kc = kc_ref[...].astype(BF16)
    vc = vc_ref[...].astype(BF16)
    head = _head_lane_id(q.shape)
    out = jnp.zeros(q.shape, F32)
    for h in range(NA_HEADS):
        qh = jnp.where(head == h, q, 0.0).astype(BF16)
        sl = lax.dot_general(qh, kl, _NT, preferred_element_type=F32) + eb_ref[0, h]
        sc = lax.dot_general(qh, kc, _NT, preferred_element_type=F32)
        m = jnp.maximum(jnp.max(sl, axis=-1, keepdims=True), jnp.max(sc, axis=-1, keepdims=True))
        p_l = jnp.exp(sl - m)
        p_c = jnp.exp(sc - m)
        den = jnp.sum(p_l, axis=-1, keepdims=True) + jnp.sum(p_c, axis=-1, keepdims=True)
        oh = (jnp.dot(p_l.astype(BF16), vl, preferred_element_type=F32)
              + jnp.dot(p_c.astype(BF16), vc, preferred_element_type=F32))
        out = jnp.where(head == h, oh / den, out)
    o_ref[0] = out.astype(BF16)


def _na_bias_tables(rel_bias, rows):
    n_heads = rel_bias.shape[0]
    c = np.arange(GRID_W)
    cs = np.clip(c - NA_COLS // 2, 0, GRID_W - NA_COLS)
    col_valid = (c[None, :] >= cs[:, None]) & (c[None, :] < cs[:, None] + NA_COLS)
    padw = GRID_W - NA_COLS
    padded = jnp.pad(rel_bias, ((0, 0), (0, 0), (padw, padw)))
    toeplitz = jnp.stack([padded[:, :, GRID_W - 1 - ci:2 * GRID_W - 1 - ci] for ci in range(GRID_W)], axis=2)
    toeplitz = jnp.where(jnp.asarray(col_valid)[None, None], toeplitz, NEG_INF)
    masked = jnp.full((n_heads, GRID_W, GRID_W), NEG_INF, F32)
    nblk = rows // NA_QROWS
    tables = []
    for blk in (0, 1, nblk - 1):
        ws = int(np.clip(blk * NA_QROWS - NA_ROWS // 2, 0, rows - NA_WIN_ROWS))
        q_rows = []
        for a in range(NA_QROWS):
            r = blk * NA_QROWS + a
            rs = int(np.clip(r - NA_ROWS // 2, 0, rows - NA_ROWS))
            blocks = []
            for j in range(NA_WIN_ROWS):
                kr = ws + j
                blocks.append(toeplitz[:, kr - r + NA_ROWS - 1] if rs <= kr < rs + NA_ROWS else masked)
            q_rows.append(jnp.concatenate(blocks, axis=-1))
        tables.append(jnp.concatenate(q_rows, axis=1))
    return jnp.stack(tables, axis=0)


def _na_attn(q, k, v, cache_k, cache_v, layer, eb):
    bsz, seq, w = q.shape
    rows = seq // GRID_W
    nblk = rows // NA_QROWS
    tq = NA_QROWS * GRID_W
    past = cache_k.shape[2]
    full = pl.BlockSpec((1, seq, w), lambda b, r: (b, 0, 0))
    ctx = pl.BlockSpec((None, None, past, w), lambda b, r: (b, layer, 0, 0))

    def eb_map(b, r):
        return (jnp.where(r == 0, 0, jnp.where(r == nblk - 1, 2, 1)), 0, 0, 0)

    return pl.pallas_call(
        functools.partial(_na_kernel, rows=rows),
        grid=(bsz, nblk),
        in_specs=[pl.BlockSpec((1, tq, w), lambda b, r: (b, r, 0)), full, full, ctx, ctx,
                  pl.BlockSpec((1, NA_HEADS, tq, NA_WIN_ROWS * GRID_W), eb_map)],
        out_specs=pl.BlockSpec((1, tq, w), lambda b, r: (b, r, 0)),
        out_shape=jax.ShapeDtypeStruct((bsz, seq, w), BF16),
        compiler_params=_cparams(2),
        name="neighbourhood_attention",
    )(q, k, v, cache_k, cache_v, eb)


_DN_HALO = 16
_DN_RSUB = 128
_DN_CGRP = 2 * LANES


def _dnprep_kernel(*refs, tt, nt, use_rope):
    if use_rope:
        main_ref, prev_ref, next_ref, w_ref, cos_ref, sin_ref, q_ref, k_ref, v_ref = refs
    else:
        main_ref, prev_ref, next_ref, w_ref, q_ref, k_ref, v_ref = refs
    t = pl.program_id(1)
    main = main_ref[0]
    halo = jnp.concatenate([jnp.where(t > 0, prev_ref[0], jnp.zeros_like(prev_ref[0])),
                            jnp.where(t < nt - 1, next_ref[0], jnp.zeros_like(next_ref[0]))], axis=0)
    pad = (DN_CONV_K - 1) // 2
    row = lax.broadcasted_iota(jnp.int32, (tt, tt), 0)
    col = lax.broadcasted_iota(jnp.int32, (tt, tt), 1)
    hrow = lax.broadcasted_iota(jnp.int32, (tt, 2 * _DN_HALO), 0)
    hcol = lax.broadcasted_iota(jnp.int32, (tt, 2 * _DN_HALO), 1)
    htok = jnp.where(hcol < _DN_HALO, hcol - _DN_HALO, tt + hcol - _DN_HALO)
    sel = {}
    for kk in range(DN_CONV_K):
        if kk != pad:
            sel[kk] = (jnp.where(col == row + (kk - pad), 1.0, 0.0).astype(BF16),
                       jnp.where(htok == hrow + (kk - pad), 1.0, 0.0).astype(BF16))
    outs = (q_ref, k_ref, v_ref)
    if use_rope:
        lane = lax.broadcasted_iota(jnp.int32, (_DN_RSUB, LANES), 1)
        first_half = (lane & (DN_DK // 4)) == 0
    for cg in range(3 * DN_QK_W // _DN_CGRP):
        gcols = slice(cg * _DN_CGRP, (cg + 1) * _DN_CGRP)
        taps = {}
        for kk in range(DN_CONV_K):
            if kk == pad:
                taps[kk] = main[:, gcols].astype(F32)
            else:
                taps[kk] = (jnp.dot(sel[kk][0], main[:, gcols], preferred_element_type=F32)
                            + jnp.dot(sel[kk][1], halo[:, gcols], preferred_element_type=F32))
        for sub in range(_DN_CGRP // LANES):
            cb = cg * (_DN_CGRP // LANES) + sub
            cols = slice(cb * LANES, (cb + 1) * LANES)
            lcols = slice(sub * LANES, (sub + 1) * LANES)
            for rs in range(tt // _DN_RSUB):
                rsl = slice(rs * _DN_RSUB, (rs + 1) * _DN_RSUB)
                acc = jnp.zeros((_DN_RSUB, LANES), F32)
                for kk in range(DN_CONV_K):
                    acc = acc + w_ref[kk:kk + 1, cols] * taps[kk][rsl, lcols]
                y = _silu(acc)
                if cb < 2 * DN_HEADS:
                    y = y * lax.rsqrt(jnp.sum(y * y, axis=-1, keepdims=True) + EPS)
                    if use_rope:
                        quarter = DN_DK // 4
                        partner = jnp.where(first_half, pltpu.roll(y, LANES - quarter, 1), pltpu.roll(y, quarter, 1))
                        y = y * cos_ref[rsl, :] + partner * sin_ref[rsl, :]
                    if cb < DN_HEADS:
                        y = y * (DN_DK ** -0.5)
                hh = cb % DN_HEADS
                outs[cb // DN_HEADS][0, rsl, hh * LANES:(hh + 1) * LANES] = y


def _rope_lane_tables(seq):
    pos = np.arange(seq)
    n_freq = DN_DK // 4
    inv = ROPE_BASE ** (-np.arange(n_freq) / n_freq)
    ang = np.stack([(pos // GRID_W)[:, None] * inv[None, :], (pos % GRID_W)[:, None] * inv[None, :]], axis=1)
    cos = np.cos(ang).astype(np.float32)
    sin = np.sin(ang).astype(np.float32)
    lane = np.arange(DN_DK)
    axis, half, freq = lane // (2 * n_freq), (lane // n_freq) % 2, lane % n_freq
    cos_l = cos[:, axis, freq]
    sin_l = sin[:, axis, freq] * np.where(half == 0, -1.0, 1.0).astype(np.float32)[None, :]
    return jnp.asarray(cos_l), jnp.asarray(sin_l)


def _dnprep(dqkv, conv_w, use_rope):
    bsz, seq, width = dqkv.shape
    tt = ROW_TILE
    nt = seq // tt
    prev, nxt = _halo_specs(tt, seq, _DN_HALO, width)
    in_specs = [pl.BlockSpec((1, tt, width), lambda b, t: (b, t, 0)), prev, nxt,
                pl.BlockSpec((DN_CONV_K, width), lambda b, t: (0, 0))]
    args = [dqkv, dqkv, dqkv, conv_w]
    if use_rope:
        tab = pl.BlockSpec((tt, DN_DK), lambda b, t: (t, 0))
        in_specs += [tab, tab]
        args += list(_rope_lane_tables(seq))
    out = jax.ShapeDtypeStruct((bsz, seq, DN_QK_W), F32)
    ospec = pl.BlockSpec((1, tt, DN_QK_W), lambda b, t: (b, t, 0))
    return pl.pallas_call(
        functools.partial(_dnprep_kernel, tt=tt, nt=nt, use_rope=use_rope),
        grid=(bsz, nt),
        in_specs=in_specs,
        out_specs=[ospec, ospec, ospec],
        out_shape=[out, out, out],
        compiler_params=_cparams(2),
        name="deltanet_prep",
    )(*args)


_LOG_CHUNK = int(math.log2(DN_CHUNK))
_N_GATES = 4 * DN_HEADS


def _bmm(a, b):
    return lax.dot_general(a.astype(BF16), b.astype(BF16), (((2,), (1,)), ((0,), (0,))),
                           preferred_element_type=F32)


def _bmm_nt(a, b):
    return lax.dot_general(a.astype(BF16), b.astype(BF16), (((2,), (2,)), ((0,), (0,))),
                           preferred_element_type=F32)


def _bmm_tn(a, b):
    return lax.dot_general(a.astype(BF16), b.astype(BF16), (((1,), (1,)), ((0,), (0,))),
                           preferred_element_type=F32)


def _pair_cols(col_a, col_b, width):
    lane = lax.broadcasted_iota(jnp.int32, (col_a.shape[0], width), 1)
    return jnp.where(lane < width // 2, col_a, col_b)


def _block_diag(y):
    lane = lax.broadcasted_iota(jnp.int32, y.shape, 2)
    first = lane < y.shape[2] // 2
    return jnp.concatenate([jnp.where(first, y, 0.0), jnp.where(first, 0.0, y)], axis=1)


def _chunk_cumsum(x, axis, reverse):
    n = x.shape[axis]
    pos = lax.broadcasted_iota(jnp.int32, x.shape, axis) & (DN_CHUNK - 1)
    step = 1
    while step < DN_CHUNK:
        if reverse:
            x = x + jnp.where(pos < DN_CHUNK - step, pltpu.roll(x, n - step, axis), 0.0)
        else:
            x = x + jnp.where(pos >= step, pltpu.roll(x, step, axis), 0.0)
        step *= 2
    return x


def _dn_tile_pair(fwd, bwd, par, of_ref, ob_ref, s_scr):
    alog_ref, dtb_ref, alogt_ref, dtbt_ref = par
    cs = DN_CHUNK
    pw = 2 * cs
    hp_n = DN_HEADS // 2
    tile = of_ref.shape[1]
    n_chunks = tile // cs
    npd = n_chunks * hp_n
    ii = lax.broadcasted_iota(jnp.int32, (cs, pw), 0)
    jj = lax.broadcasted_iota(jnp.int32, (cs, pw), 1) & (cs - 1)
    sx = jnp.stack([jnp.where(ii >= jj, ii ^ jj, -1), jnp.where(ii <= jj, ii ^ jj, -1)], axis=0)[:, None]
    incl = sx >= 0
    before = sx > 0

    gc_l, gct_l, beta_l, gc2_l, beta2_l, tot2_l, q_l, k_l, v_l = ([] for _ in range(9))
    for d, (q_ref, k_ref, v_ref, ab_ref, abt_ref) in enumerate((fwd, bwd)):
        ab = ab_ref[0]
        abt = abt_ref[0]
        g_col = -jnp.exp(alog_ref[...]) * _softplus(ab + dtb_ref[...])
        g_row = -jnp.exp(alogt_ref[...]) * _softplus(abt + dtbt_ref[...])
        beta_all = _sigmoid(ab)
        gc_rows = jnp.concatenate([_chunk_cumsum(g_row[:, i * LANES:(i + 1) * LANES], 1, d == 1)
                                   for i in range(tile // LANES)], axis=1)
        qx = q_ref[0]
        kx = k_ref[0]
        vx = v_ref[0]
        for c in range(n_chunks):
            rows = slice(c * cs, (c + 1) * cs)
            gc_c = _chunk_cumsum(g_col[rows], 0, d == 1)
            gct_c = gc_rows[:, rows]
            last = 0 if d == 1 else cs - 1
            tot_c = gc_c[last:last + 1, :]
            for hp in range(hp_n):
                ga = d * DN_HEADS + 2 * hp
                gb = ga + 1
                ba = ga + 2 * DN_HEADS
                bb = ba + 1
                gc_l.append(_pair_cols(gc_c[:, ga:ga + 1], gc_c[:, gb:gb + 1], pw))
                gct_l.append(jnp.concatenate([gct_c[ga:ga + 1, :], gct_c[gb:gb + 1, :]], axis=1))
                beta_l.append(_pair_cols(beta_all[rows, ba:ba + 1], beta_all[rows, bb:bb + 1], pw))
                gc2_l.append(_pair_cols(gc_c[:, ga:ga + 1], gc_c[:, gb:gb + 1], 2 * LANES))
                beta2_l.append(_pair_cols(beta_all[rows, ba:ba + 1], beta_all[rows, bb:bb + 1], 2 * LANES))
                tot2_l.append(_pair_cols(tot_c[:, ga:ga + 1], tot_c[:, gb:gb + 1], 2 * LANES))
                cols = slice(hp * 2 * LANES, (hp + 1) * 2 * LANES)
                q_l.append(qx[rows, cols])
                k_l.append(kx[rows, cols])
                v_l.append(vx[rows, cols])

    def stk(xs):
        return jnp.stack(xs, axis=0)

    gc, gc_t, beta = stk(gc_l), stk(gct_l), stk(beta_l)
    gc2, beta2, tot2 = stk(gc2_l), stk(beta2_l), stk(tot2_l)
    q, k, v = stk(q_l), stk(k_l), stk(v_l)

    def by_dir(x):
        return x.reshape((2, npd) + x.shape[1:])

    def flat(x):
        return x.reshape((2 * npd,) + x.shape[2:])

    k_bd = _block_diag(k)
    kk = by_dir(_bmm_nt(k, k_bd))
    qk = by_dir(_bmm_nt(q, k_bd))
    decay = jnp.where(incl, jnp.exp(jnp.where(incl, by_dir(gc - gc_t), 0.0)), 0.0)
    a_mat = jnp.where(before, (by_dir(beta) * kk) * decay, 0.0)
    a_qk = flat(qk * decay)

    e_mat = flat(-jnp.where(sx == 1, a_mat, 0.0))
    for lb in range(1, _LOG_CHUNK):
        l_b = flat(jnp.where(lax.shift_right_arithmetic(sx, lb) == 1, a_mat, 0.0))
        p_mat = l_b + _bmm(l_b, _block_diag(e_mat))
        e_mat = e_mat - p_mat - _bmm(e_mat, _block_diag(p_mat))

    eg2 = jnp.exp(gc2)
    vb = v * beta2
    kb = k * (beta2 * eg2)
    zero2 = jnp.zeros(vb.shape, F32)
    rhs = jnp.concatenate([vb[..., :LANES], kb[..., :LANES], vb[..., LANES:], kb[..., LANES:]], axis=-1)
    rhs_bd = jnp.concatenate([jnp.concatenate([vb[..., :LANES], kb[..., :LANES], zero2], axis=-1),
                              jnp.concatenate([zero2, vb[..., LANES:], kb[..., LANES:]], axis=-1)], axis=1)
    sol = rhs + _bmm(e_mat, rhs_bd)
    q_dec = q * eg2
    k_dec = k * jnp.exp(tot2 - gc2)
    c_dec2 = jnp.exp(tot2)

    def heads(x, chunks, off, stride):
        out = []
        for d in range(2):
            for hp in range(hp_n):
                p = d * npd + chunks[d] * hp_n + hp
                out += [x[p][:, off:off + LANES], x[p][:, off + stride:off + stride + LANES]]
        return jnp.stack(out, axis=0)

    s = s_scr[...].reshape(2 * DN_HEADS, DN_DK, DN_DV)
    zero_u = jnp.zeros((cs, LANES), F32)
    for step in range(n_chunks):
        chunks = (step, n_chunks - 1 - step)
        u_base = heads(sol, chunks, 0, 2 * LANES)
        w_dec = heads(sol, chunks, LANES, 2 * LANES)
        qd = heads(q_dec, chunks, 0, LANES)
        kd = heads(k_dec, chunks, 0, LANES)
        cd = heads(c_dec2, chunks, 0, LANES)[:, :, :1]
        ws_qs = _bmm(jnp.concatenate([w_dec, qd], axis=1), s)
        u = u_base - ws_qs[:, :cs]
        u_bd = jnp.stack([jnp.concatenate([jnp.concatenate([u[2 * i], zero_u], axis=-1),
                                           jnp.concatenate([zero_u, u[2 * i + 1]], axis=-1)], axis=0)
                          for i in range(DN_HEADS)], axis=0)
        aqk_sel = jnp.stack([a_qk[d * npd + chunks[d] * hp_n + hp] for d in range(2) for hp in range(hp_n)], axis=0)
        o_loc = _bmm(aqk_sel, u_bd)
        s = s * cd + _bmm_tn(kd, u)
        for d, o_ref in enumerate((of_ref, ob_ref)):
            c = chunks[d]
            for h in range(DN_HEADS):
                i = d * DN_HEADS + h
                o_ref[0, c * cs:(c + 1) * cs, h * LANES:(h + 1) * LANES] = (
                    ws_qs[i, cs:] + o_loc[i // 2][:, (h % 2) * LANES:(h % 2 + 1) * LANES]).astype(o_ref.dtype)
    s_scr[...] = s.reshape(2, DN_HEADS, DN_DK, DN_DV)


def _dnscan_kernel(*refs, nt, has_s0, want_state):
    it = iter(refs)
    fwd = [next(it) for _ in range(5)]
    bwd = [next(it) for _ in range(5)]
    par = [next(it) for _ in range(4)]
    s0_ref = next(it) if has_s0 else None
    of_ref = next(it)
    ob_ref = next(it)
    st_ref = next(it) if want_state else None
    s_scr = next(it)
    t = pl.program_id(1)

    @pl.when(t == 0)
    def _():
        if has_s0:
            s_scr[...] = s0_ref[0]
        else:
            s_scr[...] = jnp.zeros(s_scr.shape, F32)

    _dn_tile_pair(fwd, bwd, par, of_ref, ob_ref, s_scr)

    if want_state:
        @pl.when(t == nt - 1)
        def _():
            st_ref[0] = s_scr[...]


def _dnscan(q, k, v, ab, abt, a_log, dt_bias, s0, want_state):
    bsz, seq, _ = q.shape
    tile = min(DN_TILE, seq)
    nt = seq // tile

    def tile_specs(tmap):
        qs = pl.BlockSpec((1, tile, DN_V_W), lambda b, t: (b, tmap(t), 0))
        return [qs, qs, qs,
                pl.BlockSpec((1, tile, _N_GATES), lambda b, t: (b, tmap(t), 0)),
                pl.BlockSpec((1, _N_GATES, tile), lambda b, t: (b, 0, tmap(t)))]

    zeros8 = jnp.zeros((2 * DN_HEADS,), F32)
    alog16 = jnp.concatenate([a_log.reshape(-1), zeros8])
    dtb16 = jnp.concatenate([dt_bias.reshape(-1), zeros8])
    prow = pl.BlockSpec((1, _N_GATES), lambda b, t: (0, 0))
    pcol = pl.BlockSpec((_N_GATES, 1), lambda b, t: (0, 0))
    in_specs = tile_specs(lambda t: t) + tile_specs(lambda t: nt - 1 - t) + [prow, prow, pcol, pcol]
    args = [q, k, v, ab, abt, q, k, v, ab, abt,
            alog16.reshape(1, _N_GATES), dtb16.reshape(1, _N_GATES),
            alog16.reshape(_N_GATES, 1), dtb16.reshape(_N_GATES, 1)]
    st_block = (1, 2, DN_HEADS, DN_DK, DN_DV)
    if s0 is not None:
        in_specs.append(pl.BlockSpec(st_block, lambda b, t: (b, 0, 0, 0, 0)))
        args.append(s0)
    o_shape = jax.ShapeDtypeStruct((bsz, seq, DN_V_W), BF16)
    out_specs = [pl.BlockSpec((1, tile, DN_V_W), lambda b, t: (b, t, 0)),
                 pl.BlockSpec((1, tile, DN_V_W), lambda b, t: (b, nt - 1 - t, 0))]
    out_shape = [o_shape, o_shape]
    if want_state:
        out_specs.append(pl.BlockSpec(st_block, lambda b, t: (b, 0, 0, 0, 0)))
        out_shape.append(jax.ShapeDtypeStruct((bsz, 2, DN_HEADS, DN_DK, DN_DV), F32))
    return pl.pallas_call(
        functools.partial(_dnscan_kernel, nt=nt, has_s0=s0 is not None, want_state=want_state),
        grid=(bsz, nt),
        in_specs=in_specs,
        out_specs=out_specs,
        out_shape=out_shape,
        scratch_shapes=[pltpu.VMEM((2, DN_HEADS, DN_DK, DN_DV), F32)],
        compiler_params=_cparams(2),
        name="deltanet_scan",
    )(*args)


def _first_argmax(x, lane, valid):
    xm = jnp.where(valid, x, -jnp.inf)
    m = jnp.max(xm, axis=-1, keepdims=True)
    idx = jnp.min(jnp.where(valid & (xm == m), lane, float(LANES)), axis=-1, keepdims=True)
    return m, idx


def _route(h, rw_hi_ref, rw_lo_ref, rb_ref):
    h_hi = h.astype(BF16)
    h_lo = (h - h_hi.astype(F32)).astype(BF16)
    both = jnp.dot(h_hi, jnp.concatenate([rw_hi_ref[...], rw_lo_ref[...]], axis=1), preferred_element_type=F32)
    logits = (both[:, :LANES] + both[:, LANES:]
              + jnp.dot(h_lo, rw_hi_ref[...], preferred_element_type=F32)) + rb_ref[...]
    lane = lax.broadcasted_iota(jnp.int32, logits.shape, 1).astype(F32)
    is_grp = lane < N_GROUPS
    gmax, g_sel = _first_argmax(logits, lane, is_grp)
    pg_sel = 1.0 / jnp.sum(jnp.where(is_grp, jnp.exp(logits - gmax), 0.0), axis=-1, keepdims=True)
    e_lane = lane - N_GROUPS
    in_grp = (e_lane >= g_sel * EXPERTS_PER_GROUP) & (e_lane < (g_sel + 1) * EXPERTS_PER_GROUP)
    m1, i1 = _first_argmax(logits, lane, in_grp)
    m2, i2 = _first_argmax(logits, lane, in_grp & (lane != i1))
    e2 = jnp.exp(m2 - m1)
    w1 = pg_sel / (1.0 + e2)
    w2 = pg_sel * e2 / (1.0 + e2)
    return g_sel, jnp.where(lane == i1, w1, 0.0) + jnp.where(lane == i2, w2, 0.0)


def _mix_out(x_ref, a_ref, b_ref, of_ref, ob_ref, z_ref, ng_ref, ga_ref, w_ref):
    o = of_ref[...].astype(F32) + ob_ref[...].astype(F32)
    z = z_ref[...].astype(F32)
    mixed = [a_ref[...], b_ref[...]]
    for h in range(DN_HEADS):
        cols = slice(h * DN_DV, (h + 1) * DN_DV)
        oh = o[:, cols]
        y = oh * lax.rsqrt(jnp.mean(oh * oh, axis=-1, keepdims=True) + EPS) * ng_ref[...] * _silu(z[:, cols])
        mixed.append(y.astype(BF16))
    acc = jnp.dot(jnp.concatenate(mixed, axis=-1), w_ref[...], preferred_element_type=F32)
    return x_ref[...] + ga_ref[0] * acc


def _moe_kernel(x_ref, a_ref, b_ref, of_ref, ob_ref, z_ref, ng_ref, ga1_ref, wout_ref,
                g_ref, sc_ref, sh_ref, ga_ref, rw_hi_ref, rw_lo_ref, rb_ref, tri_ref, w1_ref, w3_ref, w2_ref,
                gf_ref, o_ref, h_scr, ghi_scr, glo_scr, oh_scr, rank_scr, oht_scr, rankt_scr, cnt_scr,
                *, final_norm):
    grp = pl.program_id(1)
    tm = x_ref.shape[0]

    @pl.when(grp == 0)
    def _():
        x_new = _mix_out(x_ref, a_ref, b_ref, of_ref, ob_ref, z_ref, ng_ref, ga1_ref, wout_ref)
        o_ref[...] = x_new
        h = _mod_norm(x_new, g_ref[...], sc_ref[0], sh_ref[0])
        h_scr[...] = h.astype(BF16)
        g_sel, gates = _route(h, rw_hi_ref, rw_lo_ref, rb_ref)
        g_hi = gates.astype(BF16)
        ghi_scr[...] = g_hi
        glo_scr[...] = (gates - g_hi.astype(F32)).astype(BF16)
        lane = lax.broadcasted_iota(jnp.int32, (tm, LANES), 1).astype(F32)
        one_hot = jnp.where(lane == g_sel, 1.0, 0.0)
        oh_b = one_hot.astype(BF16)
        oh_scr[...] = one_hot
        rank_scr[...] = jnp.dot(tri_ref[...], oh_b, preferred_element_type=F32)
        pick = jnp.where(lax.broadcasted_iota(jnp.int32, (SUBLANES, LANES), 0)
                         == lax.broadcasted_iota(jnp.int32, (SUBLANES, LANES), 1), 1.0, 0.0).astype(BF16)
        oh_t = lax.dot_general(pick, oh_b, _NT, preferred_element_type=F32)
        oht_scr[...] = oh_t
        rankt_scr[...] = lax.dot_general(oh_t.astype(BF16), tri_ref[...], _NT, preferred_element_type=F32)
        for k in range(N_GROUPS):
            cnt_scr[k] = jnp.sum(one_hot[:, k:k + 1]).astype(jnp.int32)

    hb = h_scr[...]
    lane = lax.broadcasted_iota(jnp.int32, (tm, LANES), 1)
    in_col = jnp.sum(jnp.where(lane == grp, oh_scr[...], 0.0), axis=-1, keepdims=True)
    rank_col = jnp.sum(jnp.where(lane == grp, rank_scr[...], 0.0), axis=-1, keepdims=True)
    in_row = oht_scr[pl.ds(grp, 1), :]
    rank_row = rankt_scr[pl.ds(grp, 1), :]
    def chunk(first_rank, n_rows):
        slot_r = lax.broadcasted_iota(jnp.int32, (n_rows, tm), 0).astype(F32)
        slot_c = lax.broadcasted_iota(jnp.int32, (tm, n_rows), 1).astype(F32)
        glane = lax.broadcasted_iota(jnp.int32, (n_rows, LANES), 1)
        base = first_rank.astype(F32)
        take = jnp.where((in_row > 0.5) & (rank_row - base == slot_r), 1.0, 0.0).astype(BF16)
        put = jnp.where((in_col > 0.5) & (rank_col - base == slot_c), 1.0, 0.0).astype(BF16)
        xs = jnp.dot(take, hb, preferred_element_type=F32).astype(BF16)
        gates = (jnp.dot(take, ghi_scr[...], preferred_element_type=F32)
                 + jnp.dot(take, glo_scr[...], preferred_element_type=F32))
        y = None
        for half in range(EXPERTS_PER_GROUP // 2):
            hid = []
            for j in (2 * half, 2 * half + 1):
                gate_e = jnp.sum(jnp.where(glane == N_GROUPS + grp * EXPERTS_PER_GROUP + j, gates, 0.0),
                                 axis=-1, keepdims=True)
                up = _silu(jnp.dot(xs, w1_ref[0, j], preferred_element_type=F32)) * jnp.dot(
                    xs, w3_ref[0, j], preferred_element_type=F32)
                hid.append((up * gate_e).astype(BF16))
            rows = slice(2 * half * D_EXPERT, 2 * (half + 1) * D_EXPERT)
            part = jnp.dot(jnp.concatenate(hid, axis=-1), w2_ref[0, 0, rows, :], preferred_element_type=F32)
            y = part if y is None else y + part
        o_ref[...] += ga_ref[0] * jnp.dot(put, y.astype(BF16), preferred_element_type=F32)

    count = cnt_scr[grp]
    rest = lax.rem(count, MOE_ROWS)
    n_full = count // MOE_ROWS + (rest > MOE_ROWS // 2).astype(jnp.int32)

    def full_chunk(c, carry):
        chunk(c * MOE_ROWS, MOE_ROWS)
        return carry

    lax.fori_loop(0, n_full, full_chunk, 0)

    @pl.when((rest > 0) & (rest <= MOE_ROWS // 2))
    def _():
        chunk(n_full * MOE_ROWS, MOE_ROWS // 2)

    if final_norm:
        @pl.when(grp == N_GROUPS - 1)
        def _():
            y = o_ref[...]
            o_ref[...] = y * lax.rsqrt(jnp.mean(y * y, axis=-1, keepdims=True) + EPS) * gf_ref[...]


def _mix_moe(x, out_a, out_b, o_f, o_b, z, norm_g, gate1, w_out_b,
             g, scale, shift, gate, rw_hi, rw_lo, rb, w1_b, w3_b, w2_b, g_final, layer, final_norm):
    bsz, seq, d = x.shape
    n = bsz * seq
    x2 = x.reshape(n, d)
    flat = [a.reshape(n, a.shape[-1]) for a in (out_a, out_b, o_f, o_b, z)]
    if scale.shape[0] == 1:
        tm = MOE_TILE
        mm = lambda i, e: (0, 0, 0)
    else:
        tm = min(MOE_TILE, seq)
        per_seq = seq // tm
        mm = lambda i, e: (i // per_seq, 0, 0)
    tri = jnp.asarray(np.tril(np.ones((tm, tm), np.float32), -1), BF16)
    vec = pl.BlockSpec((1, d), lambda i, e: (0, 0))
    mod = pl.BlockSpec((1, 1, d), mm)
    rspec = pl.BlockSpec((d, LANES), lambda i, e: (0, 0))
    gw = EXPERTS_PER_GROUP * D_EXPERT
    out = pl.pallas_call(
        functools.partial(_moe_kernel, final_norm=final_norm),
        grid=(bsz * seq // tm, N_GROUPS),
        in_specs=[pl.BlockSpec((tm, d), lambda i, e: (i, 0))]
                 + [pl.BlockSpec((tm, a.shape[-1]), lambda i, e: (i, 0)) for a in flat]
                 + [pl.BlockSpec((1, DN_DV), lambda i, e: (0, 0)), mod,
                    pl.BlockSpec((None, MIX_W, d), lambda i, e: (layer, 0, 0)),
                    vec, mod, mod, mod,
                  rspec, rspec, pl.BlockSpec((1, LANES), lambda i, e: (0, 0)),
                  pl.BlockSpec((tm, tm), lambda i, e: (0, 0)),
                  pl.BlockSpec((None, 1, EXPERTS_PER_GROUP, d, D_EXPERT), lambda i, e: (layer, e, 0, 0, 0)),
                  pl.BlockSpec((None, 1, EXPERTS_PER_GROUP, d, D_EXPERT), lambda i, e: (layer, e, 0, 0, 0)),
                  pl.BlockSpec((1, 1, gw, d), lambda i, e: (layer, e, 0, 0)),
                  vec],
        out_specs=pl.BlockSpec((tm, d), lambda i, e: (i, 0)),
        out_shape=jax.ShapeDtypeStruct((bsz * seq, d), F32),
        scratch_shapes=[pltpu.VMEM((tm, d), BF16), pltpu.VMEM((tm, LANES), BF16), pltpu.VMEM((tm, LANES), BF16),
                        pltpu.VMEM((tm, LANES), F32), pltpu.VMEM((tm, LANES), F32),
                        pltpu.VMEM((SUBLANES, tm), F32), pltpu.VMEM((SUBLANES, tm), F32),
                        pltpu.SMEM((N_GROUPS,), jnp.int32)],
        compiler_params=pltpu.CompilerParams(dimension_semantics=("arbitrary", "arbitrary"),
                                             vmem_limit_bytes=MOE_VMEM_LIMIT),
        name="moe",
    )(x2, *flat, norm_g.reshape(1, -1), gate1, w_out_b,
      g, scale, shift, gate, rw_hi, rw_lo, rb, tri, w1_b, w3_b, w2_b, g_final.reshape(1, -1))
    return out.reshape(bsz, seq, d)


def _split_hi_lo(w):
    hi = w.astype(BF16)
    return hi, (w - hi.astype(F32)).astype(BF16)


def kernel(x_prompt, x_sample, cache_na_k, cache_na_v, state_delta, c, c_ctx, w_mod, b_mod, g_norm1, g_norm2,
           w_in, conv_a_w, conv_a_b, ln_a_g, ln_a_b, na_rel_bias, dn_conv_w, dn_a_log, dn_dt_bias, dn_norm_g,
           w_out, router_wg, router_bg, router_we, router_be, w1, w3, w2, g_final):
    n_dec = x_sample.shape[0]
    d = D_MODEL
    cond8 = jnp.concatenate([c_ctx[None, :], c, jnp.zeros((8 - 1 - n_dec, d), F32)], axis=0)
    mods = _adaln(cond8, w_mod, b_mod).reshape(DEPTH, 8, 6, d)

    w_in_b = w_in[:, :, :OFF_DN_AB].astype(BF16)
    w_abt_b = jnp.swapaxes(w_in[:, :, OFF_DN_AB:], 1, 2).astype(BF16)
    w_out_b = w_out.astype(BF16)
    w1_b = w1.astype(BF16).reshape(DEPTH, N_GROUPS, EXPERTS_PER_GROUP, d, D_EXPERT)
    w3_b = w3.astype(BF16).reshape(DEPTH, N_GROUPS, EXPERTS_PER_GROUP, d, D_EXPERT)
    w2_b = w2.astype(BF16).reshape(DEPTH, N_GROUPS, EXPERTS_PER_GROUP * D_EXPERT, d)
    pad = LANES - N_GROUPS - N_EXPERTS
    rw = jnp.concatenate([router_wg, router_we, jnp.zeros((DEPTH, d, pad), F32)], axis=-1)
    rw_hi, rw_lo = _split_hi_lo(rw)
    rb = jnp.concatenate([router_bg, router_be, jnp.zeros((DEPTH, pad), F32)], axis=-1)

    past = cache_na_k.shape[2]
    cache_k = cache_na_k.reshape(n_dec, DEPTH, past, NA_W)
    cache_v = cache_na_v.reshape(n_dec, DEPTH, past, NA_W)
    rows = x_sample.shape[1] // GRID_W

    def layer(x, l, is_ctx):
        m = mods[l, 0:1] if is_ctx else mods[l, 1:1 + n_dec]
        sh1, sc1, ga1, sh2, sc2, ga2 = (m[:, i:i + 1, :] for i in range(6))
        glu, q_na, k_na, v_na, dqkv, dz, ab, abt = _inproj(x, g_norm1[l:l + 1], sc1, sh1, w_in_b, w_abt_b, l)
        out_a = _cconv(glu, conv_a_w[l], conv_a_b[l], ln_a_g[l], ln_a_b[l])
        if is_ctx:
            out_b = _ctx_attn(q_na, k_na, v_na)
        else:
            out_b = _na_attn(q_na, k_na, v_na, cache_k, cache_v, l, _na_bias_tables(na_rel_bias[l], rows))
        dq, dk, dv = _dnprep(dqkv, dn_conv_w[l], use_rope=not is_ctx)
        scan = _dnscan(dq, dk, dv, ab, abt, dn_a_log[l], dn_dt_bias[l],
                       None if is_ctx else state_delta[:, l], want_state=is_ctx)
        o_f, o_b = scan[0], scan[1]
        x = _mix_moe(x, out_a, out_b, o_f, o_b, dz, dn_norm_g[l], ga1, w_out_b,
                     g_norm2[l:l + 1], sc2, sh2, ga2, rw_hi[l], rw_lo[l], rb[l:l + 1], w1_b, w3_b, w2_b,
                     g_final, layer=l, final_norm=(l == DEPTH - 1))
        return x, k_na, v_na, (scan[2] if is_ctx else None)

    xc = x_prompt
    new_k, new_v, new_s = [], [], []
    for l in range(DEPTH):
        xc, k_na, v_na, s_ctx = layer(xc, l, True)
        new_k.append(k_na.reshape(k_na.shape[0], k_na.shape[1], NA_HEADS, NA_HEAD_DIM))
        new_v.append(v_na.reshape(v_na.shape[0], v_na.shape[1], NA_HEADS, NA_HEAD_DIM))
        new_s.append(s_ctx)
    xs = x_sample
    for l in range(DEPTH):
        xs, _, _, _ = layer(xs, l, False)
    return (xc, xs, jnp.stack(new_k, axis=1), jnp.stack(new_v, axis=1), jnp.stack(new_s, axis=1))
```
